```python
import math
import jax
import jax.numpy as jnp
from jax import lax
import numpy as np

D_MODEL = 1024
BATCH = 32
SEQ = 256
DEPTH = 4
DEC_BATCH = 2
DEC_SEQ = 4096
PAST_LEN = 256

GRID_W = 64
EPS = 1e-6
MIX_WIDTH = D_MODEL
POOL_WIDTH = D_MODEL // 4
POOL_GROUPS = 4
POOL_GROUP_DIM = POOL_WIDTH // POOL_GROUPS
POOL_WINDOWS = (2, 4, 8, 16)
GLA_HEADS = 4
GLA_DV = 64
GLA_DK = 32
GLA_WIDTH = GLA_HEADS * GLA_DV
GLA_GATE_RANK = 16
GLA_GATE_TAU = 16.0
GLA_CHUNK = 64
ATTN_HEADS = 8
ATTN_KV_HEADS = 2
ATTN_GROUP = ATTN_HEADS // ATTN_KV_HEADS
HEAD_DIM = 64
ATTN_WIDTH = ATTN_HEADS * HEAD_DIM
WINDOW = 128
ATTN_BLOCK = 128
ROPE_THETA = 10000.0
NEG_INF = -1e30
PEER_HEADS = 8
PEER_N_KEYS = 128
PEER_N_EXPERTS = PEER_N_KEYS * PEER_N_KEYS
PEER_QUERY_DIM = 128
PEER_TOPK = 16
PEER_TOKEN_BLOCK = 128
IN_SIZES = (POOL_WIDTH, GLA_HEADS * GLA_DK, GLA_HEADS * GLA_DK, GLA_WIDTH, GLA_WIDTH, GLA_GATE_RANK, GLA_GATE_RANK, ATTN_HEADS * HEAD_DIM, ATTN_KV_HEADS * HEAD_DIM, ATTN_KV_HEADS * HEAD_DIM)
IN_WIDTH = 1824

kernel_name = 'hybrid_pool_gla_swa_peer_dit_step'


def _rmsnorm(x, g):
    xf = x.astype(jnp.float32)
    y = xf * lax.rsqrt(jnp.mean(xf * xf, axis=-1, keepdims=True) + EPS)
    return (y * g.astype(jnp.float32)).astype(x.dtype)


def _split_proj(p):
    cuts = np.cumsum(IN_SIZES)[:-1].tolist()
    return jnp.split(p, cuts, axis=-1)


def _pool_mix(x, w, scale):
    B, T, _ = x.shape
    xg = x.reshape(B, T, POOL_GROUPS, POOL_GROUP_DIM)
    cs = jnp.concatenate([jnp.zeros((B, 1, POOL_GROUPS, POOL_GROUP_DIM), jnp.float32),
                          jnp.cumsum(xg.astype(jnp.float32), axis=1)], axis=1)
    t = jnp.arange(T)
    diffs = []
    for g, win in enumerate(POOL_WINDOWS):
        left = win // 2
        right = win - 1 - left
        lo = jnp.maximum(t - left, 0)
        hi = jnp.minimum(t + right + 1, T)
        csg = cs[:, :, g]
        mean = (csg[:, hi] - csg[:, lo]) / (hi - lo).astype(jnp.float32)[None, :, None]
        diffs.append(mean.astype(x.dtype) - xg[:, :, g])
    d = jnp.stack(diffs, axis=2)
    y = jnp.einsum('btgc,gcd->btgd', d, w)
    return y.reshape(B, T, POOL_WIDTH) * scale


def _gla_log_decay(z, w2, b2):
    B, T, _ = z.shape
    logit = (z @ w2 + b2).astype(jnp.float32)
    return (jax.nn.log_sigmoid(logit) / GLA_GATE_TAU).reshape(B, T, GLA_HEADS, GLA_DK)


def _gla_scan(q, k, v, log_a, s0):
    B, T, H, dk = q.shape
    dv = v.shape[-1]
    C = GLA_CHUNK
    nc = T // C

    def chunks(a):
        return a.astype(jnp.float32).reshape(B, nc, C, H, a.shape[-1]).transpose(1, 0, 3, 2, 4)

    qc, kc, vc, ac = chunks(q), chunks(k), chunks(v), chunks(log_a)
    causal = jnp.tril(jnp.ones((C, C), dtype=bool))[:, :, None]

    def step(S, inp):
        qb, kb, vb, ab = inp
        b = jnp.cumsum(ab, axis=2)
        inter = jnp.einsum('bhtd,bhde->bhte', qb * jnp.exp(b), S)
        diff = b[:, :, :, None, :] - b[:, :, None, :, :]
        decay = jnp.where(causal, jnp.exp(jnp.where(causal, diff, 0.0)), 0.0)
        scores = jnp.einsum('bhtd,bhsd,bhtsd->bhts', qb, kb, decay)
        intra = jnp.einsum('bhts,bhse->bhte', scores, vb)
        b_last = b[:, :, -1:, :]
        S_new = jnp.exp(b_last[:, :, 0, :])[..., None] * S + jnp.einsum('bhsd,bhse->bhde', kb * jnp.exp(b_last - b), vb)
        return S_new, inter + intra

    S, o = lax.scan(step, s0.astype(jnp.float32), (qc, kc, vc, ac))
    o = o.transpose(1, 0, 3, 2, 4).reshape(B, T, H, dv)
    return o.astype(v.dtype), S.astype(s0.dtype)


def _gla_bidir(q, k, v, la_f, la_b, s0_f, s0_b):
    o_f, s_f = _gla_scan(q, k, v, la_f, s0_f)
    rev = lambda a: jnp.flip(a, axis=1)
    o_b, s_b = _gla_scan(rev(q), rev(k), rev(v), rev(la_b), s0_b)
    return o_f + rev(o_b), s_f, s_b


def _gla_output(o, g, norm_g):
    B, T, H, dv = o.shape
    of = o.astype(jnp.float32)
    of = of * lax.rsqrt(jnp.mean(of * of, axis=-1, keepdims=True) + EPS)
    of = of * norm_g.reshape(H, dv).astype(jnp.float32)
    return of.reshape(B, T, H * dv).astype(o.dtype) * jax.nn.silu(g)


def _axial_rope(x, pos_row, pos_col):
    half = HEAD_DIM // 2
    nf = half // 2
    freqs = ROPE_THETA ** (-jnp.arange(nf, dtype=jnp.float32) / nf)

    def rot(xa, pos):
        ang = pos.astype(jnp.float32)[:, None] * freqs[None, :]
        cos = jnp.cos(ang)[None, :, None, :]
        sin = jnp.sin(ang)[None, :, None, :]
        x1, x2 = xa[..., :nf], xa[..., nf:]
        return jnp.concatenate([x1 * cos - x2 * sin, x1 * sin + x2 * cos], axis=-1)

    xf = x.astype(jnp.float32)
    return jnp.concatenate([rot(xf[..., :half], pos_row), rot(xf[..., half:], pos_col)], axis=-1).astype(x.dtype)


def _context_attention(q, k, v, sink):
    B, T, _, _ = q.shape
    nqb = T // ATTN_BLOCK
    scale = HEAD_DIM ** -0.5
    qb = q.reshape(B, nqb, ATTN_BLOCK, ATTN_KV_HEADS, ATTN_GROUP, HEAD_DIM).transpose(1, 0, 2, 3, 4, 5)
    sink_b = sink.reshape(ATTN_KV_HEADS, ATTN_GROUP).astype(jnp.float32)

    def one(qblk):
        s = jnp.einsum('bqgrd,bkgd->bgrqk', qblk, k).astype(jnp.float32) * scale
        sk = jnp.broadcast_to(sink_b[None, :, :, None, None], s.shape[:-1] + (1,))
        p = jax.nn.softmax(jnp.concatenate([s, sk], axis=-1), axis=-1)[..., :-1]
        return jnp.einsum('bgrqk,bkgd->bqgrd', p.astype(v.dtype), v)

    o = lax.map(one, qb)
    return o.transpose(1, 0, 2, 3, 4, 5).reshape(B, T, ATTN_WIDTH)


def _latent_attention(q, k, v, ck, cv, sink):
    B, T, _, _ = q.shape
    nb = T // ATTN_BLOCK
    scale = HEAD_DIM ** -0.5
    qb = q.reshape(B, nb, ATTN_BLOCK, ATTN_KV_HEADS, ATTN_GROUP, HEAD_DIM)
    pad = jnp.zeros((B, ATTN_BLOCK, ATTN_KV_HEADS, HEAD_DIM), k.dtype)

    def band(a):
        ap = jnp.concatenate([pad, a, pad], axis=1).reshape(B, nb + 2, ATTN_BLOCK, ATTN_KV_HEADS, HEAD_DIM)
        return jnp.concatenate([ap[:, :-2], ap[:, 1:-1], ap[:, 2:]], axis=2)

    kw, vw = band(k), band(v)
    blk = jnp.arange(nb)
    qpos = blk[:, None] * ATTN_BLOCK + jnp.arange(ATTN_BLOCK)[None, :]
    kpos = (blk[:, None] - 1) * ATTN_BLOCK + jnp.arange(3 * ATTN_BLOCK)[None, :]
    mask = ((jnp.abs(qpos[:, :, None] - kpos[:, None, :]) <= WINDOW)
            & (kpos >= 0)[:, None, :] & (kpos < T)[:, None, :])
    s_loc = jnp.einsum('bnqgrd,bnkgd->bngrqk', qb, kw).astype(jnp.float32) * scale
    s_loc = jnp.where(mask[None, :, None, None], s_loc, NEG_INF)
    s_ctx = jnp.einsum('bnqgrd,bkgd->bngrqk', qb, ck).astype(jnp.float32) * scale
    s_sink = jnp.broadcast_to(sink.reshape(ATTN_KV_HEADS, ATTN_GROUP).astype(jnp.float32)[None, None, :, :, None, None],
                              s_loc.shape[:-1] + (1,))
    p = jax.nn.softmax(jnp.concatenate([s_loc, s_ctx, s_sink], axis=-1), axis=-1)
    L = 3 * ATTN_BLOCK
    Lc = ck.shape[1]
    out = (jnp.einsum('bngrqk,bnkgd->bnqgrd', p[..., :L].astype(v.dtype), vw)
           + jnp.einsum('bngrqk,bkgd->bnqgrd', p[..., L:L + Lc].astype(v.dtype), cv))
    return out.reshape(B, T, ATTN_WIDTH)


def _peer(h, wq, subkeys, u, v):
    B, T, D = h.shape
    n = B * T
    hf = h.reshape(n, D)
    q = (hf @ wq).reshape(n, PEER_HEADS, 2, PEER_QUERY_DIM // 2)
    s = jnp.einsum('nhpd,hpkd->nhpk', q, subkeys).astype(jnp.float32)
    sv, si = lax.top_k(s, PEER_TOPK)
    cand = (sv[:, :, 0, :, None] + sv[:, :, 1, None, :]).reshape(n, PEER_HEADS, PEER_TOPK * PEER_TOPK)
    cand_idx = (si[:, :, 0, :, None] * PEER_N_KEYS + si[:, :, 1, None, :]).reshape(n, PEER_HEADS, PEER_TOPK * PEER_TOPK)
    top_v, top_pos = lax.top_k(cand, PEER_TOPK)
    experts = jnp.take_along_axis(cand_idx, top_pos, axis=-1)
    gates = jax.nn.softmax(top_v, axis=-1).astype(h.dtype)
    nblk = n // PEER_TOKEN_BLOCK
    xs = (hf.reshape(nblk, PEER_TOKEN_BLOCK, D),
          experts.reshape(nblk, PEER_TOKEN_BLOCK, PEER_HEADS * PEER_TOPK),
          gates.reshape(nblk, PEER_TOKEN_BLOCK, PEER_HEADS * PEER_TOPK))

    def blk_fn(args):
        xb, eb, gb = args
        act = jax.nn.gelu(jnp.einsum('td,tkd->tk', xb, u[eb]), approximate=False)
        return jnp.einsum('tk,tkd->td', gb * act, v[eb])

    y = lax.map(blk_fn, xs)
    return y.reshape(B, T, D)


def _trunk_layer(x, cvec, lp, ctx, pos):
    (w_ada, b_ada, n1, n2, w_in, pool_w, pool_scale, gla_w_f, gla_b_f, gla_w_b, gla_b_b,
     gla_norm, sink, w_out, peer_wq, peer_subkeys, peer_u, peer_v) = lp
    B, T, _ = x.shape
    sh1, sc1, gt1, sh2, sc2, gt2 = jnp.split(jax.nn.silu(cvec) @ w_ada + b_ada, 6, axis=-1)
    h = _rmsnorm(x, n1) * (1 + sc1) + sh1
    p_pool, gq, gk, gv, gg, gz_f, gz_b, aq, ak, av = _split_proj(h @ w_in)
    y_pool = _pool_mix(p_pool, pool_w, pool_scale)
    gq = gq.reshape(B, T, GLA_HEADS, GLA_DK) * GLA_DK ** -0.5
    gk = gk.reshape(B, T, GLA_HEADS, GLA_DK)
    gv = gv.reshape(B, T, GLA_HEADS, GLA_DV)
    la_f = _gla_log_decay(gz_f, gla_w_f, gla_b_f)
    la_b = _gla_log_decay(gz_b, gla_w_b, gla_b_b)
    aq = aq.reshape(B, T, ATTN_HEADS, HEAD_DIM)
    ak = ak.reshape(B, T, ATTN_KV_HEADS, HEAD_DIM)
    av = av.reshape(B, T, ATTN_KV_HEADS, HEAD_DIM)
    if ctx is None:
        s0 = jnp.zeros((B, GLA_HEADS, GLA_DK, GLA_DV), x.dtype)
        o, s_f, s_b = _gla_bidir(gq, gk, gv, la_f, la_b, s0, s0)
        y_attn = _context_attention(aq, ak, av, sink)
    else:
        ck, cv, s0_f, s0_b = ctx
        o, s_f, s_b = _gla_bidir(gq, gk, gv, la_f, la_b, s0_f, s0_b)
        pos_row, pos_col = pos
        aq = _axial_rope(aq, pos_row, pos_col)
        ak = _axial_rope(ak, pos_row, pos_col)
        y_attn = _latent_attention(aq, ak, av, ck, cv, sink)
    y_gla = _gla_output(o, gg, gla_norm)
    mix = jnp.concatenate([y_pool, y_gla, y_attn], axis=-1)
    x = x + gt1 * (mix @ w_out)
    h2 = _rmsnorm(x, n2) * (1 + sc2) + sh2
    x = x + gt2 * _peer(h2, peer_wq, peer_subkeys, peer_u, peer_v)
    return x, ak, av, s_f, s_b


def setup_inputs(seed: int = 0) -> dict:
    key = jax.random.key(seed)
    ks = jax.random.split(key, 27)
    D = D_MODEL

    def nrm(k, shape, s):
        return s * jax.random.normal(k, shape, jnp.float32)

    return {
        'x_prompt': nrm(ks[0], (BATCH, SEQ, D), 1.0),
        'x_sample': nrm(ks[1], (DEC_BATCH, DEC_SEQ, D), 1.0),
        'cache_k': nrm(ks[2], (DEC_BATCH, DEPTH, PAST_LEN, ATTN_KV_HEADS, HEAD_DIM), 1.0),
        'cache_v': nrm(ks[3], (DEC_BATCH, DEPTH, PAST_LEN, ATTN_KV_HEADS, HEAD_DIM), 1.0),
        'state_fwd': nrm(ks[4], (DEC_BATCH, DEPTH, GLA_HEADS, GLA_DK, GLA_DV), 0.5),
        'state_bwd': nrm(ks[5], (DEC_BATCH, DEPTH, GLA_HEADS, GLA_DK, GLA_DV), 0.5),
        'c': nrm(ks[6], (DEC_BATCH, D), 1.0),
        'c_ctx': nrm(ks[7], (D,), 1.0),
        'w_ada': nrm(ks[8], (DEPTH, D, 6 * D), 0.5 * D ** -0.5),
        'b_ada': nrm(ks[9], (DEPTH, 6 * D), 0.01),
        'norm1_g': 1.0 + nrm(ks[10], (DEPTH, D), 0.02),
        'norm2_g': 1.0 + nrm(ks[11], (DEPTH, D), 0.02),
        'w_in': nrm(ks[12], (DEPTH, D, IN_WIDTH), D ** -0.5),
        'pool_w': nrm(ks[13], (DEPTH, POOL_GROUPS, POOL_GROUP_DIM, POOL_GROUP_DIM), POOL_GROUP_DIM ** -0.5),
        'pool_scale': 1.0 + nrm(ks[14], (DEPTH, POOL_WIDTH), 0.1),
        'gla_gate_w_f': nrm(ks[15], (DEPTH, GLA_GATE_RANK, GLA_HEADS * GLA_DK), GLA_GATE_RANK ** -0.5),
        'gla_gate_b_f': nrm(ks[16], (DEPTH, GLA_HEADS * GLA_DK), 0.01),
        'gla_gate_w_b': nrm(ks[17], (DEPTH, GLA_GATE_RANK, GLA_HEADS * GLA_DK), GLA_GATE_RANK ** -0.5),
        'gla_gate_b_b': nrm(ks[18], (DEPTH, GLA_HEADS * GLA_DK), 0.01),
        'gla_norm_g': 1.0 + nrm(ks[19], (DEPTH, GLA_WIDTH), 0.02),
        'attn_sink': nrm(ks[20], (DEPTH, ATTN_HEADS), 0.5),
        'w_out': nrm(ks[21], (DEPTH, MIX_WIDTH, D), MIX_WIDTH ** -0.5),
        'peer_wq': nrm(ks[22], (DEPTH, D, PEER_HEADS * PEER_QUERY_DIM), D ** -0.5),
        'peer_subkeys': nrm(ks[23], (DEPTH, PEER_HEADS, 2, PEER_N_KEYS, PEER_QUERY_DIM // 2), (PEER_QUERY_DIM // 2) ** -0.5),
        'peer_u': nrm(ks[24], (DEPTH, PEER_N_EXPERTS, D), D ** -0.5),
        'peer_v': nrm(ks[25], (DEPTH, PEER_N_EXPERTS, D), D ** -0.5),
        'final_norm_g': 1.0 + nrm(ks[26], (D,), 0.02),
    }


def reference(x_prompt, x_sample, cache_k, cache_v, state_fwd, state_bwd, c, c_ctx,
              w_ada, b_ada, norm1_g, norm2_g, w_in, pool_w, pool_scale,
              gla_gate_w_f, gla_gate_b_f, gla_gate_w_b, gla_gate_b_b, gla_norm_g,
              attn_sink, w_out, peer_wq, peer_subkeys, peer_u, peer_v, final_norm_g):
    layer_params = (w_ada, b_ada, norm1_g, norm2_g, w_in, pool_w, pool_scale,
                    gla_gate_w_f, gla_gate_b_f, gla_gate_w_b, gla_gate_b_b, gla_norm_g,
                    attn_sink, w_out, peer_wq, peer_subkeys, peer_u, peer_v)
    xc = x_prompt
    c_ctx_b = c_ctx[None, None, :]
    ks_l, vs_l, sf_l, sb_l = [], [], [], []
    for l in range(DEPTH):
        lp = tuple(p[l] for p in layer_params)
        xc, k_l, v_l, s_f, s_b = _trunk_layer(xc, c_ctx_b, lp, None, None)
        ks_l.append(k_l)
        vs_l.append(v_l)
        sf_l.append(s_f)
        sb_l.append(s_b)
    y_prompt = _rmsnorm(xc, final_norm_g)
    new_cache_k = jnp.stack(ks_l, axis=1)
    new_cache_v = jnp.stack(vs_l, axis=1)
    new_state_fwd = jnp.stack(sf_l, axis=1)
    new_state_bwd = jnp.stack(sb_l, axis=1)
    n_lat = x_sample.shape[1]
    rows = n_lat // GRID_W
    pos = (jnp.repeat(jnp.arange(rows), GRID_W), jnp.tile(jnp.arange(GRID_W), rows))
    xs = x_sample
    c_b = c[:, None, :]
    for l in range(DEPTH):
        lp = tuple(p[l] for p in layer_params)
        ctx = (cache_k[:, l], cache_v[:, l], state_fwd[:, l], state_bwd[:, l])
        xs, _, _, _, _ = _trunk_layer(xs, c_b, lp, ctx, pos)
    y_sample = _rmsnorm(xs, final_norm_g)
    return (y_prompt, y_sample, new_cache_k, new_cache_v, new_state_fwd, new_state_bwd)
```

```python
import functools
import math

import numpy as np
import jax
import jax.numpy as jnp
from jax import lax
from jax.experimental import pallas as pl
from jax.experimental.pallas import tpu as pltpu

F32 = jnp.float32
BF16 = jnp.bfloat16
HIGHEST = lax.Precision.HIGHEST

D_MODEL = 1024
GRID_W = 64
EPS = 1e-6
POOL_WIDTH = 256
POOL_GROUPS = 4
POOL_GROUP_DIM = 64
POOL_WINDOWS = (2, 4, 8, 16)
GLA_HEADS = 4
GLA_DV = 64
GLA_DK = 32
GLA_QK = GLA_HEADS * GLA_DK
GLA_WIDTH = GLA_HEADS * GLA_DV
GLA_GATE_RANK = 16
GLA_GATE_TAU = 16.0
GLA_CHUNK = 64
ATTN_HEADS = 8
ATTN_KV_HEADS = 2
ATTN_GROUP = 4
HEAD_DIM = 64
ATTN_WIDTH = ATTN_HEADS * HEAD_DIM
KV_WIDTH = ATTN_KV_HEADS * HEAD_DIM
WINDOW = 128
ATTN_BLOCK = 128
ROPE_THETA = 10000.0
NEG_INF = -1e30
PEER_HEADS = 8
PEER_HALF = 64
PEER_TOPK = 16

LANES = 128
SUBLANES = 8
VMEM_LIMIT = 56 * 1024 * 1024

TOKEN_BLOCK = 512
GLA_BLOCK = 256
PEER_EXPERT_BLOCK = 512

C_POOL, C_GQ, C_GK, C_GV, C_GG, C_GZ, C_AQ, C_AK, C_AV, C_AQS, C_AKS, C_END = (
    0, 256, 384, 512, 768, 1024, 1152, 1664, 1792, 1920, 2432, 2560)


def _params(sem):
    return pltpu.CompilerParams(dimension_semantics=sem, vmem_limit_bytes=VMEM_LIMIT)


def _bdot(a, b):
    return jnp.dot(a.astype(BF16), b.astype(BF16), preferred_element_type=F32)


def _fdot(a, b):
    return jnp.dot(a, b, preferred_element_type=F32, precision=HIGHEST)


def _sigmoid(x):
    return 1.0 / (1.0 + jnp.exp(-x))


def _rms(x):
    return x * lax.rsqrt(jnp.mean(x * x, axis=-1, keepdims=True) + EPS)


def _ada_kernel(c_ref, w_ref, b_ref, o_ref):
    c = c_ref[...]
    o_ref[...] = _fdot(c * _sigmoid(c), w_ref[...]) + b_ref[...]


def _ada_mods(cvec, w_ada, b_ada):
    depth, d, six_d = w_ada.shape
    nj = six_d // d
    return pl.pallas_call(
        _ada_kernel,
        grid=(depth, nj),
        in_specs=[pl.BlockSpec((SUBLANES, d), lambda l, j: (0, 0)),
                  pl.BlockSpec((None, d, d), lambda l, j: (l, 0, j)),
                  pl.BlockSpec((None, 1, d), lambda l, j: (l, 0, j))],
        out_specs=pl.BlockSpec((None, SUBLANES, d), lambda l, j: (l, 0, j)),
        out_shape=jax.ShapeDtypeStruct((depth, SUBLANES, six_d), F32),
        compiler_params=_params(("parallel", "parallel")),
        name="ada_mod",
    )(cvec, w_ada, b_ada.reshape(depth, 1, six_d))


def _inproj_kernel(x_ref, mod_ref, n1_ref, w_ref, cos_ref, sin_ref, w2_ref, b2_ref,
                   pp_ref, gq_ref, gk_ref, gv_ref, gg_ref, laf_ref, lab_ref, aq_ref, ak_ref, av_ref):
    h = _rms(x_ref[...]) * n1_ref[...]
    h = (h * (1.0 + mod_ref[1:2, :]) + mod_ref[0:1, :]).astype(BF16)

    def proj(lo, hi):
        return jnp.dot(h, w_ref[:, lo:hi], preferred_element_type=F32)

    pp_ref[...] = proj(C_POOL, C_GQ)
    gq_ref[...] = proj(C_GQ, C_GK) * (GLA_DK ** -0.5)
    gk_ref[...] = proj(C_GK, C_GV)
    gv_ref[...] = proj(C_GV, C_GG)
    gg_ref[...] = proj(C_GG, C_GZ)
    logit = _fdot(proj(C_GZ, C_AQ), w2_ref[...]) + b2_ref[...]
    la = (jnp.minimum(logit, 0.0) - jnp.log1p(jnp.exp(-jnp.abs(logit)))) * (1.0 / GLA_GATE_TAU)
    laf_ref[...] = la[:, :GLA_QK]
    lab_ref[...] = la[:, GLA_QK:]
    cos = cos_ref[...]
    sin = sin_ref[...]
    cos4 = jnp.concatenate([cos] * (ATTN_WIDTH // LANES), axis=1)
    sin4 = jnp.concatenate([sin] * (ATTN_WIDTH // LANES), axis=1)
    aq_ref[...] = (proj(C_AQ, C_AK) * cos4 + proj(C_AQS, C_AKS) * sin4) * (HEAD_DIM ** -0.5)
    ak_ref[...] = proj(C_AK, C_AV) * cos + proj(C_AKS, C_END) * sin
    av_ref[...] = proj(C_AV, C_AQS)


def _inproj(x, modblk, n1, w_big, cos, sin, w2, b2):
    n, d = x.shape
    tm = TOKEN_BLOCK
    widths = (POOL_WIDTH, GLA_QK, GLA_QK, GLA_WIDTH, GLA_WIDTH, GLA_QK, GLA_QK, ATTN_WIDTH, KV_WIDTH, KV_WIDTH)
    row = lambda w: pl.BlockSpec((tm, w), lambda i: (i, 0))
    full = lambda a: pl.BlockSpec(a.shape, lambda i: (0,) * a.ndim)
    return pl.pallas_call(
        _inproj_kernel,
        grid=(n // tm,),
        in_specs=[row(d), pl.BlockSpec((None, 6, d), lambda i: (i, 0, 0)), full(n1), full(w_big),
                  row(LANES), row(LANES), full(w2), full(b2)],
        out_specs=[row(w) for w in widths],
        out_shape=[jax.ShapeDtypeStruct((n, w), F32) for w in widths],
        compiler_params=_params(("parallel",)),
        name="in_proj",
    )(x, modblk, n1, w_big, cos, sin, w2, b2)


def _pool_kernel(x_ref, w_ref, scale_ref, o_ref):
    t = x_ref.shape[0]
    pad = 32
    n = t + pad
    x = x_ref[...]
    xp = jnp.concatenate([x, jnp.zeros((pad, POOL_WIDTH), F32)], axis=0)
    back = {1: xp}
    for w in (2, 4, 8, 16):
        back[w] = back[w // 2] + pltpu.roll(back[w // 2], w // 2, 0)
    grp = lax.broadcasted_iota(jnp.int32, (1, POOL_WIDTH), 1) // POOL_GROUP_DIM
    tok = lax.broadcasted_iota(jnp.int32, (t, 1), 0)
    wsum = None
    cnt = None
    for g, w in enumerate(POOL_WINDOWS):
        left = w // 2
        right = w - 1 - left
        ws = back[w] if right == 0 else pltpu.roll(back[w], n - right, 0)
        ws = ws[:t]
        c = (jnp.minimum(tok + right + 1, t) - jnp.maximum(tok - left, 0)).astype(F32)
        if wsum is None:
            wsum, cnt = ws, jnp.broadcast_to(c, (t, POOL_WIDTH))
        else:
            wsum = jnp.where(grp == g, ws, wsum)
            cnt = jnp.where(grp == g, c, cnt)
    diff = wsum / cnt - x
    o_ref[...] = _bdot(diff, w_ref[...]) * scale_ref[...]


def _pool(pp, w_bd, scale, seq, first_block, nseq):
    return pl.pallas_call(
        _pool_kernel,
        grid=(nseq,),
        in_specs=[pl.BlockSpec((seq, POOL_WIDTH), lambda b: (first_block + b, 0)),
                  pl.BlockSpec(w_bd.shape, lambda b: (0, 0)),
                  pl.BlockSpec(scale.shape, lambda b: (0, 0))],
        out_specs=pl.BlockSpec((seq, POOL_WIDTH), lambda b: (b, 0)),
        out_shape=jax.ShapeDtypeStruct((nseq * seq, POOL_WIDTH), F32),
        compiler_params=_params(("parallel",)),
        name="pool_mix",
    )(pp, w_bd, scale)


def _gla_rows(reverse):
    out = []
    for s in range(GLA_CHUNK):
        g = s // SUBLANES
        out.append((0, SUBLANES * (g + 1)) if reverse else (SUBLANES * g, GLA_CHUNK))
    return out


GLA_PAIR_ROWS = sum(hi - lo for lo, hi in _gla_rows(False))


def _gla_kernel(q_ref, k_ref, v_ref, la_ref, s0_ref, hexp_ref, tri_ref, bd_ref,
                o_ref, sout_ref, st_ref, b_ref, p_ref, z_ref, *, reverse, nchunk):
    j = pl.program_id(1)

    @pl.when(j == 0)
    def _():
        st_ref[...] = s0_ref[...]

    rows = _gla_rows(reverse)
    ngrp = GLA_CHUNK // SUBLANES
    for c in (range(nchunk - 1, -1, -1) if reverse else range(nchunk)):
        r0 = c * GLA_CHUNK
        q = q_ref[r0:r0 + GLA_CHUNK, :]
        k = k_ref[r0:r0 + GLA_CHUNK, :]
        v = v_ref[r0:r0 + GLA_CHUNK, :]
        b = _fdot(tri_ref[...], la_ref[r0:r0 + GLA_CHUNK, :])
        b_ref[...] = b
        blast = b[0:1, :] if reverse else b[GLA_CHUNK - 1:GLA_CHUNK, :]
        st = st_ref[...]
        inter = lax.dot_general((q * jnp.exp(b)).astype(BF16), st.astype(BF16),
                                (((1,), (1,)), ((), ())), preferred_element_type=F32)
        off = 0
        for s, (lo, hi) in enumerate(rows):
            tio = lax.broadcasted_iota(jnp.int32, (hi - lo, 1), 0) + lo
            valid = (tio <= s) if reverse else (tio >= s)
            dlt = jnp.where(valid, b[lo:hi] - b_ref[s:s + 1, :], 0.0)
            p_ref[off:off + hi - lo, :] = jnp.where(valid, q[lo:hi] * k_ref[r0 + s:r0 + s + 1, :] * jnp.exp(dlt), 0.0)
            off += hi - lo
        z_ref[...] = jnp.dot(p_ref[...].astype(BF16), hexp_ref[...], preferred_element_type=F32)
        oacc = [None] * ngrp
        off = 0
        for g in range(ngrp):
            lo, hi = rows[g * SUBLANES]
            cg = None
            for s in range(g * SUBLANES, (g + 1) * SUBLANES):
                term = z_ref[off:off + hi - lo, :] * v_ref[r0 + s:r0 + s + 1, :]
                cg = term if cg is None else cg + term
                off += hi - lo
            for rg in range(lo // SUBLANES, hi // SUBLANES):
                piece = cg[(rg - lo // SUBLANES) * SUBLANES:(rg - lo // SUBLANES + 1) * SUBLANES]
                oacc[rg] = piece if oacc[rg] is None else oacc[rg] + piece
        o_ref[r0:r0 + GLA_CHUNK, :] = jnp.concatenate(oacc, axis=0) + inter
        ke = k * jnp.exp(blast - b)
        kv = jnp.dot(v.T.astype(BF16), ke.astype(BF16), preferred_element_type=F32)
        st_ref[...] = st * jnp.exp(blast) + kv * bd_ref[...]

    @pl.when(j == pl.num_programs(1) - 1)
    def _():
        sout_ref[...] = st_ref[...]


def _gla_consts():
    hd = np.arange(GLA_QK) // GLA_DK
    he = np.arange(GLA_WIDTH) // GLA_DV
    hexp = (hd[:, None] == he[None, :]).astype(np.float32)
    bd = hexp.T.copy()
    t = np.arange(GLA_CHUNK)
    tri_f = (t[None, :] <= t[:, None]).astype(np.float32)
    tri_b = (t[None, :] >= t[:, None]).astype(np.float32)
    return jnp.asarray(hexp, BF16), jnp.asarray(bd, F32), jnp.asarray(tri_f), jnp.asarray(tri_b)


def _gla(gq, gk, gv, la, s0t, consts, seq, first_block, nseq, reverse):
    hexp, bd, tri_f, tri_b = consts
    tri = tri_b if reverse else tri_f
    blk = min(GLA_BLOCK, seq)
    nblk = seq // blk
    if reverse:
        tok = lambda b, j: (first_block + b * nblk + (nblk - 1 - j), 0)
        otok = lambda b, j: (b * nblk + (nblk - 1 - j), 0)
    else:
        tok = lambda b, j: (first_block + b * nblk + j, 0)
        otok = lambda b, j: (b * nblk + j, 0)
    const = lambda a: pl.BlockSpec(a.shape, lambda b, j: (0,) * a.ndim)
    return pl.pallas_call(
        functools.partial(_gla_kernel, reverse=reverse, nchunk=blk // GLA_CHUNK),
        grid=(nseq, nblk),
        in_specs=[pl.BlockSpec((blk, GLA_QK), tok), pl.BlockSpec((blk, GLA_QK), tok),
                  pl.BlockSpec((blk, GLA_WIDTH), tok), pl.BlockSpec((blk, GLA_QK), tok),
                  pl.BlockSpec((None, GLA_WIDTH, GLA_QK), lambda b, j: (b, 0, 0)),
                  const(hexp), const(tri), const(bd)],
        out_specs=[pl.BlockSpec((blk, GLA_WIDTH), otok),
                   pl.BlockSpec((None, GLA_WIDTH, GLA_QK), lambda b, j: (b, 0, 0))],
        out_shape=[jax.ShapeDtypeStruct((nseq * seq, GLA_WIDTH), F32),
                   jax.ShapeDtypeStruct((nseq, GLA_WIDTH, GLA_QK), F32)],
        scratch_shapes=[pltpu.VMEM((GLA_WIDTH, GLA_QK), F32), pltpu.VMEM((GLA_CHUNK, GLA_QK), F32),
                        pltpu.VMEM((GLA_PAIR_ROWS, GLA_QK), F32), pltpu.VMEM((GLA_PAIR_ROWS, GLA_WIDTH), F32)],
        compiler_params=_params(("parallel", "arbitrary")),
        name="gla_bwd" if reverse else "gla_fwd",
    )(gq, gk, gv, la, s0t, hexp, tri, bd)


def _state_to_blockdiag_t(s):
    b = s.shape[0]
    eye = jnp.eye(GLA_HEADS, dtype=s.dtype)
    return jnp.einsum('bhde,hg->bhegd', s, eye).reshape(b, GLA_WIDTH, GLA_QK)


def _blockdiag_t_to_state(st):
    b = st.shape[0]
    eye = jnp.eye(GLA_HEADS, dtype=st.dtype)
    return jnp.einsum('bhegd,hg->bhde', st.reshape(b, GLA_HEADS, GLA_DV, GLA_HEADS, GLA_DK), eye)


def _stack_heads(q_ref, kv):
    return jnp.concatenate([q_ref[:, (kv * ATTN_GROUP + r) * HEAD_DIM:(kv * ATTN_GROUP + r + 1) * HEAD_DIM]
                            for r in range(ATTN_GROUP)], axis=0)


def _sink_column(sink_ref, kv, rows):
    return jnp.concatenate([jnp.broadcast_to(sink_ref[kv * ATTN_GROUP + r:kv * ATTN_GROUP + r + 1, 0:1], (rows, 1))
                            for r in range(ATTN_GROUP)], axis=0)


def _qk(q, k):
    return lax.dot_general(q.astype(BF16), k.astype(BF16), (((1,), (1,)), ((), ())), preferred_element_type=F32)


def _ctx_attn_kernel(q_ref, k_ref, v_ref, sink_ref, o_ref):
    t = q_ref.shape[0]
    for kv in range(ATTN_KV_HEADS):
        k = k_ref[:, kv * HEAD_DIM:(kv + 1) * HEAD_DIM]
        v = v_ref[:, kv * HEAD_DIM:(kv + 1) * HEAD_DIM]
        s = _qk(_stack_heads(q_ref, kv), k)
        sink = _sink_column(sink_ref, kv, t)
        m = jnp.maximum(jnp.max(s, axis=-1, keepdims=True), sink)
        p = jnp.exp(s - m)
        den = jnp.sum(p, axis=-1, keepdims=True) + jnp.exp(sink - m)
        o = _bdot(p, v) / den
        for r in range(ATTN_GROUP):
            h = kv * ATTN_GROUP + r
            o_ref[:, h * HEAD_DIM:(h + 1) * HEAD_DIM] = o[r * t:(r + 1) * t]


def _ctx_attn(aq, ak, av, sink_b, seq, nseq):
    tok = lambda w: pl.BlockSpec((seq, w), lambda b: (b, 0))
    return pl.pallas_call(
        _ctx_attn_kernel,
        grid=(nseq,),
        in_specs=[tok(ATTN_WIDTH), tok(KV_WIDTH), tok(KV_WIDTH), pl.BlockSpec(sink_b.shape, lambda b: (0, 0))],
        out_specs=tok(ATTN_WIDTH),
        out_shape=jax.ShapeDtypeStruct((nseq * seq, ATTN_WIDTH), F32),
        compiler_params=_params(("parallel",)),
        name="ctx_attn",
    )(aq, ak, av, sink_b)


def _lat_attn_kernel(q_ref, k_ref, v_ref, ck_ref, cv_ref, sink_ref, o_ref):
    t = k_ref.shape[0]
    blk = ATTN_BLOCK
    span = 3 * blk
    i = pl.program_id(1)
    start = pl.multiple_of(jnp.clip((i - 1) * blk, 0, t - span), blk)
    kw = k_ref[pl.ds(start, span), :]
    vw = v_ref[pl.ds(start, span), :]
    qpos = i * blk + lax.broadcasted_iota(jnp.int32, (blk, 1), 0)
    kpos = start + lax.broadcasted_iota(jnp.int32, (1, span), 1)
    inwin = jnp.abs(qpos - kpos) <= WINDOW
    inwin = jnp.concatenate([inwin] * ATTN_GROUP, axis=0)
    for kv in range(ATTN_KV_HEADS):
        lanes = slice(kv * HEAD_DIM, (kv + 1) * HEAD_DIM)
        q = _stack_heads(q_ref, kv)
        s_loc = jnp.where(inwin, _qk(q, kw[:, lanes]), NEG_INF)
        s_ctx = _qk(q, ck_ref[:, lanes])
        sink = _sink_column(sink_ref, kv, blk)
        m = jnp.maximum(jnp.maximum(jnp.max(s_loc, axis=-1, keepdims=True),
                                    jnp.max(s_ctx, axis=-1, keepdims=True)), sink)
        p_loc = jnp.exp(s_loc - m)
        p_ctx = jnp.exp(s_ctx - m)
        den = (jnp.sum(p_loc, axis=-1, keepdims=True) + jnp.sum(p_ctx, axis=-1, keepdims=True)
               + jnp.exp(sink - m))
        o = (_bdot(p_loc, vw[:, lanes]) + _bdot(p_ctx, cv_ref[:, lanes])) / den
        for r in range(ATTN_GROUP):
            h = kv * ATTN_GROUP + r
            o_ref[:, h * HEAD_DIM:(h + 1) * HEAD_DIM] = o[r * blk:(r + 1) * blk]


def _lat_attn(aq, ak, av, ck, cv, sink_b, seq, first_tok, nseq):
    nblk = seq // ATTN_BLOCK
    fb_q = first_tok // ATTN_BLOCK
    fb_s = first_tok // seq
    return pl.pallas_call(
        _lat_attn_kernel,
        grid=(nseq, nblk),
        in_specs=[pl.BlockSpec((ATTN_BLOCK, ATTN_WIDTH), lambda b, i: (fb_q + b * nblk + i, 0)),
                  pl.BlockSpec((seq, KV_WIDTH), lambda b, i: (fb_s + b, 0)),
                  pl.BlockSpec((seq, KV_WIDTH), lambda b, i: (fb_s + b, 0)),
                  pl.BlockSpec((None,) + ck.shape[1:], lambda b, i: (b, 0, 0)),
                  pl.BlockSpec((None,) + cv.shape[1:], lambda b, i: (b, 0, 0)),
                  pl.BlockSpec(sink_b.shape, lambda b, i: (0, 0))],
        out_specs=pl.BlockSpec((ATTN_BLOCK, ATTN_WIDTH), lambda b, i: (b * nblk + i, 0)),
        out_shape=jax.ShapeDtypeStruct((nseq * seq, ATTN_WIDTH), F32),
        compiler_params=_params(("parallel", "arbitrary")),
        name="lat_attn",
    )(aq, ak, av, ck, cv, sink_b)


def _outproj_kernel(yp_ref, of_ref, ob_ref, gg_ref, ya_ref, x_ref, mod_ref, gn_ref, hm_ref, w_ref, n2_ref,
                    xo_ref, h2t_ref):
    o = of_ref[...] + ob_ref[...]
    ms = _fdot(o * o, hm_ref[...])
    gg = gg_ref[...]
    y = o * lax.rsqrt(ms + EPS) * gn_ref[...] * (gg * _sigmoid(gg))
    mix = jnp.concatenate([yp_ref[...], y, ya_ref[...]], axis=1)
    xn = x_ref[...] + mod_ref[2:3, :] * _bdot(mix, w_ref[...])
    xo_ref[...] = xn
    h2 = _rms(xn) * n2_ref[...] * (1.0 + mod_ref[4:5, :]) + mod_ref[3:4, :]
    h2t_ref[...] = h2.T.astype(BF16)


def _outproj(ypool, o_f, o_b, gg, yattn, x, modblk, gnorm, hmean, w_out, n2):
    n, d = x.shape
    tm = TOKEN_BLOCK
    row = lambda w: pl.BlockSpec((tm, w), lambda i: (i, 0))
    full = lambda a: pl.BlockSpec(a.shape, lambda i: (0,) * a.ndim)
    return pl.pallas_call(
        _outproj_kernel,
        grid=(n // tm,),
        in_specs=[row(POOL_WIDTH), row(GLA_WIDTH), row(GLA_WIDTH), row(GLA_WIDTH), row(ATTN_WIDTH), row(d),
                  pl.BlockSpec((None, 6, d), lambda i: (i, 0, 0)), full(gnorm), full(hmean), full(w_out), full(n2)],
        out_specs=[row(d), pl.BlockSpec((d, tm), lambda i: (0, i))],
        out_shape=[jax.ShapeDtypeStruct((n, d), F32), jax.ShapeDtypeStruct((d, n), BF16)],
        compiler_params=_params(("parallel",)),
        name="out_proj",
    )(ypool, o_f, o_b, gg, yattn, x, modblk, gnorm, hmean, w_out, n2)


def _peer_score_kernel(h2t_ref, wqt_ref, sk_ref, s1_ref, s2_ref):
    qt = jnp.dot(wqt_ref[...], h2t_ref[...], preferred_element_type=F32)
    for h in range(PEER_HEADS):
        for p, out in enumerate((s1_ref, s2_ref)):
            r = (2 * h + p) * PEER_HALF
            out[h] = jnp.dot(sk_ref[2 * h + p], qt[r:r + PEER_HALF].astype(BF16), preferred_element_type=F32)


def _peer_scores(h2t, wqt, sk):
    d, n = h2t.shape
    tb = TOKEN_BLOCK
    nkeys = sk.shape[1]
    out = pl.BlockSpec((PEER_HEADS, nkeys, tb), lambda i: (0, 0, i))
    return pl.pallas_call(
        _peer_score_kernel,
        grid=(n // tb,),
        in_specs=[pl.BlockSpec((d, tb), lambda i: (0, i)),
                  pl.BlockSpec(wqt.shape, lambda i: (0, 0)),
                  pl.BlockSpec(sk.shape, lambda i: (0, 0, 0))],
        out_specs=[out, out],
        out_shape=[jax.ShapeDtypeStruct((PEER_HEADS, nkeys, n), F32)] * 2,
        compiler_params=_params(("parallel",)),
        name="peer_scores",
    )(h2t, wqt, sk)


PEER_CANDS = [(a, b) for a in range(PEER_TOPK) for b in range(PEER_TOPK) if (a + 1) * (b + 1) <= PEER_TOPK]
PEER_CAND_ROWS = -(-len(PEER_CANDS) // SUBLANES) * SUBLANES


def _top_values(cur, count):
    vals = []
    for _ in range(count):
        m = jnp.max(cur, axis=0, keepdims=True)
        vals.append(m)
        cur = jnp.where(cur == m, -jnp.inf, cur)
    return vals


def _peer_gate_kernel(s1_ref, s2_ref, tau_ref, a_ref, bx_ref, cand_ref):
    for h in range(PEER_HEADS):
        s1 = s1_ref[h]
        s2 = s2_ref[h]
        v1 = _top_values(s1, PEER_TOPK)
        v2 = _top_values(s2, PEER_TOPK)
        cand_ref[...] = jnp.full(cand_ref.shape, -jnp.inf, F32)
        for r, (a, b) in enumerate(PEER_CANDS):
            cand_ref[r:r + 1, :] = v1[a] + v2[b]
        best = _top_values(cand_ref[...], PEER_TOPK)
        z = None
        for val in best:
            e = jnp.exp(val - best[0])
            z = e if z is None else z + e
        tau_ref[h:h + 1, :] = best[-1]
        a_ref[h] = jnp.exp(s1 - v1[0]) / z
        bx_ref[h] = jnp.exp(s2 - v2[0])


def _peer_gates(s1, s2):
    _, nkeys, n = s1.shape
    tb = TOKEN_BLOCK
    blk = pl.BlockSpec((PEER_HEADS, nkeys, tb), lambda i: (0, 0, i))
    return pl.pallas_call(
        _peer_gate_kernel,
        grid=(n // tb,),
        in_specs=[blk, blk],
        out_specs=[pl.BlockSpec((PEER_HEADS, tb), lambda i: (0, i)), blk, blk],
        out_shape=[jax.ShapeDtypeStruct((PEER_HEADS, n), F32),
                   jax.ShapeDtypeStruct((PEER_HEADS, nkeys, n), F32),
                   jax.ShapeDtypeStruct((PEER_HEADS, nkeys, n), F32)],
        scratch_shapes=[pltpu.VMEM((PEER_CAND_ROWS, tb), F32)],
        compiler_params=_params(("parallel",)),
        name="peer_gates",
    )(s1, s2)


def _gelu(x):
    return 0.5 * x * (1.0 + lax.erf(x * (2.0 ** -0.5)))


def _peer_dense_kernel(h2t_ref, u_ref, vt_ref, s1_ref, a_ref, s2_ref, bx_ref, tau_ref, x_ref, mod_ref,
                       xo_ref, yt_ref, *, nkeys):
    e = pl.program_id(1)

    @pl.when(e == 0)
    def _():
        yt_ref[...] = jnp.zeros(yt_ref.shape, F32)

    act = _gelu(jnp.dot(u_ref[...], h2t_ref[...], preferred_element_type=F32))
    per_step = u_ref.shape[0] // nkeys
    w = []
    for ii in range(per_step):
        i = e * per_step + ii
        g = None
        for h in range(PEER_HEADS):
            picked = (s2_ref[h] + s1_ref[h, pl.ds(i, 1), :]) >= tau_ref[h:h + 1, :]
            term = jnp.where(picked, bx_ref[h], 0.0) * a_ref[h, pl.ds(i, 1), :]
            g = term if g is None else g + term
        w.append((g * act[ii * nkeys:(ii + 1) * nkeys]).astype(BF16))
    yt_ref[...] += jnp.dot(vt_ref[...], jnp.concatenate(w, axis=0), preferred_element_type=F32)

    @pl.when(e == pl.num_programs(1) - 1)
    def _():
        xo_ref[...] = x_ref[...] + mod_ref[5:6, :] * yt_ref[...].T


def _peer_dense(h2t, u, vt, s1, a, s2, bx, tau, x, modblk):
    n, d = x.shape
    nexp = u.shape[0]
    nkeys = s1.shape[1]
    tb = TOKEN_BLOCK
    eb = min(PEER_EXPERT_BLOCK, nexp)
    keyed = pl.BlockSpec((PEER_HEADS, nkeys, tb), lambda t, e: (0, 0, t))
    return pl.pallas_call(
        functools.partial(_peer_dense_kernel, nkeys=nkeys),
        grid=(n // tb, nexp // eb),
        in_specs=[pl.BlockSpec((d, tb), lambda t, e: (0, t)),
                  pl.BlockSpec((eb, d), lambda t, e: (e, 0)),
                  pl.BlockSpec((d, eb), lambda t, e: (0, e)),
                  keyed, keyed, keyed, keyed,
                  pl.BlockSpec((PEER_HEADS, tb), lambda t, e: (0, t)),
                  pl.BlockSpec((tb, d), lambda t, e: (t, 0)),
                  pl.BlockSpec((None, 6, d), lambda t, e: (t, 0, 0))],
        out_specs=pl.BlockSpec((tb, d), lambda t, e: (t, 0)),
        out_shape=jax.ShapeDtypeStruct((n, d), F32),
        scratch_shapes=[pltpu.VMEM((d, tb), F32)],
        compiler_params=_params(("parallel", "arbitrary")),
        name="peer_dense",
    )(h2t, u, vt, s1, a, s2, bx, tau, x, modblk)


def _final_norm_kernel(x_ref, g_ref, o_ref):
    o_ref[...] = _rms(x_ref[...]) * g_ref[...]


def _final_norm(x, g):
    n, d = x.shape
    tm = TOKEN_BLOCK
    return pl.pallas_call(
        _final_norm_kernel,
        grid=(n // tm,),
        in_specs=[pl.BlockSpec((tm, d), lambda i: (i, 0)), pl.BlockSpec((1, d), lambda i: (0, 0))],
        out_specs=pl.BlockSpec((tm, d), lambda i: (i, 0)),
        out_shape=jax.ShapeDtypeStruct((n, d), F32),
        compiler_params=_params(("parallel",)),
        name="final_norm",
    )(x, g.reshape(1, d))


def _rope_swap_columns(width):
    half = HEAD_DIM // 2
    nf = half // 2
    perm = np.zeros(width, np.int32)
    sign = np.zeros(width, np.float32)
    for c in range(width):
        r = c % half
        if r < nf:
            perm[c], sign[c] = c + nf, -1.0
        else:
            perm[c], sign[c] = c - nf, 1.0
    return perm, sign


def _pack_w_in(w_in):
    d = w_in.shape[0]
    o_aq = POOL_WIDTH + 2 * GLA_QK + 2 * GLA_WIDTH + 2 * GLA_GATE_RANK
    o_ak = o_aq + ATTN_WIDTH
    o_av = o_ak + KV_WIDTH
    w_aq = w_in[:, o_aq:o_ak]
    w_ak = w_in[:, o_ak:o_av]
    pq, sq = _rope_swap_columns(ATTN_WIDTH)
    pk, sk = _rope_swap_columns(KV_WIDTH)
    gz = jnp.pad(w_in[:, o_aq - 2 * GLA_GATE_RANK:o_aq], ((0, 0), (0, LANES - 2 * GLA_GATE_RANK)))
    cols = [w_in[:, :o_aq - 2 * GLA_GATE_RANK], gz, w_aq, w_ak, w_in[:, o_av:],
            w_aq[:, pq] * sq[None, :], w_ak[:, pk] * sk[None, :]]
    return jnp.concatenate(cols, axis=1).astype(BF16)


def _block_diag(w):
    g, c, _ = w.shape
    eye = jnp.eye(g, dtype=w.dtype)
    return jnp.einsum('gcd,gh->gchd', w, eye).reshape(g * c, g * c)


def _rope_tables(n_ctx, n_lat_seq, n_lat_batch):
    half = HEAD_DIM // 2
    nf = half // 2
    freqs = ROPE_THETA ** (-jnp.arange(nf, dtype=F32) / nf)
    rows = n_lat_seq // GRID_W
    pos_row = jnp.repeat(jnp.arange(rows), GRID_W).astype(F32)
    pos_col = jnp.tile(jnp.arange(GRID_W), rows).astype(F32)
    ar = pos_row[:, None] * freqs[None, :]
    ac = pos_col[:, None] * freqs[None, :]
    cos = jnp.concatenate([jnp.cos(ar)] * 2 + [jnp.cos(ac)] * 2, axis=1)
    sin = jnp.concatenate([jnp.sin(ar)] * 2 + [jnp.sin(ac)] * 2, axis=1)
    reps = LANES // HEAD_DIM
    cos = jnp.tile(jnp.tile(cos, (1, reps)), (n_lat_batch, 1))
    sin = jnp.tile(jnp.tile(sin, (1, reps)), (n_lat_batch, 1))
    cos = jnp.concatenate([jnp.ones((n_ctx, LANES), F32), cos], axis=0)
    sin = jnp.concatenate([jnp.zeros((n_ctx, LANES), F32), sin], axis=0)
    return cos, sin


def kernel(x_prompt, x_sample, cache_k, cache_v, state_fwd, state_bwd, c, c_ctx, w_ada, b_ada, norm1_g, norm2_g,
           w_in, pool_w, pool_scale, gla_gate_w_f, gla_gate_b_f, gla_gate_w_b, gla_gate_b_b, gla_norm_g,
           attn_sink, w_out, peer_wq, peer_subkeys, peer_u, peer_v, final_norm_g):
    nb, seq, d = x_prompt.shape
    nlb, lseq, _ = x_sample.shape
    depth = w_ada.shape[0]
    n_ctx, n_lat = nb * seq, nlb * lseq
    n = n_ctx + n_lat
    tm = TOKEN_BLOCK
    nkeys = peer_subkeys.shape[3]
    assert d == D_MODEL and n_ctx % tm == 0 and lseq % tm == 0 and n_ctx % lseq == 0
    assert seq % GLA_CHUNK == 0 and lseq % GLA_BLOCK == 0 and lseq % GRID_W == 0 and lseq >= 3 * ATTN_BLOCK
    assert nlb + 1 <= SUBLANES and nkeys % SUBLANES == 0

    x0 = jnp.concatenate([x_prompt.reshape(n_ctx, d), x_sample.reshape(n_lat, d)], axis=0)
    cvec = jnp.zeros((SUBLANES, d), F32).at[0].set(c_ctx).at[1:1 + nlb].set(c)
    mods = _ada_mods(cvec, w_ada, b_ada)
    blk_row = np.concatenate([np.zeros(n_ctx // tm, np.int32),
                              1 + np.repeat(np.arange(nlb, dtype=np.int32), lseq // tm)])
    modblk = mods[:, blk_row, :].reshape(depth, n // tm, 6, d)
    cos, sin = _rope_tables(n_ctx, lseq, nlb)
    gla_consts = _gla_consts()
    hd = np.arange(GLA_WIDTH) // GLA_DV
    hmean = jnp.asarray((hd[:, None] == hd[None, :]).astype(np.float32) / GLA_DV)
    zero_state = jnp.zeros((nb, GLA_WIDTH, GLA_QK), F32)

    def layer(x, lp):
        (mod_l, n1, n2, w_in_l, pool_w_l, pool_scale_l, gw_f, gb_f, gw_b, gb_b, gnorm, sink, w_out_l,
         wq, subk, pu, pv, ck, cv, sf, sb) = lp
        w_big = _pack_w_in(w_in_l)
        w2 = jnp.zeros((LANES, 2 * GLA_QK), F32)
        w2 = w2.at[:GLA_GATE_RANK, :GLA_QK].set(gw_f).at[GLA_GATE_RANK:2 * GLA_GATE_RANK, GLA_QK:].set(gw_b)
        b2 = jnp.concatenate([gb_f, gb_b]).reshape(1, 2 * GLA_QK)
        pp, gq, gk, gv, gg, la_f, la_b, aq, ak, av = _inproj(x, mod_l, n1.reshape(1, d), w_big, cos, sin, w2, b2)

        w_bd = _block_diag(pool_w_l).astype(BF16)
        scale = pool_scale_l.reshape(1, POOL_WIDTH)
        y_pool = jnp.concatenate([_pool(pp, w_bd, scale, seq, 0, nb),
                                  _pool(pp, w_bd, scale, lseq, n_ctx // lseq, nlb)], axis=0)

        lblk = min(GLA_BLOCK, lseq)
        of_c, sf_c = _gla(gq, gk, gv, la_f, zero_state, gla_consts, seq, 0, nb, False)
        ob_c, sb_c = _gla(gq, gk, gv, la_b, zero_state, gla_consts, seq, 0, nb, True)
        of_l, _ = _gla(gq, gk, gv, la_f, _state_to_blockdiag_t(sf), gla_consts, lseq, n_ctx // lblk, nlb, False)
        ob_l, _ = _gla(gq, gk, gv, la_b, _state_to_blockdiag_t(sb), gla_consts, lseq, n_ctx // lblk, nlb, True)
        o_f = jnp.concatenate([of_c, of_l], axis=0)
        o_b = jnp.concatenate([ob_c, ob_l], axis=0)

        sink_b = jnp.broadcast_to(sink.reshape(ATTN_HEADS, 1), (ATTN_HEADS, LANES))
        y_attn = jnp.concatenate([_ctx_attn(aq, ak, av, sink_b, seq, nb),
                                  _lat_attn(aq, ak, av, ck, cv, sink_b, lseq, n_ctx, nlb)], axis=0)

        x1, h2t = _outproj(y_pool, o_f, o_b, gg, y_attn, x, mod_l, gnorm.reshape(1, GLA_WIDTH), hmean,
                           w_out_l.astype(BF16), n2.reshape(1, d))

        wqt = wq.T.astype(BF16)
        sk = subk.reshape(2 * PEER_HEADS, nkeys, PEER_HALF).astype(BF16)
        s1, s2 = _peer_scores(h2t, wqt, sk)
        tau, a, bx = _peer_gates(s1, s2)
        x2 = _peer_dense(h2t, pu.astype(BF16), pv.T.astype(BF16), s1, a, s2, bx, tau, x1, mod_l)
        return x2, (ak[:n_ctx], av[:n_ctx], _blockdiag_t_to_state(sf_c), _blockdiag_t_to_state(sb_c))

    past = cache_k.shape[2]
    xs = (modblk, norm1_g, norm2_g, w_in, pool_w, pool_scale, gla_gate_w_f, gla_gate_b_f, gla_gate_w_b,
          gla_gate_b_b, gla_norm_g, attn_sink, w_out, peer_wq, peer_subkeys, peer_u, peer_v,
          jnp.swapaxes(cache_k, 0, 1).reshape(depth, nlb, past, KV_WIDTH),
          jnp.swapaxes(cache_v, 0, 1).reshape(depth, nlb, past, KV_WIDTH),
          jnp.swapaxes(state_fwd, 0, 1), jnp.swapaxes(state_bwd, 0, 1))
    x_fin, (ks, vs, sfs, sbs) = lax.scan(layer, x0, xs)

    y = _final_norm(x_fin, final_norm_g)
    y_prompt = y[:n_ctx].reshape(nb, seq, d)
    y_sample = y[n_ctx:].reshape(nlb, lseq, d)
    new_k = jnp.swapaxes(ks.reshape(depth, nb, seq, ATTN_KV_HEADS, HEAD_DIM), 0, 1)
    new_v = jnp.swapaxes(vs.reshape(depth, nb, seq, ATTN_KV_HEADS, HEAD_DIM), 0, 1)
    return (y_prompt, y_sample, new_k, new_v, jnp.swapaxes(sfs, 0, 1), jnp.swapaxes(sbs, 0, 1))
```

```python
import functools
import math

import numpy as np
import jax
import jax.numpy as jnp
from jax import lax
from jax.experimental import pallas as pl
from jax.experimental.pallas import tpu as pltpu

F32 = jnp.float32
BF16 = jnp.bfloat16
HIGHEST = lax.Precision.HIGHEST

D_MODEL = 1024
GRID_W = 64
EPS = 1e-6
POOL_WIDTH = 256
POOL_GROUPS = 4
POOL_GROUP_DIM = 64
POOL_WINDOWS = (2, 4, 8, 16)
GLA_HEADS = 4
GLA_DV = 64
GLA_DK = 32
GLA_QK = GLA_HEADS * GLA_DK
GLA_WIDTH = GLA_HEADS * GLA_DV
GLA_GATE_RANK = 16
GLA_GATE_TAU = 16.0
GLA_CHUNK = 64
ATTN_HEADS = 8
ATTN_KV_HEADS = 2
ATTN_GROUP = 4
HEAD_DIM = 64
ATTN_WIDTH = ATTN_HEADS * HEAD_DIM
KV_WIDTH = ATTN_KV_HEADS * HEAD_DIM
WINDOW = 128
ATTN_BLOCK = 128
ROPE_THETA = 10000.0
NEG_INF = -1e30
PEER_HEADS = 8
PEER_HALF = 64
PEER_TOPK = 16

LANES = 128
SUBLANES = 8
VMEM_LIMIT = 56 * 1024 * 1024

TOKEN_BLOCK = 512
GLA_BLOCK = 256
PEER_EXPERT_BLOCK = 1024

C_POOL, C_GQ, C_GK, C_GV, C_GG, C_GZ, C_AQ, C_AK, C_AV, C_AQS, C_AKS, C_END = (
    0, 256, 384, 512, 768, 1024, 1152, 1664, 1792, 1920, 2432, 2560)


def _params(sem):
    return pltpu.CompilerParams(dimension_semantics=sem, vmem_limit_bytes=VMEM_LIMIT)


def _bdot(a, b):
    return jnp.dot(a.astype(BF16), b.astype(BF16), preferred_element_type=F32)


def _fdot(a, b):
    return jnp.dot(a, b, preferred_element_type=F32, precision=HIGHEST)


def _sigmoid(x):
    return 1.0 / (1.0 + jnp.exp(-x))


def _rms(x):
    return x * lax.rsqrt(jnp.mean(x * x, axis=-1, keepdims=True) + EPS)


def _ada_kernel(c_ref, w_ref, b_ref, o_ref):
    c = c_ref[...]
    o_ref[...] = _fdot(c * _sigmoid(c), w_ref[...]) + b_ref[...]


def _ada_mods(cvec, w_ada, b_ada):
    depth, d, six_d = w_ada.shape
    nj = six_d // d
    return pl.pallas_call(
        _ada_kernel,
        grid=(depth, nj),
        in_specs=[pl.BlockSpec((SUBLANES, d), lambda l, j: (0, 0)),
                  pl.BlockSpec((None, d, d), lambda l, j: (l, 0, j)),
                  pl.BlockSpec((None, 1, d), lambda l, j: (l, 0, j))],
        out_specs=pl.BlockSpec((None, SUBLANES, d), lambda l, j: (l, 0, j)),
        out_shape=jax.ShapeDtypeStruct((depth, SUBLANES, six_d), F32),
        compiler_params=_params(("parallel", "parallel")),
        name="ada_mod",
    )(cvec, w_ada, b_ada.reshape(depth, 1, six_d))


def _inproj_kernel(x_ref, mod_ref, n1_ref, w_ref, cos_ref, sin_ref, w2_ref, b2_ref,
                   pp_ref, gq_ref, gk_ref, gv_ref, gg_ref, laf_ref, lab_ref, aq_ref, ak_ref, av_ref):
    h = _rms(x_ref[...]) * n1_ref[...]
    h = (h * (1.0 + mod_ref[1:2, :]) + mod_ref[0:1, :]).astype(BF16)

    def proj(lo, hi):
        return jnp.dot(h, w_ref[:, lo:hi], preferred_element_type=F32)

    pp_ref[...] = proj(C_POOL, C_GQ)
    gq_ref[...] = proj(C_GQ, C_GK) * (GLA_DK ** -0.5)
    gk_ref[...] = proj(C_GK, C_GV)
    gv_ref[...] = proj(C_GV, C_GG)
    gg_ref[...] = proj(C_GG, C_GZ)
    logit = _fdot(proj(C_GZ, C_AQ), w2_ref[...]) + b2_ref[...]
    la = (jnp.minimum(logit, 0.0) - jnp.log1p(jnp.exp(-jnp.abs(logit)))) * (1.0 / GLA_GATE_TAU)
    laf_ref[...] = la[:, :GLA_QK]
    lab_ref[...] = la[:, GLA_QK:]
    cos = cos_ref[...]
    sin = sin_ref[...]
    cos4 = jnp.concatenate([cos] * (ATTN_WIDTH // LANES), axis=1)
    sin4 = jnp.concatenate([sin] * (ATTN_WIDTH // LANES), axis=1)
    aq_ref[...] = (proj(C_AQ, C_AK) * cos4 + proj(C_AQS, C_AKS) * sin4) * (HEAD_DIM ** -0.5)
    ak_ref[...] = proj(C_AK, C_AV) * cos + proj(C_AKS, C_END) * sin
    av_ref[...] = proj(C_AV, C_AQS)


def _inproj(x, modblk, n1, w_big, cos, sin, w2, b2):
    n, d = x.shape
    tm = TOKEN_BLOCK
    widths = (POOL_WIDTH, GLA_QK, GLA_QK, GLA_WIDTH, GLA_WIDTH, GLA_QK, GLA_QK, ATTN_WIDTH, KV_WIDTH, KV_WIDTH)
    row = lambda w: pl.BlockSpec((tm, w), lambda i: (i, 0))
    full = lambda a: pl.BlockSpec(a.shape, lambda i: (0,) * a.ndim)
    return pl.pallas_call(
        _inproj_kernel,
        grid=(n // tm,),
        in_specs=[row(d), pl.BlockSpec((None, 6, d), lambda i: (i, 0, 0)), full(n1), full(w_big),
                  row(LANES), row(LANES), full(w2), full(b2)],
        out_specs=[row(w) for w in widths],
        out_shape=[jax.ShapeDtypeStruct((n, w), F32) for w in widths],
        compiler_params=_params(("parallel",)),
        name="in_proj",
    )(x, modblk, n1, w_big, cos, sin, w2, b2)


def _pool_kernel(x_ref, w_ref, scale_ref, o_ref):
    t = x_ref.shape[0]
    pad = 32
    n = t + pad
    x = x_ref[...]
    xp = jnp.concatenate([x, jnp.zeros((pad, POOL_WIDTH), F32)], axis=0)
    back = {1: xp}
    for w in (2, 4, 8, 16):
        back[w] = back[w // 2] + pltpu.roll(back[w // 2], w // 2, 0)
    grp = lax.broadcasted_iota(jnp.int32, (1, POOL_WIDTH), 1) // POOL_GROUP_DIM
    tok = lax.broadcasted_iota(jnp.int32, (t, 1), 0)
    wsum = None
    cnt = None
    for g, w in enumerate(POOL_WINDOWS):
        left = w // 2
        right = w - 1 - left
        ws = back[w] if right == 0 else pltpu.roll(back[w], n - right, 0)
        ws = ws[:t]
        c = (jnp.minimum(tok + right + 1, t) - jnp.maximum(tok - left, 0)).astype(F32)
        if wsum is None:
            wsum, cnt = ws, jnp.broadcast_to(c, (t, POOL_WIDTH))
        else:
            wsum = jnp.where(grp == g, ws, wsum)
            cnt = jnp.where(grp == g, c, cnt)
    diff = wsum / cnt - x
    o_ref[...] = _bdot(diff, w_ref[...]) * scale_ref[...]


def _pool(pp, w_bd, scale, seq, first_block, nseq):
    return pl.pallas_call(
        _pool_kernel,
        grid=(nseq,),
        in_specs=[pl.BlockSpec((seq, POOL_WIDTH), lambda b: (first_block + b, 0)),
                  pl.BlockSpec(w_bd.shape, lambda b: (0, 0)),
                  pl.BlockSpec(scale.shape, lambda b: (0, 0))],
        out_specs=pl.BlockSpec((seq, POOL_WIDTH), lambda b: (b, 0)),
        out_shape=jax.ShapeDtypeStruct((nseq * seq, POOL_WIDTH), F32),
        compiler_params=_params(("parallel",)),
        name="pool_mix",
    )(pp, w_bd, scale)


def _gla_rows(reverse):
    out = []
    for s in range(GLA_CHUNK):
        g = s // SUBLANES
        out.append((0, SUBLANES * (g + 1)) if reverse else (SUBLANES * g, GLA_CHUNK))
    return out


GLA_PAIR_ROWS = sum(hi - lo for lo, hi in _gla_rows(False))


def _gla_kernel(q_ref, k_ref, v_ref, la_ref, s0_ref, hexp_ref, tri_ref, bd_ref,
                o_ref, sout_ref, st_ref, b_ref, p_ref, z_ref, *, reverse, nchunk):
    j = pl.program_id(1)

    @pl.when(j == 0)
    def _():
        st_ref[...] = s0_ref[...]

    rows = _gla_rows(reverse)
    ngrp = GLA_CHUNK // SUBLANES
    for c in (range(nchunk - 1, -1, -1) if reverse else range(nchunk)):
        r0 = c * GLA_CHUNK
        q = q_ref[r0:r0 + GLA_CHUNK, :]
        k = k_ref[r0:r0 + GLA_CHUNK, :]
        v = v_ref[r0:r0 + GLA_CHUNK, :]
        b = _fdot(tri_ref[...], la_ref[r0:r0 + GLA_CHUNK, :])
        b_ref[...] = b
        blast = b[0:1, :] if reverse else b[GLA_CHUNK - 1:GLA_CHUNK, :]
        st = st_ref[...]
        inter = lax.dot_general((q * jnp.exp(b)).astype(BF16), st.astype(BF16),
                                (((1,), (1,)), ((), ())), preferred_element_type=F32)
        off = 0
        for s, (lo, hi) in enumerate(rows):
            tio = lax.broadcasted_iota(jnp.int32, (hi - lo, 1), 0) + lo
            valid = (tio <= s) if reverse else (tio >= s)
            dlt = jnp.where(valid, b[lo:hi] - b_ref[s:s + 1, :], 0.0)
            p_ref[off:off + hi - lo, :] = jnp.where(valid, q[lo:hi] * k_ref[r0 + s:r0 + s + 1, :] * jnp.exp(dlt), 0.0)
            off += hi - lo
        z_ref[...] = jnp.dot(p_ref[...].astype(BF16), hexp_ref[...], preferred_element_type=F32)
        oacc = [None] * ngrp
        off = 0
        for g in range(ngrp):
            lo, hi = rows[g * SUBLANES]
            cg = None
            for s in range(g * SUBLANES, (g + 1) * SUBLANES):
                term = z_ref[off:off + hi - lo, :] * v_ref[r0 + s:r0 + s + 1, :]
                cg = term if cg is None else cg + term
                off += hi - lo
            for rg in range(lo // SUBLANES, hi // SUBLANES):
                piece = cg[(rg - lo // SUBLANES) * SUBLANES:(rg - lo // SUBLANES + 1) * SUBLANES]
                oacc[rg] = piece if oacc[rg] is None else oacc[rg] + piece
        o_ref[r0:r0 + GLA_CHUNK, :] = jnp.concatenate(oacc, axis=0) + inter
        ke = k * jnp.exp(blast - b)
        kv = jnp.dot(v.T.astype(BF16), ke.astype(BF16), preferred_element_type=F32)
        st_ref[...] = st * jnp.exp(blast) + kv * bd_ref[...]

    @pl.when(j == pl.num_programs(1) - 1)
    def _():
        sout_ref[...] = st_ref[...]


def _gla_consts():
    hd = np.arange(GLA_QK) // GLA_DK
    he = np.arange(GLA_WIDTH) // GLA_DV
    hexp = (hd[:, None] == he[None, :]).astype(np.float32)
    bd = hexp.T.copy()
    t = np.arange(GLA_CHUNK)
    tri_f = (t[None, :] <= t[:, None]).astype(np.float32)
    tri_b = (t[None, :] >= t[:, None]).astype(np.float32)
    return jnp.asarray(hexp, BF16), jnp.asarray(bd, F32), jnp.asarray(tri_f), jnp.asarray(tri_b)


def _gla(gq, gk, gv, la, s0t, consts, seq, first_block, nseq, reverse):
    hexp, bd, tri_f, tri_b = consts
    tri = tri_b if reverse else tri_f
    blk = min(GLA_BLOCK, seq)
    nblk = seq // blk
    if reverse:
        tok = lambda b, j: (first_block + b * nblk + (nblk - 1 - j), 0)
        otok = lambda b, j: (b * nblk + (nblk - 1 - j), 0)
    else:
        tok = lambda b, j: (first_block + b * nblk + j, 0)
        otok = lambda b, j: (b * nblk + j, 0)
    const = lambda a: pl.BlockSpec(a.shape, lambda b, j: (0,) * a.ndim)
    return pl.pallas_call(
        functools.partial(_gla_kernel, reverse=reverse, nchunk=blk // GLA_CHUNK),
        grid=(nseq, nblk),
        in_specs=[pl.BlockSpec((blk, GLA_QK), tok), pl.BlockSpec((blk, GLA_QK), tok),
                  pl.BlockSpec((blk, GLA_WIDTH), tok), pl.BlockSpec((blk, GLA_QK), tok),
                  pl.BlockSpec((None, GLA_WIDTH, GLA_QK), lambda b, j: (b, 0, 0)),
                  const(hexp), const(tri), const(bd)],
        out_specs=[pl.BlockSpec((blk, GLA_WIDTH), otok),
                   pl.BlockSpec((None, GLA_WIDTH, GLA_QK), lambda b, j: (b, 0, 0))],
        out_shape=[jax.ShapeDtypeStruct((nseq * seq, GLA_WIDTH), F32),
                   jax.ShapeDtypeStruct((nseq, GLA_WIDTH, GLA_QK), F32)],
        scratch_shapes=[pltpu.VMEM((GLA_WIDTH, GLA_QK), F32), pltpu.VMEM((GLA_CHUNK, GLA_QK), F32),
                        pltpu.VMEM((GLA_PAIR_ROWS, GLA_QK), F32), pltpu.VMEM((GLA_PAIR_ROWS, GLA_WIDTH), F32)],
        compiler_params=_params(("parallel", "arbitrary")),
        name="gla_bwd" if reverse else "gla_fwd",
    )(gq, gk, gv, la, s0t, hexp, tri, bd)


def _state_to_blockdiag_t(s):
    b = s.shape[0]
    eye = jnp.eye(GLA_HEADS, dtype=s.dtype)
    return jnp.einsum('bhde,hg->bhegd', s, eye).reshape(b, GLA_WIDTH, GLA_QK)


def _blockdiag_t_to_state(st):
    b = st.shape[0]
    eye = jnp.eye(GLA_HEADS, dtype=st.dtype)
    return jnp.einsum('bhegd,hg->bhde', st.reshape(b, GLA_HEADS, GLA_DV, GLA_HEADS, GLA_DK), eye)


def _stack_heads(q_ref, kv):
    return jnp.concatenate([q_ref[:, (kv * ATTN_GROUP + r) * HEAD_DIM:(kv * ATTN_GROUP + r + 1) * HEAD_DIM]
                            for r in range(ATTN_GROUP)], axis=0)


def _sink_column(sink_ref, kv, rows):
    return jnp.concatenate([jnp.broadcast_to(sink_ref[kv * ATTN_GROUP + r:kv * ATTN_GROUP + r + 1, 0:1], (rows, 1))
                            for r in range(ATTN_GROUP)], axis=0)


def _qk(q, k):
    return lax.dot_general(q.astype(BF16), k.astype(BF16), (((1,), (1,)), ((), ())), preferred_element_type=F32)


def _ctx_attn_kernel(q_ref, k_ref, v_ref, sink_ref, o_ref):
    t = q_ref.shape[0]
    for kv in range(ATTN_KV_HEADS):
        k = k_ref[:, kv * HEAD_DIM:(kv + 1) * HEAD_DIM]
        v = v_ref[:, kv * HEAD_DIM:(kv + 1) * HEAD_DIM]
        s = _qk(_stack_heads(q_ref, kv), k)
        sink = _sink_column(sink_ref, kv, t)
        m = jnp.maximum(jnp.max(s, axis=-1, keepdims=True), sink)
        p = jnp.exp(s - m)
        den = jnp.sum(p, axis=-1, keepdims=True) + jnp.exp(sink - m)
        o = _bdot(p, v) / den
        for r in range(ATTN_GROUP):
            h = kv * ATTN_GROUP + r
            o_ref[:, h * HEAD_DIM:(h + 1) * HEAD_DIM] = o[r * t:(r + 1) * t]


def _ctx_attn(aq, ak, av, sink_b, seq, nseq):
    tok = lambda w: pl.BlockSpec((seq, w), lambda b: (b, 0))
    return pl.pallas_call(
        _ctx_attn_kernel,
        grid=(nseq,),
        in_specs=[tok(ATTN_WIDTH), tok(KV_WIDTH), tok(KV_WIDTH), pl.BlockSpec(sink_b.shape, lambda b: (0, 0))],
        out_specs=tok(ATTN_WIDTH),
        out_shape=jax.ShapeDtypeStruct((nseq * seq, ATTN_WIDTH), F32),
        compiler_params=_params(("parallel",)),
        name="ctx_attn",
    )(aq, ak, av, sink_b)


def _lat_attn_kernel(q_ref, k_ref, v_ref, ck_ref, cv_ref, sink_ref, o_ref):
    t = k_ref.shape[0]
    blk = ATTN_BLOCK
    span = 3 * blk
    i = pl.program_id(1)
    start = pl.multiple_of(jnp.clip((i - 1) * blk, 0, t - span), blk)
    kw = k_ref[pl.ds(start, span), :]
    vw = v_ref[pl.ds(start, span), :]
    qpos = i * blk + lax.broadcasted_iota(jnp.int32, (blk, 1), 0)
    kpos = start + lax.broadcasted_iota(jnp.int32, (1, span), 1)
    inwin = jnp.abs(qpos - kpos) <= WINDOW
    inwin = jnp.concatenate([inwin] * ATTN_GROUP, axis=0)
    for kv in range(ATTN_KV_HEADS):
        lanes = slice(kv * HEAD_DIM, (kv + 1) * HEAD_DIM)
        q = _stack_heads(q_ref, kv)
        s_loc = jnp.where(inwin, _qk(q, kw[:, lanes]), NEG_INF)
        s_ctx = _qk(q, ck_ref[:, lanes])
        sink = _sink_column(sink_ref, kv, blk)
        m = jnp.maximum(jnp.maximum(jnp.max(s_loc, axis=-1, keepdims=True),
                                    jnp.max(s_ctx, axis=-1, keepdims=True)), sink)
        p_loc = jnp.exp(s_loc - m)
        p_ctx = jnp.exp(s_ctx - m)
        den = (jnp.sum(p_loc, axis=-1, keepdims=True) + jnp.sum(p_ctx, axis=-1, keepdims=True)
               + jnp.exp(sink - m))
        o = (_bdot(p_loc, vw[:, lanes]) + _bdot(p_ctx, cv_ref[:, lanes])) / den
        for r in range(ATTN_GROUP):
            h = kv * ATTN_GROUP + r
            o_ref[:, h * HEAD_DIM:(h + 1) * HEAD_DIM] = o[r * blk:(r + 1) * blk]


def _lat_attn(aq, ak, av, ck, cv, sink_b, seq, first_tok, nseq):
    nblk = seq // ATTN_BLOCK
    fb_q = first_tok // ATTN_BLOCK
    fb_s = first_tok // seq
    return pl.pallas_call(
        _lat_attn_kernel,
        grid=(nseq, nblk),
        in_specs=[pl.BlockSpec((ATTN_BLOCK, ATTN_WIDTH), lambda b, i: (fb_q + b * nblk + i, 0)),
                  pl.BlockSpec((seq, KV_WIDTH), lambda b, i: (fb_s + b, 0)),
                  pl.BlockSpec((seq, KV_WIDTH), lambda b, i: (fb_s + b, 0)),
                  pl.BlockSpec((None,) + ck.shape[1:], lambda b, i: (b, 0, 0)),
                  pl.BlockSpec((None,) + cv.shape[1:], lambda b, i: (b, 0, 0)),
                  pl.BlockSpec(sink_b.shape, lambda b, i: (0, 0))],
        out_specs=pl.BlockSpec((ATTN_BLOCK, ATTN_WIDTH), lambda b, i: (b * nblk + i, 0)),
        out_shape=jax.ShapeDtypeStruct((nseq * seq, ATTN_WIDTH), F32),
        compiler_params=_params(("parallel", "arbitrary")),
        name="lat_attn",
    )(aq, ak, av, ck, cv, sink_b)


def _outproj_kernel(yp_ref, of_ref, ob_ref, gg_ref, ya_ref, x_ref, mod_ref, gn_ref, hm_ref, w_ref, n2_ref,
                    xo_ref, h2t_ref):
    o = of_ref[...] + ob_ref[...]
    ms = _fdot(o * o, hm_ref[...])
    gg = gg_ref[...]
    y = o * lax.rsqrt(ms + EPS) * gn_ref[...] * (gg * _sigmoid(gg))
    mix = jnp.concatenate([yp_ref[...], y, ya_ref[...]], axis=1)
    xn = x_ref[...] + mod_ref[2:3, :] * _bdot(mix, w_ref[...])
    xo_ref[...] = xn
    h2 = _rms(xn) * n2_ref[...] * (1.0 + mod_ref[4:5, :]) + mod_ref[3:4, :]
    h2t_ref[...] = h2.T.astype(BF16)


def _outproj(ypool, o_f, o_b, gg, yattn, x, modblk, gnorm, hmean, w_out, n2):
    n, d = x.shape
    tm = TOKEN_BLOCK
    row = lambda w: pl.BlockSpec((tm, w), lambda i: (i, 0))
    full = lambda a: pl.BlockSpec(a.shape, lambda i: (0,) * a.ndim)
    return pl.pallas_call(
        _outproj_kernel,
        grid=(n // tm,),
        in_specs=[row(POOL_WIDTH), row(GLA_WIDTH), row(GLA_WIDTH), row(GLA_WIDTH), row(ATTN_WIDTH), row(d),
                  pl.BlockSpec((None, 6, d), lambda i: (i, 0, 0)), full(gnorm), full(hmean), full(w_out), full(n2)],
        out_specs=[row(d), pl.BlockSpec((d, tm), lambda i: (0, i))],
        out_shape=[jax.ShapeDtypeStruct((n, d), F32), jax.ShapeDtypeStruct((d, n), BF16)],
        compiler_params=_params(("parallel",)),
        name="out_proj",
    )(ypool, o_f, o_b, gg, yattn, x, modblk, gnorm, hmean, w_out, n2)


def _peer_score_kernel(h2t_ref, wqt_ref, sk_ref, s1_ref, s2_ref):
    qt = jnp.dot(wqt_ref[...], h2t_ref[...], preferred_element_type=F32)
    for h in range(PEER_HEADS):
        for p, out in enumerate((s1_ref, s2_ref)):
            r = (2 * h + p) * PEER_HALF
            out[h] = jnp.dot(sk_ref[2 * h + p], qt[r:r + PEER_HALF].astype(BF16), preferred_element_type=F32)


def _peer_scores(h2t, wqt, sk):
    d, n = h2t.shape
    tb = TOKEN_BLOCK
    nkeys = sk.shape[1]
    out = pl.BlockSpec((PEER_HEADS, nkeys, tb), lambda i: (0, 0, i))
    return pl.pallas_call(
        _peer_score_kernel,
        grid=(n // tb,),
        in_specs=[pl.BlockSpec((d, tb), lambda i: (0, i)),
                  pl.BlockSpec(wqt.shape, lambda i: (0, 0)),
                  pl.BlockSpec(sk.shape, lambda i: (0, 0, 0))],
        out_specs=[out, out],
        out_shape=[jax.ShapeDtypeStruct((PEER_HEADS, nkeys, n), F32)] * 2,
        compiler_params=_params(("parallel",)),
        name="peer_scores",
    )(h2t, wqt, sk)


PEER_CANDS = [(a, b) for a in range(PEER_TOPK) for b in range(PEER_TOPK) if (a + 1) * (b + 1) <= PEER_TOPK]
PEER_CAND_ROWS = -(-len(PEER_CANDS) // SUBLANES) * SUBLANES


def _top_values(cur, count):
    vals = []
    for _ in range(count):
        m = jnp.max(cur, axis=0, keepdims=True)
        vals.append(m)
        cur = jnp.where(cur == m, -jnp.inf, cur)
    return vals


def _peer_gate_kernel(s1_ref, s2_ref, cnt_ref, a_ref, rk_ref, bx_ref, cand_ref):
    for h in range(PEER_HEADS):
        s1 = s1_ref[h]
        s2 = s2_ref[h]
        v1 = _top_values(s1, PEER_TOPK)
        v2 = _top_values(s2, PEER_TOPK)
        cand_ref[...] = jnp.full(cand_ref.shape, -jnp.inf, F32)
        for r, (a, b) in enumerate(PEER_CANDS):
            cand_ref[r:r + 1, :] = v1[a] + v2[b]
        best = _top_values(cand_ref[...], PEER_TOPK)
        z = None
        for val in best:
            e = jnp.exp(val - best[0])
            z = e if z is None else z + e
        tau = best[-1]
        rk = None
        cnt = None
        for b in range(PEER_TOPK):
            above = jnp.where(v2[b] > s2, 1.0, 0.0)
            reach = jnp.where(s1 + v2[b] >= tau, 1.0, 0.0)
            rk = above if rk is None else rk + above
            cnt = reach if cnt is None else cnt + reach
        cnt_ref[h] = cnt
        a_ref[h] = jnp.exp(s1 - v1[0]) / z
        rk_ref[h] = pltpu.bitcast(rk.astype(BF16), jnp.uint32)
        bx_ref[h] = pltpu.bitcast(jnp.exp(s2 - v2[0]).astype(BF16), jnp.uint32)


def _peer_gates(s1, s2):
    _, nkeys, n = s1.shape
    tb = TOKEN_BLOCK
    blk = pl.BlockSpec((PEER_HEADS, nkeys, tb), lambda i: (0, 0, i))
    half = pl.BlockSpec((PEER_HEADS, nkeys // 2, tb), lambda i: (0, 0, i))
    return pl.pallas_call(
        _peer_gate_kernel,
        grid=(n // tb,),
        in_specs=[blk, blk],
        out_specs=[blk, blk, half, half],
        out_shape=[jax.ShapeDtypeStruct((PEER_HEADS, nkeys, n), F32),
                   jax.ShapeDtypeStruct((PEER_HEADS, nkeys, n), F32),
                   jax.ShapeDtypeStruct((PEER_HEADS, nkeys // 2, n), jnp.uint32),
                   jax.ShapeDtypeStruct((PEER_HEADS, nkeys // 2, n), jnp.uint32)],
        scratch_shapes=[pltpu.VMEM((PEER_CAND_ROWS, tb), F32)],
        compiler_params=_params(("parallel",)),
        name="peer_gates",
    )(s1, s2)


def _gelu(x):
    return 0.5 * x * (1.0 + lax.erf(x * (2.0 ** -0.5)))


PEER_ROW_GROUP = 4


def _peer_dense_kernel(h2t_ref, u_ref, vt_ref, cnt_ref, a_ref, rk_ref, bx_ref, x_ref, mod_ref,
                       xo_ref, yt_ref, at_ref, w_ref, *, nkeys):
    e = pl.program_id(1)

    @pl.when(e == 0)
    def _():
        yt_ref[...] = jnp.zeros(yt_ref.shape, F32)

    at_ref[...] = jnp.dot(u_ref[...], h2t_ref[...], preferred_element_type=F32)
    per_step = u_ref.shape[0] // nkeys
    first_key = pl.multiple_of(e * per_step, per_step)
    zero = jnp.zeros((nkeys, LANES), BF16)
    for tt in range(h2t_ref.shape[1] // LANES):
        cols = slice(tt * LANES, (tt + 1) * LANES)
        for i0 in range(0, per_step, PEER_ROW_GROUP):
            g = [None] * PEER_ROW_GROUP
            for h in range(PEER_HEADS):
                rk = pltpu.bitcast(rk_ref[h, :, cols], BF16)
                bx = pltpu.bitcast(bx_ref[h, :, cols], BF16)
                cnts = cnt_ref[h, pl.ds(first_key, per_step), cols]
                arows = a_ref[h, pl.ds(first_key, per_step), cols]
                for ii in range(PEER_ROW_GROUP):
                    cnt = cnts[i0 + ii:i0 + ii + 1, :].astype(BF16)
                    term = jnp.where(rk < cnt, bx, zero) * arows[i0 + ii:i0 + ii + 1, :].astype(BF16)
                    g[ii] = term if g[ii] is None else g[ii] + term
            for ii in range(PEER_ROW_GROUP):
                rows = slice((i0 + ii) * nkeys, (i0 + ii + 1) * nkeys)
                w_ref[rows, cols] = g[ii] * _gelu(at_ref[rows, cols]).astype(BF16)
    yt_ref[...] += jnp.dot(vt_ref[...], w_ref[...], preferred_element_type=F32)

    @pl.when(e == pl.num_programs(1) - 1)
    def _():
        xo_ref[...] = x_ref[...] + mod_ref[5:6, :] * yt_ref[...].T


def _peer_dense(h2t, u, vt, cnt, a, rk, bx, x, modblk):
    n, d = x.shape
    nexp = u.shape[0]
    nkeys = cnt.shape[1]
    tb = TOKEN_BLOCK
    eb = min(PEER_EXPERT_BLOCK, nexp)
    assert nkeys == LANES and eb % (SUBLANES * nkeys) == 0 and SUBLANES % PEER_ROW_GROUP == 0
    keyed = pl.BlockSpec((PEER_HEADS, nkeys, tb), lambda t, e: (0, 0, t))
    packed = pl.BlockSpec((PEER_HEADS, nkeys // 2, tb), lambda t, e: (0, 0, t))
    return pl.pallas_call(
        functools.partial(_peer_dense_kernel, nkeys=nkeys),
        grid=(n // tb, nexp // eb),
        in_specs=[pl.BlockSpec((d, tb), lambda t, e: (0, t)),
                  pl.BlockSpec((eb, d), lambda t, e: (e, 0)),
                  pl.BlockSpec((d, eb), lambda t, e: (0, e)),
                  keyed, keyed, packed, packed,
                  pl.BlockSpec((tb, d), lambda t, e: (t, 0)),
                  pl.BlockSpec((None, 6, d), lambda t, e: (t, 0, 0))],
        out_specs=pl.BlockSpec((tb, d), lambda t, e: (t, 0)),
        out_shape=jax.ShapeDtypeStruct((n, d), F32),
        scratch_shapes=[pltpu.VMEM((d, tb), F32), pltpu.VMEM((eb, tb), F32), pltpu.VMEM((eb, tb), BF16)],
        compiler_params=_params(("parallel", "arbitrary")),
        name="peer_dense",
    )(h2t, u, vt, cnt, a, rk, bx, x, modblk)


def _final_norm_kernel(x_ref, g_ref, o_ref):
    o_ref[...] = _rms(x_ref[...]) * g_ref[...]


def _final_norm(x, g):
    n, d = x.shape
    tm = TOKEN_BLOCK
    return pl.pallas_call(
        _final_norm_kernel,
        grid=(n // tm,),
        in_specs=[pl.BlockSpec((tm, d), lambda i: (i, 0)), pl.BlockSpec((1, d), lambda i: (0, 0))],
        out_specs=pl.BlockSpec((tm, d), lambda i: (i, 0)),
        out_shape=jax.ShapeDtypeStruct((n, d), F32),
        compiler_params=_params(("parallel",)),
        name="final_norm",
    )(x, g.reshape(1, d))


def _rope_swap_columns(width):
    half = HEAD_DIM // 2
    nf = half // 2
    perm = np.zeros(width, np.int32)
    sign = np.zeros(width, np.float32)
    for c in range(width):
        r = c % half
        if r < nf:
            perm[c], sign[c] = c + nf, -1.0
        else:
            perm[c], sign[c] = c - nf, 1.0
    return perm, sign


def _pack_w_in(w_in):
    d = w_in.shape[0]
    o_aq = POOL_WIDTH + 2 * GLA_QK + 2 * GLA_WIDTH + 2 * GLA_GATE_RANK
    o_ak = o_aq + ATTN_WIDTH
    o_av = o_ak + KV_WIDTH
    w_aq = w_in[:, o_aq:o_ak]
    w_ak = w_in[:, o_ak:o_av]
    pq, sq = _rope_swap_columns(ATTN_WIDTH)
    pk, sk = _rope_swap_columns(KV_WIDTH)
    gz = jnp.pad(w_in[:, o_aq - 2 * GLA_GATE_RANK:o_aq], ((0, 0), (0, LANES - 2 * GLA_GATE_RANK)))
    cols = [w_in[:, :o_aq - 2 * GLA_GATE_RANK], gz, w_aq, w_ak, w_in[:, o_av:],
            w_aq[:, pq] * sq[None, :], w_ak[:, pk] * sk[None, :]]
    return jnp.concatenate(cols, axis=1).astype(BF16)


def _block_diag(w):
    g, c, _ = w.shape
    eye = jnp.eye(g, dtype=w.dtype)
    return jnp.einsum('gcd,gh->gchd', w, eye).reshape(g * c, g * c)


def _rope_tables(n_ctx, n_lat_seq, n_lat_batch):
    half = HEAD_DIM // 2
    nf = half // 2
    freqs = ROPE_THETA ** (-jnp.arange(nf, dtype=F32) / nf)
    rows = n_lat_seq // GRID_W
    pos_row = jnp.repeat(jnp.arange(rows), GRID_W).astype(F32)
    pos_col = jnp.tile(jnp.arange(GRID_W), rows).astype(F32)
    ar = pos_row[:, None] * freqs[None, :]
    ac = pos_col[:, None] * freqs[None, :]
    cos = jnp.concatenate([jnp.cos(ar)] * 2 + [jnp.cos(ac)] * 2, axis=1)
    sin = jnp.concatenate([jnp.sin(ar)] * 2 + [jnp.sin(ac)] * 2, axis=1)
    reps = LANES // HEAD_DIM
    cos = jnp.tile(jnp.tile(cos, (1, reps)), (n_lat_batch, 1))
    sin = jnp.tile(jnp.tile(sin, (1, reps)), (n_lat_batch, 1))
    cos = jnp.concatenate([jnp.ones((n_ctx, LANES), F32), cos], axis=0)
    sin = jnp.concatenate([jnp.zeros((n_ctx, LANES), F32), sin], axis=0)
    return cos, sin


def kernel(x_prompt, x_sample, cache_k, cache_v, state_fwd, state_bwd, c, c_ctx, w_ada, b_ada, norm1_g, norm2_g,
           w_in, pool_w, pool_scale, gla_gate_w_f, gla_gate_b_f, gla_gate_w_b, gla_gate_b_b, gla_norm_g,
           attn_sink, w_out, peer_wq, peer_subkeys, peer_u, peer_v, final_norm_g):
    nb, seq, d = x_prompt.shape
    nlb, lseq, _ = x_sample.shape
    depth = w_ada.shape[0]
    n_ctx, n_lat = nb * seq, nlb * lseq
    n = n_ctx + n_lat
    tm = TOKEN_BLOCK
    nkeys = peer_subkeys.shape[3]
    assert d == D_MODEL and n_ctx % tm == 0 and lseq % tm == 0 and n_ctx % lseq == 0
    assert seq % GLA_CHUNK == 0 and lseq % GLA_BLOCK == 0 and lseq % GRID_W == 0 and lseq >= 3 * ATTN_BLOCK
    assert nlb + 1 <= SUBLANES and nkeys % SUBLANES == 0

    x0 = jnp.concatenate([x_prompt.reshape(n_ctx, d), x_sample.reshape(n_lat, d)], axis=0)
    cvec = jnp.zeros((SUBLANES, d), F32).at[0].set(c_ctx).at[1:1 + nlb].set(c)
    mods = _ada_mods(cvec, w_ada, b_ada)
    blk_row = np.concatenate([np.zeros(n_ctx // tm, np.int32),
                              1 + np.repeat(np.arange(nlb, dtype=np.int32), lseq // tm)])
    modblk = mods[:, blk_row, :].reshape(depth, n // tm, 6, d)
    cos, sin = _rope_tables(n_ctx, lseq, nlb)
    gla_consts = _gla_consts()
    hd = np.arange(GLA_WIDTH) // GLA_DV
    hmean = jnp.asarray((hd[:, None] == hd[None, :]).astype(np.float32) / GLA_DV)
    zero_state = jnp.zeros((nb, GLA_WIDTH, GLA_QK), F32)

    def layer(x, lp):
        (mod_l, n1, n2, w_in_l, pool_w_l, pool_scale_l, gw_f, gb_f, gw_b, gb_b, gnorm, sink, w_out_l,
         wq, subk, pu, pv, ck, cv, sf, sb) = lp
        w_big = _pack_w_in(w_in_l)
        w2 = jnp.zeros((LANES, 2 * GLA_QK), F32)
        w2 = w2.at[:GLA_GATE_RANK, :GLA_QK].set(gw_f).at[GLA_GATE_RANK:2 * GLA_GATE_RANK, GLA_QK:].set(gw_b)
        b2 = jnp.concatenate([gb_f, gb_b]).reshape(1, 2 * GLA_QK)
        pp, gq, gk, gv, gg, la_f, la_b, aq, ak, av = _inproj(x, mod_l, n1.reshape(1, d), w_big, cos, sin, w2, b2)

        w_bd = _block_diag(pool_w_l).astype(BF16)
        scale = pool_scale_l.reshape(1, POOL_WIDTH)
        y_pool = jnp.concatenate([_pool(pp, w_bd, scale, seq, 0, nb),
                                  _pool(pp, w_bd, scale, lseq, n_ctx // lseq, nlb)], axis=0)

        lblk = min(GLA_BLOCK, lseq)
        of_c, sf_c = _gla(gq, gk, gv, la_f, zero_state, gla_consts, seq, 0, nb, False)
        ob_c, sb_c = _gla(gq, gk, gv, la_b, zero_state, gla_consts, seq, 0, nb, True)
        of_l, _ = _gla(gq, gk, gv, la_f, _state_to_blockdiag_t(sf), gla_consts, lseq, n_ctx // lblk, nlb, False)
        ob_l, _ = _gla(gq, gk, gv, la_b, _state_to_blockdiag_t(sb), gla_consts, lseq, n_ctx // lblk, nlb, True)
        o_f = jnp.concatenate([of_c, of_l], axis=0)
        o_b = jnp.concatenate([ob_c, ob_l], axis=0)

        sink_b = jnp.broadcast_to(sink.reshape(ATTN_HEADS, 1), (ATTN_HEADS, LANES))
        y_attn = jnp.concatenate([_ctx_attn(aq, ak, av, sink_b, seq, nb),
                                  _lat_attn(aq, ak, av, ck, cv, sink_b, lseq, n_ctx, nlb)], axis=0)

        x1, h2t = _outproj(y_pool, o_f, o_b, gg, y_attn, x, mod_l, gnorm.reshape(1, GLA_WIDTH), hmean,
                           w_out_l.astype(BF16), n2.reshape(1, d))

        wqt = wq.T.astype(BF16)
        sk = subk.reshape(2 * PEER_HEADS, nkeys, PEER_HALF).astype(BF16)
        s1, s2 = _peer_scores(h2t, wqt, sk)
        cnt, a, rk, bx = _peer_gates(s1, s2)
        x2 = _peer_dense(h2t, pu.astype(BF16), pv.T.astype(BF16), cnt, a, rk, bx, x1, mod_l)
        return x2, (ak[:n_ctx], av[:n_ctx], _blockdiag_t_to_state(sf_c), _blockdiag_t_to_state(sb_c))

    past = cache_k.shape[2]
    xs = (modblk, norm1_g, norm2_g, w_in, pool_w, pool_scale, gla_gate_w_f, gla_gate_b_f, gla_gate_w_b,
          gla_gate_b_b, gla_norm_g, attn_sink, w_out, peer_wq, peer_subkeys, peer_u, peer_v,
          jnp.swapaxes(cache_k, 0, 1).reshape(depth, nlb, past, KV_WIDTH),
          jnp.swapaxes(cache_v, 0, 1).reshape(depth, nlb, past, KV_WIDTH),
          jnp.swapaxes(state_fwd, 0, 1), jnp.swapaxes(state_bwd, 0, 1))
    x_fin, (ks, vs, sfs, sbs) = lax.scan(layer, x0, xs)

    y = _final_norm(x_fin, final_norm_g)
    y_prompt = y[:n_ctx].reshape(nb, seq, d)
    y_sample = y[n_ctx:].reshape(nlb, lseq, d)
    new_k = jnp.swapaxes(ks.reshape(depth, nb, seq, ATTN_KV_HEADS, HEAD_DIM), 0, 1)
    new_v = jnp.swapaxes(vs.reshape(depth, nb, seq, ATTN_KV_HEADS, HEAD_DIM), 0, 1)
    return (y_prompt, y_sample, new_k, new_v, jnp.swapaxes(sfs, 0, 1), jnp.swapaxes(sbs, 0, 1))
```

```python
import functools
import math

import numpy as np
import jax
import jax.numpy as jnp
from jax import lax
from jax.experimental import pallas as pl
from jax.experimental.pallas import tpu as pltpu

F32 = jnp.float32
BF16 = jnp.bfloat16
HIGHEST = lax.Precision.HIGHEST

D_MODEL = 1024
GRID_W = 64
EPS = 1e-6
POOL_WIDTH = 256
POOL_GROUPS = 4
POOL_GROUP_DIM = 64
POOL_WINDOWS = (2, 4, 8, 16)
GLA_HEADS = 4
GLA_DV = 64
GLA_DK = 32
GLA_QK = GLA_HEADS * GLA_DK
GLA_WIDTH = GLA_HEADS * GLA_DV
GLA_GATE_RANK = 16
GLA_GATE_TAU = 16.0
GLA_CHUNK = 64
ATTN_HEADS = 8
ATTN_KV_HEADS = 2
ATTN_GROUP = 4
HEAD_DIM = 64
ATTN_WIDTH = ATTN_HEADS * HEAD_DIM
KV_WIDTH = ATTN_KV_HEADS * HEAD_DIM
WINDOW = 128
ATTN_BLOCK = 128
ROPE_THETA = 10000.0
NEG_INF = -1e30
PEER_HEADS = 8
PEER_HALF = 64
PEER_TOPK = 16

LANES = 128
SUBLANES = 8
VMEM_LIMIT = 56 * 1024 * 1024

TOKEN_BLOCK = 512
GLA_BLOCK = 256
PEER_EXPERT_BLOCK = 1024

C_POOL, C_GQ, C_GK, C_GV, C_GG, C_GZ, C_AQ, C_AK, C_AV, C_AQS, C_AKS, C_END = (
    0, 256, 384, 512, 768, 1024, 1152, 1664, 1792, 1920, 2432, 2560)


def _params(sem):
    return pltpu.CompilerParams(dimension_semantics=sem, vmem_limit_bytes=VMEM_LIMIT)


def _bdot(a, b):
    return jnp.dot(a.astype(BF16), b.astype(BF16), preferred_element_type=F32)


def _fdot(a, b):
    return jnp.dot(a, b, preferred_element_type=F32, precision=HIGHEST)


def _sigmoid(x):
    return 1.0 / (1.0 + jnp.exp(-x))


def _rms(x):
    return x * lax.rsqrt(jnp.mean(x * x, axis=-1, keepdims=True) + EPS)


def _ada_kernel(c_ref, w_ref, b_ref, o_ref):
    c = c_ref[...]
    o_ref[...] = _fdot(c * _sigmoid(c), w_ref[...]) + b_ref[...]


def _ada_mods(cvec, w_ada, b_ada):
    depth, d, six_d = w_ada.shape
    nj = six_d // d
    return pl.pallas_call(
        _ada_kernel,
        grid=(depth, nj),
        in_specs=[pl.BlockSpec((SUBLANES, d), lambda l, j: (0, 0)),
                  pl.BlockSpec((None, d, d), lambda l, j: (l, 0, j)),
                  pl.BlockSpec((None, 1, d), lambda l, j: (l, 0, j))],
        out_specs=pl.BlockSpec((None, SUBLANES, d), lambda l, j: (l, 0, j)),
        out_shape=jax.ShapeDtypeStruct((depth, SUBLANES, six_d), F32),
        compiler_params=_params(("parallel", "parallel")),
        name="ada_mod",
    )(cvec, w_ada, b_ada.reshape(depth, 1, six_d))


def _inproj_kernel(x_ref, mod_ref, n1_ref, w_ref, cos_ref, sin_ref, w2_ref, b2_ref,
                   pp_ref, gq_ref, gk_ref, gv_ref, gg_ref, laf_ref, lab_ref, aq_ref, ak_ref, av_ref):
    h = _rms(x_ref[...]) * n1_ref[...]
    h = (h * (1.0 + mod_ref[1:2, :]) + mod_ref[0:1, :]).astype(BF16)

    def proj(lo, hi):
        return jnp.dot(h, w_ref[:, lo:hi], preferred_element_type=F32)

    pp_ref[...] = proj(C_POOL, C_GQ)
    gq_ref[...] = proj(C_GQ, C_GK) * (GLA_DK ** -0.5)
    gk_ref[...] = proj(C_GK, C_GV)
    gv_ref[...] = proj(C_GV, C_GG)
    gg_ref[...] = proj(C_GG, C_GZ)
    logit = _fdot(proj(C_GZ, C_AQ), w2_ref[...]) + b2_ref[...]
    la = (jnp.minimum(logit, 0.0) - jnp.log1p(jnp.exp(-jnp.abs(logit)))) * (1.0 / GLA_GATE_TAU)
    laf_ref[...] = la[:, :GLA_QK]
    lab_ref[...] = la[:, GLA_QK:]
    cos = cos_ref[...]
    sin = sin_ref[...]
    cos4 = jnp.concatenate([cos] * (ATTN_WIDTH // LANES), axis=1)
    sin4 = jnp.concatenate([sin] * (ATTN_WIDTH // LANES), axis=1)
    aq_ref[...] = (proj(C_AQ, C_AK) * cos4 + proj(C_AQS, C_AKS) * sin4) * (HEAD_DIM ** -0.5)
    ak_ref[...] = proj(C_AK, C_AV) * cos + proj(C_AKS, C_END) * sin
    av_ref[...] = proj(C_AV, C_AQS)


def _inproj(x, modblk, n1, w_big, cos, sin, w2, b2):
    n, d = x.shape
    tm = TOKEN_BLOCK
    widths = (POOL_WIDTH, GLA_QK, GLA_QK, GLA_WIDTH, GLA_WIDTH, GLA_QK, GLA_QK, ATTN_WIDTH, KV_WIDTH, KV_WIDTH)
    row = lambda w: pl.BlockSpec((tm, w), lambda i: (i, 0))
    full = lambda a: pl.BlockSpec(a.shape, lambda i: (0,) * a.ndim)
    return pl.pallas_call(
        _inproj_kernel,
        grid=(n // tm,),
        in_specs=[row(d), pl.BlockSpec((None, 6, d), lambda i: (i, 0, 0)), full(n1), full(w_big),
                  row(LANES), row(LANES), full(w2), full(b2)],
        out_specs=[row(w) for w in widths],
        out_shape=[jax.ShapeDtypeStruct((n, w), F32) for w in widths],
        compiler_params=_params(("parallel",)),
        name="in_proj",
    )(x, modblk, n1, w_big, cos, sin, w2, b2)


def _pool_kernel(x_ref, w_ref, scale_ref, o_ref):
    t = x_ref.shape[0]
    pad = 32
    n = t + pad
    x = x_ref[...]
    xp = jnp.concatenate([x, jnp.zeros((pad, POOL_WIDTH), F32)], axis=0)
    back = {1: xp}
    for w in (2, 4, 8, 16):
        back[w] = back[w // 2] + pltpu.roll(back[w // 2], w // 2, 0)
    grp = lax.broadcasted_iota(jnp.int32, (1, POOL_WIDTH), 1) // POOL_GROUP_DIM
    tok = lax.broadcasted_iota(jnp.int32, (t, 1), 0)
    wsum = None
    cnt = None
    for g, w in enumerate(POOL_WINDOWS):
        left = w // 2
        right = w - 1 - left
        ws = back[w] if right == 0 else pltpu.roll(back[w], n - right, 0)
        ws = ws[:t]
        c = (jnp.minimum(tok + right + 1, t) - jnp.maximum(tok - left, 0)).astype(F32)
        if wsum is None:
            wsum, cnt = ws, jnp.broadcast_to(c, (t, POOL_WIDTH))
        else:
            wsum = jnp.where(grp == g, ws, wsum)
            cnt = jnp.where(grp == g, c, cnt)
    diff = wsum / cnt - x
    o_ref[...] = _bdot(diff, w_ref[...]) * scale_ref[...]


def _pool(pp, w_bd, scale, seq, first_block, nseq):
    return pl.pallas_call(
        _pool_kernel,
        grid=(nseq,),
        in_specs=[pl.BlockSpec((seq, POOL_WIDTH), lambda b: (first_block + b, 0)),
                  pl.BlockSpec(w_bd.shape, lambda b: (0, 0)),
                  pl.BlockSpec(scale.shape, lambda b: (0, 0))],
        out_specs=pl.BlockSpec((seq, POOL_WIDTH), lambda b: (b, 0)),
        out_shape=jax.ShapeDtypeStruct((nseq * seq, POOL_WIDTH), F32),
        compiler_params=_params(("parallel",)),
        name="pool_mix",
    )(pp, w_bd, scale)


def _gla_rows(reverse):
    out = []
    for s in range(GLA_CHUNK):
        g = s // SUBLANES
        out.append((0, SUBLANES * (g + 1)) if reverse else (SUBLANES * g, GLA_CHUNK))
    return out


GLA_PAIR_ROWS = sum(hi - lo for lo, hi in _gla_rows(False))


def _gla_kernel(q_ref, k_ref, v_ref, la_ref, s0_ref, hexp_ref, tri_ref, bd_ref,
                o_ref, sout_ref, st_ref, b_ref, p_ref, z_ref, *, reverse, nchunk):
    j = pl.program_id(1)

    @pl.when(j == 0)
    def _():
        st_ref[...] = s0_ref[...]

    rows = _gla_rows(reverse)
    ngrp = GLA_CHUNK // SUBLANES
    for c in (range(nchunk - 1, -1, -1) if reverse else range(nchunk)):
        r0 = c * GLA_CHUNK
        q = q_ref[r0:r0 + GLA_CHUNK, :]
        k = k_ref[r0:r0 + GLA_CHUNK, :]
        v = v_ref[r0:r0 + GLA_CHUNK, :]
        b = _fdot(tri_ref[...], la_ref[r0:r0 + GLA_CHUNK, :])
        b_ref[...] = b
        blast = b[0:1, :] if reverse else b[GLA_CHUNK - 1:GLA_CHUNK, :]
        st = st_ref[...]
        inter = lax.dot_general((q * jnp.exp(b)).astype(BF16), st.astype(BF16),
                                (((1,), (1,)), ((), ())), preferred_element_type=F32)
        off = 0
        for s, (lo, hi) in enumerate(rows):
            tio = lax.broadcasted_iota(jnp.int32, (hi - lo, 1), 0) + lo
            valid = (tio <= s) if reverse else (tio >= s)
            dlt = jnp.where(valid, b[lo:hi] - b_ref[s:s + 1, :], 0.0)
            p_ref[off:off + hi - lo, :] = jnp.where(valid, q[lo:hi] * k_ref[r0 + s:r0 + s + 1, :] * jnp.exp(dlt), 0.0)
            off += hi - lo
        z_ref[...] = jnp.dot(p_ref[...].astype(BF16), hexp_ref[...], preferred_element_type=F32)
        oacc = [None] * ngrp
        off = 0
        for g in range(ngrp):
            lo, hi = rows[g * SUBLANES]
            cg = None
            for s in range(g * SUBLANES, (g + 1) * SUBLANES):
                term = z_ref[off:off + hi - lo, :] * v_ref[r0 + s:r0 + s + 1, :]
                cg = term if cg is None else cg + term
                off += hi - lo
            for rg in range(lo // SUBLANES, hi // SUBLANES):
                piece = cg[(rg - lo // SUBLANES) * SUBLANES:(rg - lo // SUBLANES + 1) * SUBLANES]
                oacc[rg] = piece if oacc[rg] is None else oacc[rg] + piece
        o_ref[r0:r0 + GLA_CHUNK, :] = jnp.concatenate(oacc, axis=0) + inter
        ke = k * jnp.exp(blast - b)
        kv = jnp.dot(v.T.astype(BF16), ke.astype(BF16), preferred_element_type=F32)
        st_ref[...] = st * jnp.exp(blast) + kv * bd_ref[...]

    @pl.when(j == pl.num_programs(1) - 1)
    def _():
        sout_ref[...] = st_ref[...]


def _gla_consts():
    hd = np.arange(GLA_QK) // GLA_DK
    he = np.arange(GLA_WIDTH) // GLA_DV
    hexp = (hd[:, None] == he[None, :]).astype(np.float32)
    bd = hexp.T.copy()
    t = np.arange(GLA_CHUNK)
    tri_f = (t[None, :] <= t[:, None]).astype(np.float32)
    tri_b = (t[None, :] >= t[:, None]).astype(np.float32)
    return jnp.asarray(hexp, BF16), jnp.asarray(bd, F32), jnp.asarray(tri_f), jnp.asarray(tri_b)


def _gla(gq, gk, gv, la, s0t, consts, seq, first_block, nseq, reverse):
    hexp, bd, tri_f, tri_b = consts
    tri = tri_b if reverse else tri_f
    blk = min(GLA_BLOCK, seq)
    nblk = seq // blk
    if reverse:
        tok = lambda b, j: (first_block + b * nblk + (nblk - 1 - j), 0)
        otok = lambda b, j: (b * nblk + (nblk - 1 - j), 0)
    else:
        tok = lambda b, j: (first_block + b * nblk + j, 0)
        otok = lambda b, j: (b * nblk + j, 0)
    const = lambda a: pl.BlockSpec(a.shape, lambda b, j: (0,) * a.ndim)
    return pl.pallas_call(
        functools.partial(_gla_kernel, reverse=reverse, nchunk=blk // GLA_CHUNK),
        grid=(nseq, nblk),
        in_specs=[pl.BlockSpec((blk, GLA_QK), tok), pl.BlockSpec((blk, GLA_QK), tok),
                  pl.BlockSpec((blk, GLA_WIDTH), tok), pl.BlockSpec((blk, GLA_QK), tok),
                  pl.BlockSpec((None, GLA_WIDTH, GLA_QK), lambda b, j: (b, 0, 0)),
                  const(hexp), const(tri), const(bd)],
        out_specs=[pl.BlockSpec((blk, GLA_WIDTH), otok),
                   pl.BlockSpec((None, GLA_WIDTH, GLA_QK), lambda b, j: (b, 0, 0))],
        out_shape=[jax.ShapeDtypeStruct((nseq * seq, GLA_WIDTH), F32),
                   jax.ShapeDtypeStruct((nseq, GLA_WIDTH, GLA_QK), F32)],
        scratch_shapes=[pltpu.VMEM((GLA_WIDTH, GLA_QK), F32), pltpu.VMEM((GLA_CHUNK, GLA_QK), F32),
                        pltpu.VMEM((GLA_PAIR_ROWS, GLA_QK), F32), pltpu.VMEM((GLA_PAIR_ROWS, GLA_WIDTH), F32)],
        compiler_params=_params(("parallel", "arbitrary")),
        name="gla_bwd" if reverse else "gla_fwd",
    )(gq, gk, gv, la, s0t, hexp, tri, bd)


def _state_to_blockdiag_t(s):
    b = s.shape[0]
    eye = jnp.eye(GLA_HEADS, dtype=s.dtype)
    return jnp.einsum('bhde,hg->bhegd', s, eye).reshape(b, GLA_WIDTH, GLA_QK)


def _blockdiag_t_to_state(st):
    b = st.shape[0]
    eye = jnp.eye(GLA_HEADS, dtype=st.dtype)
    return jnp.einsum('bhegd,hg->bhde', st.reshape(b, GLA_HEADS, GLA_DV, GLA_HEADS, GLA_DK), eye)


def _stack_heads(q_ref, kv):
    return jnp.concatenate([q_ref[:, (kv * ATTN_GROUP + r) * HEAD_DIM:(kv * ATTN_GROUP + r + 1) * HEAD_DIM]
                            for r in range(ATTN_GROUP)], axis=0)


def _sink_column(sink_ref, kv, rows):
    return jnp.concatenate([jnp.broadcast_to(sink_ref[kv * ATTN_GROUP + r:kv * ATTN_GROUP + r + 1, 0:1], (rows, 1))
                            for r in range(ATTN_GROUP)], axis=0)


def _qk(q, k):
    return lax.dot_general(q.astype(BF16), k.astype(BF16), (((1,), (1,)), ((), ())), preferred_element_type=F32)


def _ctx_attn_kernel(q_ref, k_ref, v_ref, sink_ref, o_ref):
    t = q_ref.shape[0]
    for kv in range(ATTN_KV_HEADS):
        k = k_ref[:, kv * HEAD_DIM:(kv + 1) * HEAD_DIM]
        v = v_ref[:, kv * HEAD_DIM:(kv + 1) * HEAD_DIM]
        s = _qk(_stack_heads(q_ref, kv), k)
        sink = _sink_column(sink_ref, kv, t)
        m = jnp.maximum(jnp.max(s, axis=-1, keepdims=True), sink)
        p = jnp.exp(s - m)
        den = jnp.sum(p, axis=-1, keepdims=True) + jnp.exp(sink - m)
        o = _bdot(p, v) / den
        for r in range(ATTN_GROUP):
            h = kv * ATTN_GROUP + r
            o_ref[:, h * HEAD_DIM:(h + 1) * HEAD_DIM] = o[r * t:(r + 1) * t]


def _ctx_attn(aq, ak, av, sink_b, seq, nseq):
    tok = lambda w: pl.BlockSpec((seq, w), lambda b: (b, 0))
    return pl.pallas_call(
        _ctx_attn_kernel,
        grid=(nseq,),
        in_specs=[tok(ATTN_WIDTH), tok(KV_WIDTH), tok(KV_WIDTH), pl.BlockSpec(sink_b.shape, lambda b: (0, 0))],
        out_specs=tok(ATTN_WIDTH),
        out_shape=jax.ShapeDtypeStruct((nseq * seq, ATTN_WIDTH), F32),
        compiler_params=_params(("parallel",)),
        name="ctx_attn",
    )(aq, ak, av, sink_b)


def _lat_attn_kernel(q_ref, k_ref, v_ref, ck_ref, cv_ref, sink_ref, o_ref):
    t = k_ref.shape[0]
    blk = ATTN_BLOCK
    span = 3 * blk
    i = pl.program_id(1)
    start = pl.multiple_of(jnp.clip((i - 1) * blk, 0, t - span), blk)
    kw = k_ref[pl.ds(start, span), :]
    vw = v_ref[pl.ds(start, span), :]
    qpos = i * blk + lax.broadcasted_iota(jnp.int32, (blk, 1), 0)
    kpos = start + lax.broadcasted_iota(jnp.int32, (1, span), 1)
    inwin = jnp.abs(qpos - kpos) <= WINDOW
    inwin = jnp.concatenate([inwin] * ATTN_GROUP, axis=0)
    for kv in range(ATTN_KV_HEADS):
        lanes = slice(kv * HEAD_DIM, (kv + 1) * HEAD_DIM)
        q = _stack_heads(q_ref, kv)
        s_loc = jnp.where(inwin, _qk(q, kw[:, lanes]), NEG_INF)
        s_ctx = _qk(q, ck_ref[:, lanes])
        sink = _sink_column(sink_ref, kv, blk)
        m = jnp.maximum(jnp.maximum(jnp.max(s_loc, axis=-1, keepdims=True),
                                    jnp.max(s_ctx, axis=-1, keepdims=True)), sink)
        p_loc = jnp.exp(s_loc - m)
        p_ctx = jnp.exp(s_ctx - m)
        den = (jnp.sum(p_loc, axis=-1, keepdims=True) + jnp.sum(p_ctx, axis=-1, keepdims=True)
               + jnp.exp(sink - m))
        o = (_bdot(p_loc, vw[:, lanes]) + _bdot(p_ctx, cv_ref[:, lanes])) / den
        for r in range(ATTN_GROUP):
            h = kv * ATTN_GROUP + r
            o_ref[:, h * HEAD_DIM:(h + 1) * HEAD_DIM] = o[r * blk:(r + 1) * blk]


def _lat_attn(aq, ak, av, ck, cv, sink_b, seq, first_tok, nseq):
    nblk = seq // ATTN_BLOCK
    fb_q = first_tok // ATTN_BLOCK
    fb_s = first_tok // seq
    return pl.pallas_call(
        _lat_attn_kernel,
        grid=(nseq, nblk),
        in_specs=[pl.BlockSpec((ATTN_BLOCK, ATTN_WIDTH), lambda b, i: (fb_q + b * nblk + i, 0)),
                  pl.BlockSpec((seq, KV_WIDTH), lambda b, i: (fb_s + b, 0)),
                  pl.BlockSpec((seq, KV_WIDTH), lambda b, i: (fb_s + b, 0)),
                  pl.BlockSpec((None,) + ck.shape[1:], lambda b, i: (b, 0, 0)),
                  pl.BlockSpec((None,) + cv.shape[1:], lambda b, i: (b, 0, 0)),
                  pl.BlockSpec(sink_b.shape, lambda b, i: (0, 0))],
        out_specs=pl.BlockSpec((ATTN_BLOCK, ATTN_WIDTH), lambda b, i: (b * nblk + i, 0)),
        out_shape=jax.ShapeDtypeStruct((nseq * seq, ATTN_WIDTH), F32),
        compiler_params=_params(("parallel", "arbitrary")),
        name="lat_attn",
    )(aq, ak, av, ck, cv, sink_b)


def _outproj_kernel(yp_ref, of_ref, ob_ref, gg_ref, ya_ref, x_ref, mod_ref, gn_ref, hm_ref, w_ref, n2_ref,
                    xo_ref, h2t_ref):
    o = of_ref[...] + ob_ref[...]
    ms = _fdot(o * o, hm_ref[...])
    gg = gg_ref[...]
    y = o * lax.rsqrt(ms + EPS) * gn_ref[...] * (gg * _sigmoid(gg))
    mix = jnp.concatenate([yp_ref[...], y, ya_ref[...]], axis=1)
    xn = x_ref[...] + mod_ref[2:3, :] * _bdot(mix, w_ref[...])
    xo_ref[...] = xn
    h2 = _rms(xn) * n2_ref[...] * (1.0 + mod_ref[4:5, :]) + mod_ref[3:4, :]
    h2t_ref[...] = pltpu.bitcast(h2.T.astype(BF16), jnp.uint32)


def _outproj(ypool, o_f, o_b, gg, yattn, x, modblk, gnorm, hmean, w_out, n2):
    n, d = x.shape
    tm = TOKEN_BLOCK
    row = lambda w: pl.BlockSpec((tm, w), lambda i: (i, 0))
    full = lambda a: pl.BlockSpec(a.shape, lambda i: (0,) * a.ndim)
    return pl.pallas_call(
        _outproj_kernel,
        grid=(n // tm,),
        in_specs=[row(POOL_WIDTH), row(GLA_WIDTH), row(GLA_WIDTH), row(GLA_WIDTH), row(ATTN_WIDTH), row(d),
                  pl.BlockSpec((None, 6, d), lambda i: (i, 0, 0)), full(gnorm), full(hmean), full(w_out), full(n2)],
        out_specs=[row(d), pl.BlockSpec((d // 2, tm), lambda i: (0, i))],
        out_shape=[jax.ShapeDtypeStruct((n, d), F32), jax.ShapeDtypeStruct((d // 2, n), jnp.uint32)],
        compiler_params=_params(("parallel",)),
        name="out_proj",
    )(ypool, o_f, o_b, gg, yattn, x, modblk, gnorm, hmean, w_out, n2)


def _peer_score_kernel(h2t_ref, wqt_ref, sk_ref, s1_ref, s2_ref):
    qt = jnp.dot(wqt_ref[...], pltpu.bitcast(h2t_ref[...], BF16), preferred_element_type=F32)
    for h in range(PEER_HEADS):
        for p, out in enumerate((s1_ref, s2_ref)):
            r = (2 * h + p) * PEER_HALF
            out[h] = jnp.dot(sk_ref[2 * h + p], qt[r:r + PEER_HALF].astype(BF16), preferred_element_type=F32)


def _peer_scores(h2t, wqt, sk):
    d, n = h2t.shape
    tb = TOKEN_BLOCK
    nkeys = sk.shape[1]
    out = pl.BlockSpec((PEER_HEADS, nkeys, tb), lambda i: (0, 0, i))
    return pl.pallas_call(
        _peer_score_kernel,
        grid=(n // tb,),
        in_specs=[pl.BlockSpec((d, tb), lambda i: (0, i)),
                  pl.BlockSpec(wqt.shape, lambda i: (0, 0)),
                  pl.BlockSpec(sk.shape, lambda i: (0, 0, 0))],
        out_specs=[out, out],
        out_shape=[jax.ShapeDtypeStruct((PEER_HEADS, nkeys, n), F32)] * 2,
        compiler_params=_params(("parallel",)),
        name="peer_scores",
    )(h2t, wqt, sk)


PEER_CANDS = [(a, b) for a in range(PEER_TOPK) for b in range(PEER_TOPK) if (a + 1) * (b + 1) <= PEER_TOPK]
PEER_CAND_ROWS = -(-len(PEER_CANDS) // SUBLANES) * SUBLANES


def _top_values(cur, count):
    vals = []
    for _ in range(count):
        m = jnp.max(cur, axis=0, keepdims=True)
        vals.append(m)
        cur = jnp.where(cur == m, -jnp.inf, cur)
    return vals


def _peer_gate_kernel(s1_ref, s2_ref, cnt_ref, a_ref, rk_ref, bx_ref, cand_ref):
    for h in range(PEER_HEADS):
        s1 = s1_ref[h]
        s2 = s2_ref[h]
        v1 = _top_values(s1, PEER_TOPK)
        v2 = _top_values(s2, PEER_TOPK)
        cand_ref[...] = jnp.full(cand_ref.shape, -jnp.inf, F32)
        for r, (a, b) in enumerate(PEER_CANDS):
            cand_ref[r:r + 1, :] = v1[a] + v2[b]
        best = _top_values(cand_ref[...], PEER_TOPK)
        z = None
        for val in best:
            e = jnp.exp(val - best[0])
            z = e if z is None else z + e
        tau = best[-1]
        rk = None
        cnt = None
        for b in range(PEER_TOPK):
            above = jnp.where(v2[b] > s2, 1.0, 0.0)
            reach = jnp.where(s1 + v2[b] >= tau, 1.0, 0.0)
            rk = above if rk is None else rk + above
            cnt = reach if cnt is None else cnt + reach
        cnt_ref[h] = cnt
        a_ref[h] = jnp.exp(s1 - v1[0]) * (0.5 / z)
        rk_ref[h] = pltpu.bitcast(rk.astype(BF16), jnp.uint32)
        bx_ref[h] = pltpu.bitcast(jnp.exp(s2 - v2[0]).astype(BF16), jnp.uint32)


def _peer_gates(s1, s2):
    _, nkeys, n = s1.shape
    tb = TOKEN_BLOCK
    blk = pl.BlockSpec((PEER_HEADS, nkeys, tb), lambda i: (0, 0, i))
    half = pl.BlockSpec((PEER_HEADS, nkeys // 2, tb), lambda i: (0, 0, i))
    return pl.pallas_call(
        _peer_gate_kernel,
        grid=(n // tb,),
        in_specs=[blk, blk],
        out_specs=[blk, blk, half, half],
        out_shape=[jax.ShapeDtypeStruct((PEER_HEADS, nkeys, n), F32),
                   jax.ShapeDtypeStruct((PEER_HEADS, nkeys, n), F32),
                   jax.ShapeDtypeStruct((PEER_HEADS, nkeys // 2, n), jnp.uint32),
                   jax.ShapeDtypeStruct((PEER_HEADS, nkeys // 2, n), jnp.uint32)],
        scratch_shapes=[pltpu.VMEM((PEER_CAND_ROWS, tb), F32)],
        compiler_params=_params(("parallel",)),
        name="peer_gates",
    )(s1, s2)


PEER_ROW_GROUP = 4
PEER_TILE_ROWS = 64


def _peer_dense_kernel(h2t_ref, u_ref, vt_ref, cnt_ref, a_ref, rk_ref, bx_ref, x_ref, mod_ref,
                       xo_ref, yt_ref, at0_ref, at1_ref, w_ref, *, nkeys):
    e = pl.program_id(1)
    per_step = 2 * u_ref.shape[0] // nkeys
    group_rows = PEER_ROW_GROUP * nkeys

    @pl.when(e == 0)
    def _():
        yt_ref[...] = jnp.zeros(yt_ref.shape, F32)
        at1_ref[...] = jnp.zeros(at1_ref.shape, F32)

    def step(at_new, at_old):
        w_new = w_ref
        live = jnp.where(e >= 1, 1.0, 0.0)
        first_key = pl.multiple_of(jnp.maximum(e - 1, 0) * per_step, per_step)
        zero = jnp.zeros((PEER_TILE_ROWS, LANES), BF16)

        def gate_tiles(i0, tt, jh):
            cols = slice(tt * LANES, (tt + 1) * LANES)
            words = slice(jh * PEER_TILE_ROWS // 2, (jh + 1) * PEER_TILE_ROWS // 2)
            g = [None] * PEER_ROW_GROUP
            for h in range(PEER_HEADS):
                rk = pltpu.bitcast(rk_ref[h, words, cols], BF16)
                bx = pltpu.bitcast(bx_ref[h, words, cols], BF16)
                cnts = cnt_ref[h, pl.ds(first_key, per_step), cols]
                arows = a_ref[h, pl.ds(first_key, per_step), cols] * live
                for ii in range(PEER_ROW_GROUP):
                    cnt = cnts[i0 + ii:i0 + ii + 1, :].astype(BF16)
                    term = jnp.where(rk < cnt, bx, zero) * arows[i0 + ii:i0 + ii + 1, :].astype(BF16)
                    g[ii] = term if g[ii] is None else g[ii] + term
            for ii in range(PEER_ROW_GROUP):
                r0 = (i0 + ii) * nkeys + jh * PEER_TILE_ROWS
                rows = slice(r0, r0 + PEER_TILE_ROWS)
                xb = at_old[rows, cols].astype(BF16)
                w_new[rows, cols] = g[ii] * (xb * (1.0 + lax.erf(xb * (2.0 ** -0.5))))

        at_new[...] = jnp.dot(pltpu.bitcast(u_ref[...], BF16), pltpu.bitcast(h2t_ref[...], BF16),
                              preferred_element_type=F32)
        for i0 in range(0, per_step, PEER_ROW_GROUP):
            for tt in range(at_old.shape[1] // LANES):
                for jh in range(nkeys // PEER_TILE_ROWS):
                    gate_tiles(i0, tt, jh)
            blk = slice(i0 * nkeys, i0 * nkeys + group_rows)
            yt_ref[...] += jnp.dot(pltpu.bitcast(vt_ref[...], BF16)[:, blk], w_new[blk, :],
                                   preferred_element_type=F32)

    @pl.when(e % 2 == 0)
    def _():
        step(at0_ref, at1_ref)

    @pl.when(e % 2 == 1)
    def _():
        step(at1_ref, at0_ref)

    @pl.when(e == pl.num_programs(1) - 1)
    def _():
        xo_ref[...] = x_ref[...] + mod_ref[5:6, :] * yt_ref[...].T


def _pack_row_pairs(x):
    r, c = x.shape
    return lax.bitcast_convert_type(jnp.swapaxes(x.reshape(r // 2, 2, c), -1, -2), jnp.uint32)


def _peer_dense(h2t, u, vt, cnt, a, rk, bx, x, modblk):
    n, d = x.shape
    nexp = 2 * u.shape[0]
    nkeys = cnt.shape[1]
    tb = TOKEN_BLOCK
    eb = min(PEER_EXPERT_BLOCK, nexp)
    assert nkeys == LANES and eb % (SUBLANES * nkeys) == 0 and SUBLANES % PEER_ROW_GROUP == 0
    ne = nexp // eb
    keyed = pl.BlockSpec((PEER_HEADS, nkeys, tb), lambda t, e: (0, 0, t))
    packed = pl.BlockSpec((PEER_HEADS, nkeys // 2, tb), lambda t, e: (0, 0, t))
    return pl.pallas_call(
        functools.partial(_peer_dense_kernel, nkeys=nkeys),
        grid=(n // tb, ne + 1),
        in_specs=[pl.BlockSpec((d // 2, tb), lambda t, e: (0, t)),
                  pl.BlockSpec((eb // 2, d), lambda t, e: (jnp.minimum(e, ne - 1), 0)),
                  pl.BlockSpec((d // 2, eb), lambda t, e: (0, jnp.maximum(e - 1, 0))),
                  keyed, keyed, packed, packed,
                  pl.BlockSpec((tb, d), lambda t, e: (t, 0)),
                  pl.BlockSpec((None, 6, d), lambda t, e: (t, 0, 0))],
        out_specs=pl.BlockSpec((tb, d), lambda t, e: (t, 0)),
        out_shape=jax.ShapeDtypeStruct((n, d), F32),
        scratch_shapes=[pltpu.VMEM((d, tb), F32), pltpu.VMEM((eb, tb), F32), pltpu.VMEM((eb, tb), F32),
                        pltpu.VMEM((eb, tb), BF16)],
        compiler_params=_params(("parallel", "arbitrary")),
        name="peer_dense",
    )(h2t, u, vt, cnt, a, rk, bx, x, modblk)


def _final_norm_kernel(x_ref, g_ref, o_ref):
    o_ref[...] = _rms(x_ref[...]) * g_ref[...]


def _final_norm(x, g):
    n, d = x.shape
    tm = TOKEN_BLOCK
    return pl.pallas_call(
        _final_norm_kernel,
        grid=(n // tm,),
        in_specs=[pl.BlockSpec((tm, d), lambda i: (i, 0)), pl.BlockSpec((1, d), lambda i: (0, 0))],
        out_specs=pl.BlockSpec((tm, d), lambda i: (i, 0)),
        out_shape=jax.ShapeDtypeStruct((n, d), F32),
        compiler_params=_params(("parallel",)),
        name="final_norm",
    )(x, g.reshape(1, d))


def _rope_swap_columns(width):
    half = HEAD_DIM // 2
    nf = half // 2
    perm = np.zeros(width, np.int32)
    sign = np.zeros(width, np.float32)
    for c in range(width):
        r = c % half
        if r < nf:
            perm[c], sign[c] = c + nf, -1.0
        else:
            perm[c], sign[c] = c - nf, 1.0
    return perm, sign


def _pack_w_in(w_in):
    d = w_in.shape[0]
    o_aq = POOL_WIDTH + 2 * GLA_QK + 2 * GLA_WIDTH + 2 * GLA_GATE_RANK
    o_ak = o_aq + ATTN_WIDTH
    o_av = o_ak + KV_WIDTH
    w_aq = w_in[:, o_aq:o_ak]
    w_ak = w_in[:, o_ak:o_av]
    pq, sq = _rope_swap_columns(ATTN_WIDTH)
    pk, sk = _rope_swap_columns(KV_WIDTH)
    gz = jnp.pad(w_in[:, o_aq - 2 * GLA_GATE_RANK:o_aq], ((0, 0), (0, LANES - 2 * GLA_GATE_RANK)))
    cols = [w_in[:, :o_aq - 2 * GLA_GATE_RANK], gz, w_aq, w_ak, w_in[:, o_av:],
            w_aq[:, pq] * sq[None, :], w_ak[:, pk] * sk[None, :]]
    return jnp.concatenate(cols, axis=1).astype(BF16)


def _block_diag(w):
    g, c, _ = w.shape
    eye = jnp.eye(g, dtype=w.dtype)
    return jnp.einsum('gcd,gh->gchd', w, eye).reshape(g * c, g * c)


def _rope_tables(n_ctx, n_lat_seq, n_lat_batch):
    half = HEAD_DIM // 2
    nf = half // 2
    freqs = ROPE_THETA ** (-jnp.arange(nf, dtype=F32) / nf)
    rows = n_lat_seq // GRID_W
    pos_row = jnp.repeat(jnp.arange(rows), GRID_W).astype(F32)
    pos_col = jnp.tile(jnp.arange(GRID_W), rows).astype(F32)
    ar = pos_row[:, None] * freqs[None, :]
    ac = pos_col[:, None] * freqs[None, :]
    cos = jnp.concatenate([jnp.cos(ar)] * 2 + [jnp.cos(ac)] * 2, axis=1)
    sin = jnp.concatenate([jnp.sin(ar)] * 2 + [jnp.sin(ac)] * 2, axis=1)
    reps = LANES // HEAD_DIM
    cos = jnp.tile(jnp.tile(cos, (1, reps)), (n_lat_batch, 1))
    sin = jnp.tile(jnp.tile(sin, (1, reps)), (n_lat_batch, 1))
    cos = jnp.concatenate([jnp.ones((n_ctx, LANES), F32), cos], axis=0)
    sin = jnp.concatenate([jnp.zeros((n_ctx, LANES), F32), sin], axis=0)
    return cos, sin


def kernel(x_prompt, x_sample, cache_k, cache_v, state_fwd, state_bwd, c, c_ctx, w_ada, b_ada, norm1_g, norm2_g,
           w_in, pool_w, pool_scale, gla_gate_w_f, gla_gate_b_f, gla_gate_w_b, gla_gate_b_b, gla_norm_g,
           attn_sink, w_out, peer_wq, peer_subkeys, peer_u, peer_v, final_norm_g):
    nb, seq, d = x_prompt.shape
    nlb, lseq, _ = x_sample.shape
    depth = w_ada.shape[0]
    n_ctx, n_lat = nb * seq, nlb * lseq
    n = n_ctx + n_lat
    tm = TOKEN_BLOCK
    nkeys = peer_subkeys.shape[3]
    assert d == D_MODEL and n_ctx % tm == 0 and lseq % tm == 0 and n_ctx % lseq == 0
    assert seq % GLA_CHUNK == 0 and lseq % GLA_BLOCK == 0 and lseq % GRID_W == 0 and lseq >= 3 * ATTN_BLOCK
    assert nlb + 1 <= SUBLANES and nkeys % SUBLANES == 0

    x0 = jnp.concatenate([x_prompt.reshape(n_ctx, d), x_sample.reshape(n_lat, d)], axis=0)
    cvec = jnp.zeros((SUBLANES, d), F32).at[0].set(c_ctx).at[1:1 + nlb].set(c)
    mods = _ada_mods(cvec, w_ada, b_ada)
    blk_row = np.concatenate([np.zeros(n_ctx // tm, np.int32),
                              1 + np.repeat(np.arange(nlb, dtype=np.int32), lseq // tm)])
    modblk = mods[:, blk_row, :].reshape(depth, n // tm, 6, d)
    cos, sin = _rope_tables(n_ctx, lseq, nlb)
    gla_consts = _gla_consts()
    hd = np.arange(GLA_WIDTH) // GLA_DV
    hmean = jnp.asarray((hd[:, None] == hd[None, :]).astype(np.float32) / GLA_DV)
    zero_state = jnp.zeros((nb, GLA_WIDTH, GLA_QK), F32)

    def layer(x, lp):
        (mod_l, n1, n2, w_in_l, pool_w_l, pool_scale_l, gw_f, gb_f, gw_b, gb_b, gnorm, sink, w_out_l,
         wq, subk, pu, pv, ck, cv, sf, sb) = lp
        w_big = _pack_w_in(w_in_l)
        w2 = jnp.zeros((LANES, 2 * GLA_QK), F32)
        w2 = w2.at[:GLA_GATE_RANK, :GLA_QK].set(gw_f).at[GLA_GATE_RANK:2 * GLA_GATE_RANK, GLA_QK:].set(gw_b)
        b2 = jnp.concatenate([gb_f, gb_b]).reshape(1, 2 * GLA_QK)
        pp, gq, gk, gv, gg, la_f, la_b, aq, ak, av = _inproj(x, mod_l, n1.reshape(1, d), w_big, cos, sin, w2, b2)

        w_bd = _block_diag(pool_w_l).astype(BF16)
        scale = pool_scale_l.reshape(1, POOL_WIDTH)
        y_pool = jnp.concatenate([_pool(pp, w_bd, scale, seq, 0, nb),
                                  _pool(pp, w_bd, scale, lseq, n_ctx // lseq, nlb)], axis=0)

        lblk = min(GLA_BLOCK, lseq)
        of_c, sf_c = _gla(gq, gk, gv, la_f, zero_state, gla_consts, seq, 0, nb, False)
        ob_c, sb_c = _gla(gq, gk, gv, la_b, zero_state, gla_consts, seq, 0, nb, True)
        of_l, _ = _gla(gq, gk, gv, la_f, _state_to_blockdiag_t(sf), gla_consts, lseq, n_ctx // lblk, nlb, False)
        ob_l, _ = _gla(gq, gk, gv, la_b, _state_to_blockdiag_t(sb), gla_consts, lseq, n_ctx // lblk, nlb, True)
        o_f = jnp.concatenate([of_c, of_l], axis=0)
        o_b = jnp.concatenate([ob_c, ob_l], axis=0)

        sink_b = jnp.broadcast_to(sink.reshape(ATTN_HEADS, 1), (ATTN_HEADS, LANES))
        y_attn = jnp.concatenate([_ctx_attn(aq, ak, av, sink_b, seq, nb),
                                  _lat_attn(aq, ak, av, ck, cv, sink_b, lseq, n_ctx, nlb)], axis=0)

        x1, h2t = _outproj(y_pool, o_f, o_b, gg, y_attn, x, mod_l, gnorm.reshape(1, GLA_WIDTH), hmean,
                           w_out_l.astype(BF16), n2.reshape(1, d))

        wqt = wq.T.astype(BF16)
        sk = subk.reshape(2 * PEER_HEADS, nkeys, PEER_HALF).astype(BF16)
        s1, s2 = _peer_scores(h2t, wqt, sk)
        cnt, a, rk, bx = _peer_gates(s1, s2)
        x2 = _peer_dense(h2t, _pack_row_pairs(pu.astype(BF16)), _pack_row_pairs(pv.T.astype(BF16)),
                         cnt, a, rk, bx, x1, mod_l)
        return x2, (ak[:n_ctx], av[:n_ctx], _blockdiag_t_to_state(sf_c), _blockdiag_t_to_state(sb_c))

    past = cache_k.shape[2]
    xs = (modblk, norm1_g, norm2_g, w_in, pool_w, pool_scale, gla_gate_w_f, gla_gate_b_f, gla_gate_w_b,
          gla_gate_b_b, gla_norm_g, attn_sink, w_out, peer_wq, peer_subkeys, peer_u, peer_v,
          jnp.swapaxes(cache_k, 0, 1).reshape(depth, nlb, past, KV_WIDTH),
          jnp.swapaxes(cache_v, 0, 1).reshape(depth, nlb, past, KV_WIDTH),
          jnp.swapaxes(state_fwd, 0, 1), jnp.swapaxes(state_bwd, 0, 1))
    x_fin, (ks, vs, sfs, sbs) = lax.scan(layer, x0, xs)

    y = _final_norm(x_fin, final_norm_g)
    y_prompt = y[:n_ctx].reshape(nb, seq, d)
    y_sample = y[n_ctx:].reshape(nlb, lseq, d)
    new_k = jnp.swapaxes(ks.reshape(depth, nb, seq, ATTN_KV_HEADS, HEAD_DIM), 0, 1)
    new_v = jnp.swapaxes(vs.reshape(depth, nb, seq, ATTN_KV_HEADS, HEAD_DIM), 0, 1)
    return (y_prompt, y_sample, new_k, new_v, jnp.swapaxes(sfs, 0, 1), jnp.swapaxes(sbs, 0, 1))
```

```python
import functools
import math

import numpy as np
import jax
import jax.numpy as jnp
from jax import lax
from jax.experimental import pallas as pl
from jax.experimental.pallas import tpu as pltpu

F32 = jnp.float32
BF16 = jnp.bfloat16
HIGHEST = lax.Precision.HIGHEST

D_MODEL = 1024
GRID_W = 64
EPS = 1e-6
POOL_WIDTH = 256
POOL_GROUPS = 4
POOL_GROUP_DIM = 64
POOL_WINDOWS = (2, 4, 8, 16)
GLA_HEADS = 4
GLA_DV = 64
GLA_DK = 32
GLA_QK = GLA_HEADS * GLA_DK
GLA_WIDTH = GLA_HEADS * GLA_DV
GLA_GATE_RANK = 16
GLA_GATE_TAU = 16.0
GLA_CHUNK = 64
ATTN_HEADS = 8
ATTN_KV_HEADS = 2
ATTN_GROUP = 4
HEAD_DIM = 64
ATTN_WIDTH = ATTN_HEADS * HEAD_DIM
KV_WIDTH = ATTN_KV_HEADS * HEAD_DIM
WINDOW = 128
ATTN_BLOCK = 128
ROPE_THETA = 10000.0
NEG_INF = -1e30
PEER_HEADS = 8
PEER_HALF = 64
PEER_TOPK = 16

LANES = 128
SUBLANES = 8
VMEM_LIMIT = 56 * 1024 * 1024

TOKEN_BLOCK = 512
GLA_BLOCK = 256
PEER_EXPERT_BLOCK = 1024

C_POOL, C_GQ, C_GK, C_GV, C_GG, C_GZ, C_AQ, C_AK, C_AV, C_AQS, C_AKS, C_END = (
    0, 256, 384, 512, 768, 1024, 1152, 1664, 1792, 1920, 2432, 2560)


def _params(sem):
    return pltpu.CompilerParams(dimension_semantics=sem, vmem_limit_bytes=VMEM_LIMIT)


def _bdot(a, b):
    return jnp.dot(a.astype(BF16), b.astype(BF16), preferred_element_type=F32)


def _fdot(a, b):
    return jnp.dot(a, b, preferred_element_type=F32, precision=HIGHEST)


def _sigmoid(x):
    return 1.0 / (1.0 + jnp.exp(-x))


def _rms(x):
    return x * lax.rsqrt(jnp.mean(x * x, axis=-1, keepdims=True) + EPS)


def _ada_kernel(c_ref, w_ref, b_ref, o_ref):
    c = c_ref[...]
    o_ref[...] = _fdot(c * _sigmoid(c), w_ref[...]) + b_ref[...]


def _ada_mods(cvec, w_ada, b_ada):
    depth, d, six_d = w_ada.shape
    nj = six_d // d
    return pl.pallas_call(
        _ada_kernel,
        grid=(depth, nj),
        in_specs=[pl.BlockSpec((SUBLANES, d), lambda l, j: (0, 0)),
                  pl.BlockSpec((None, d, d), lambda l, j: (l, 0, j)),
                  pl.BlockSpec((None, 1, d), lambda l, j: (l, 0, j))],
        out_specs=pl.BlockSpec((None, SUBLANES, d), lambda l, j: (l, 0, j)),
        out_shape=jax.ShapeDtypeStruct((depth, SUBLANES, six_d), F32),
        compiler_params=_params(("parallel", "parallel")),
        name="ada_mod",
    )(cvec, w_ada, b_ada.reshape(depth, 1, six_d))


def _inproj_kernel(x_ref, mod_ref, n1_ref, w_ref, cos_ref, sin_ref, w2_ref, b2_ref,
                   pp_ref, gq_ref, gk_ref, gv_ref, gg_ref, laf_ref, lab_ref, aq_ref, ak_ref, av_ref):
    h = _rms(x_ref[...]) * n1_ref[...]
    h = (h * (1.0 + mod_ref[1:2, :]) + mod_ref[0:1, :]).astype(BF16)

    def proj(lo, hi):
        return jnp.dot(h, w_ref[:, lo:hi], preferred_element_type=F32)

    pp_ref[...] = proj(C_POOL, C_GQ)
    gq_ref[...] = proj(C_GQ, C_GK) * (GLA_DK ** -0.5)
    gk_ref[...] = proj(C_GK, C_GV)
    gv_ref[...] = proj(C_GV, C_GG)
    gg_ref[...] = proj(C_GG, C_GZ)
    logit = _fdot(proj(C_GZ, C_AQ), w2_ref[...]) + b2_ref[...]
    la = (jnp.minimum(logit, 0.0) - jnp.log1p(jnp.exp(-jnp.abs(logit)))) * (1.0 / GLA_GATE_TAU)
    laf_ref[...] = la[:, :GLA_QK]
    lab_ref[...] = la[:, GLA_QK:]
    cos = cos_ref[...]
    sin = sin_ref[...]
    cos4 = jnp.concatenate([cos] * (ATTN_WIDTH // LANES), axis=1)
    sin4 = jnp.concatenate([sin] * (ATTN_WIDTH // LANES), axis=1)
    aq_ref[...] = (proj(C_AQ, C_AK) * cos4 + proj(C_AQS, C_AKS) * sin4) * (HEAD_DIM ** -0.5)
    ak_ref[...] = proj(C_AK, C_AV) * cos + proj(C_AKS, C_END) * sin
    av_ref[...] = proj(C_AV, C_AQS)


def _inproj(x, modblk, n1, w_big, cos, sin, w2, b2):
    n, d = x.shape
    tm = TOKEN_BLOCK
    widths = (POOL_WIDTH, GLA_QK, GLA_QK, GLA_WIDTH, GLA_WIDTH, GLA_QK, GLA_QK, ATTN_WIDTH, KV_WIDTH, KV_WIDTH)
    row = lambda w: pl.BlockSpec((tm, w), lambda i: (i, 0))
    full = lambda a: pl.BlockSpec(a.shape, lambda i: (0,) * a.ndim)
    return pl.pallas_call(
        _inproj_kernel,
        grid=(n // tm,),
        in_specs=[row(d), pl.BlockSpec((None, 6, d), lambda i: (i, 0, 0)), full(n1), full(w_big),
                  row(LANES), row(LANES), full(w2), full(b2)],
        out_specs=[row(w) for w in widths],
        out_shape=[jax.ShapeDtypeStruct((n, w), F32) for w in widths],
        compiler_params=_params(("parallel",)),
        name="in_proj",
    )(x, modblk, n1, w_big, cos, sin, w2, b2)


def _pool_kernel(x_ref, w_ref, scale_ref, o_ref):
    t = x_ref.shape[0]
    pad = 32
    n = t + pad
    x = x_ref[...]
    xp = jnp.concatenate([x, jnp.zeros((pad, POOL_WIDTH), F32)], axis=0)
    back = {1: xp}
    for w in (2, 4, 8, 16):
        back[w] = back[w // 2] + pltpu.roll(back[w // 2], w // 2, 0)
    grp = lax.broadcasted_iota(jnp.int32, (1, POOL_WIDTH), 1) // POOL_GROUP_DIM
    tok = lax.broadcasted_iota(jnp.int32, (t, 1), 0)
    wsum = None
    cnt = None
    for g, w in enumerate(POOL_WINDOWS):
        left = w // 2
        right = w - 1 - left
        ws = back[w] if right == 0 else pltpu.roll(back[w], n - right, 0)
        ws = ws[:t]
        c = (jnp.minimum(tok + right + 1, t) - jnp.maximum(tok - left, 0)).astype(F32)
        if wsum is None:
            wsum, cnt = ws, jnp.broadcast_to(c, (t, POOL_WIDTH))
        else:
            wsum = jnp.where(grp == g, ws, wsum)
            cnt = jnp.where(grp == g, c, cnt)
    diff = wsum / cnt - x
    o_ref[...] = _bdot(diff, w_ref[...]) * scale_ref[...]


def _pool(pp, w_bd, scale, seq, first_block, nseq):
    return pl.pallas_call(
        _pool_kernel,
        grid=(nseq,),
        in_specs=[pl.BlockSpec((seq, POOL_WIDTH), lambda b: (first_block + b, 0)),
                  pl.BlockSpec(w_bd.shape, lambda b: (0, 0)),
                  pl.BlockSpec(scale.shape, lambda b: (0, 0))],
        out_specs=pl.BlockSpec((seq, POOL_WIDTH), lambda b: (b, 0)),
        out_shape=jax.ShapeDtypeStruct((nseq * seq, POOL_WIDTH), F32),
        compiler_params=_params(("parallel",)),
        name="pool_mix",
    )(pp, w_bd, scale)


def _gla_rows(reverse):
    out = []
    for s in range(GLA_CHUNK):
        g = s // SUBLANES
        out.append((0, SUBLANES * (g + 1)) if reverse else (SUBLANES * g, GLA_CHUNK))
    return out


GLA_PAIR_ROWS = sum(hi - lo for lo, hi in _gla_rows(False))


def _gla_kernel(q_ref, k_ref, v_ref, la_ref, s0_ref, hexp_ref, tri_ref, bd_ref,
                o_ref, sout_ref, st_ref, b_ref, p_ref, z_ref, *, reverse, nchunk):
    j = pl.program_id(1)

    @pl.when(j == 0)
    def _():
        st_ref[...] = s0_ref[...]

    rows = _gla_rows(reverse)
    ngrp = GLA_CHUNK // SUBLANES
    for c in (range(nchunk - 1, -1, -1) if reverse else range(nchunk)):
        r0 = c * GLA_CHUNK
        q = q_ref[r0:r0 + GLA_CHUNK, :]
        k = k_ref[r0:r0 + GLA_CHUNK, :]
        v = v_ref[r0:r0 + GLA_CHUNK, :]
        b = _fdot(tri_ref[...], la_ref[r0:r0 + GLA_CHUNK, :])
        b_ref[...] = b
        blast = b[0:1, :] if reverse else b[GLA_CHUNK - 1:GLA_CHUNK, :]
        st = st_ref[...]
        inter = lax.dot_general((q * jnp.exp(b)).astype(BF16), st.astype(BF16),
                                (((1,), (1,)), ((), ())), preferred_element_type=F32)
        off = 0
        for s, (lo, hi) in enumerate(rows):
            tio = lax.broadcasted_iota(jnp.int32, (hi - lo, 1), 0) + lo
            valid = (tio <= s) if reverse else (tio >= s)
            dlt = jnp.where(valid, b[lo:hi] - b_ref[s:s + 1, :], 0.0)
            p_ref[off:off + hi - lo, :] = jnp.where(valid, q[lo:hi] * k_ref[r0 + s:r0 + s + 1, :] * jnp.exp(dlt), 0.0)
            off += hi - lo
        z_ref[...] = jnp.dot(p_ref[...].astype(BF16), hexp_ref[...], preferred_element_type=F32)
        oacc = [None] * ngrp
        off = 0
        for g in range(ngrp):
            lo, hi = rows[g * SUBLANES]
            cg = None
            for s in range(g * SUBLANES, (g + 1) * SUBLANES):
                term = z_ref[off:off + hi - lo, :] * v_ref[r0 + s:r0 + s + 1, :]
                cg = term if cg is None else cg + term
                off += hi - lo
            for rg in range(lo // SUBLANES, hi // SUBLANES):
                piece = cg[(rg - lo // SUBLANES) * SUBLANES:(rg - lo // SUBLANES + 1) * SUBLANES]
                oacc[rg] = piece if oacc[rg] is None else oacc[rg] + piece
        o_ref[r0:r0 + GLA_CHUNK, :] = jnp.concatenate(oacc, axis=0) + inter
        ke = k * jnp.exp(blast - b)
        kv = jnp.dot(v.T.astype(BF16), ke.astype(BF16), preferred_element_type=F32)
        st_ref[...] = st * jnp.exp(blast) + kv * bd_ref[...]

    @pl.when(j == pl.num_programs(1) - 1)
    def _():
        sout_ref[...] = st_ref[...]


def _gla_consts():
    hd = np.arange(GLA_QK) // GLA_DK
    he = np.arange(GLA_WIDTH) // GLA_DV
    hexp = (hd[:, None] == he[None, :]).astype(np.float32)
    bd = hexp.T.copy()
    t = np.arange(GLA_CHUNK)
    tri_f = (t[None, :] <= t[:, None]).astype(np.float32)
    tri_b = (t[None, :] >= t[:, None]).astype(np.float32)
    return jnp.asarray(hexp, BF16), jnp.asarray(bd, F32), jnp.asarray(tri_f), jnp.asarray(tri_b)


def _gla(gq, gk, gv, la, s0t, consts, seq, first_block, nseq, reverse):
    hexp, bd, tri_f, tri_b = consts
    tri = tri_b if reverse else tri_f
    blk = min(GLA_BLOCK, seq)
    nblk = seq // blk
    if reverse:
        tok = lambda b, j: (first_block + b * nblk + (nblk - 1 - j), 0)
        otok = lambda b, j: (b * nblk + (nblk - 1 - j), 0)
    else:
        tok = lambda b, j: (first_block + b * nblk + j, 0)
        otok = lambda b, j: (b * nblk + j, 0)
    const = lambda a: pl.BlockSpec(a.shape, lambda b, j: (0,) * a.ndim)
    return pl.pallas_call(
        functools.partial(_gla_kernel, reverse=reverse, nchunk=blk // GLA_CHUNK),
        grid=(nseq, nblk),
        in_specs=[pl.BlockSpec((blk, GLA_QK), tok), pl.BlockSpec((blk, GLA_QK), tok),
                  pl.BlockSpec((blk, GLA_WIDTH), tok), pl.BlockSpec((blk, GLA_QK), tok),
                  pl.BlockSpec((None, GLA_WIDTH, GLA_QK), lambda b, j: (b, 0, 0)),
                  const(hexp), const(tri), const(bd)],
        out_specs=[pl.BlockSpec((blk, GLA_WIDTH), otok),
                   pl.BlockSpec((None, GLA_WIDTH, GLA_QK), lambda b, j: (b, 0, 0))],
        out_shape=[jax.ShapeDtypeStruct((nseq * seq, GLA_WIDTH), F32),
                   jax.ShapeDtypeStruct((nseq, GLA_WIDTH, GLA_QK), F32)],
        scratch_shapes=[pltpu.VMEM((GLA_WIDTH, GLA_QK), F32), pltpu.VMEM((GLA_CHUNK, GLA_QK), F32),
                        pltpu.VMEM((GLA_PAIR_ROWS, GLA_QK), F32), pltpu.VMEM((GLA_PAIR_ROWS, GLA_WIDTH), F32)],
        compiler_params=_params(("parallel", "arbitrary")),
        name="gla_bwd" if reverse else "gla_fwd",
    )(gq, gk, gv, la, s0t, hexp, tri, bd)


def _state_to_blockdiag_t(s):
    b = s.shape[0]
    eye = jnp.eye(GLA_HEADS, dtype=s.dtype)
    return jnp.einsum('bhde,hg->bhegd', s, eye).reshape(b, GLA_WIDTH, GLA_QK)


def _blockdiag_t_to_state(st):
    b = st.shape[0]
    eye = jnp.eye(GLA_HEADS, dtype=st.dtype)
    return jnp.einsum('bhegd,hg->bhde', st.reshape(b, GLA_HEADS, GLA_DV, GLA_HEADS, GLA_DK), eye)


def _stack_heads(q_ref, kv):
    return jnp.concatenate([q_ref[:, (kv * ATTN_GROUP + r) * HEAD_DIM:(kv * ATTN_GROUP + r + 1) * HEAD_DIM]
                            for r in range(ATTN_GROUP)], axis=0)


def _sink_column(sink_ref, kv, rows):
    return jnp.concatenate([jnp.broadcast_to(sink_ref[kv * ATTN_GROUP + r:kv * ATTN_GROUP + r + 1, 0:1], (rows, 1))
                            for r in range(ATTN_GROUP)], axis=0)


def _qk(q, k):
    return lax.dot_general(q.astype(BF16), k.astype(BF16), (((1,), (1,)), ((), ())), preferred_element_type=F32)


def _ctx_attn_kernel(q_ref, k_ref, v_ref, sink_ref, o_ref):
    t = q_ref.shape[0]
    for kv in range(ATTN_KV_HEADS):
        k = k_ref[:, kv * HEAD_DIM:(kv + 1) * HEAD_DIM]
        v = v_ref[:, kv * HEAD_DIM:(kv + 1) * HEAD_DIM]
        s = _qk(_stack_heads(q_ref, kv), k)
        sink = _sink_column(sink_ref, kv, t)
        m = jnp.maximum(jnp.max(s, axis=-1, keepdims=True), sink)
        p = jnp.exp(s - m)
        den = jnp.sum(p, axis=-1, keepdims=True) + jnp.exp(sink - m)
        o = _bdot(p, v) / den
        for r in range(ATTN_GROUP):
            h = kv * ATTN_GROUP + r
            o_ref[:, h * HEAD_DIM:(h + 1) * HEAD_DIM] = o[r * t:(r + 1) * t]


def _ctx_attn(aq, ak, av, sink_b, seq, nseq):
    tok = lambda w: pl.BlockSpec((seq, w), lambda b: (b, 0))
    return pl.pallas_call(
        _ctx_attn_kernel,
        grid=(nseq,),
        in_specs=[tok(ATTN_WIDTH), tok(KV_WIDTH), tok(KV_WIDTH), pl.BlockSpec(sink_b.shape, lambda b: (0, 0))],
        out_specs=tok(ATTN_WIDTH),
        out_shape=jax.ShapeDtypeStruct((nseq * seq, ATTN_WIDTH), F32),
        compiler_params=_params(("parallel",)),
        name="ctx_attn",
    )(aq, ak, av, sink_b)


def _lat_attn_kernel(q_ref, k_ref, v_ref, ck_ref, cv_ref, sink_ref, o_ref):
    t = k_ref.shape[0]
    blk = ATTN_BLOCK
    span = 3 * blk
    i = pl.program_id(1)
    start = pl.multiple_of(jnp.clip((i - 1) * blk, 0, t - span), blk)
    kw = k_ref[pl.ds(start, span), :]
    vw = v_ref[pl.ds(start, span), :]
    qpos = i * blk + lax.broadcasted_iota(jnp.int32, (blk, 1), 0)
    kpos = start + lax.broadcasted_iota(jnp.int32, (1, span), 1)
    inwin = jnp.abs(qpos - kpos) <= WINDOW
    inwin = jnp.concatenate([inwin] * ATTN_GROUP, axis=0)
    for kv in range(ATTN_KV_HEADS):
        lanes = slice(kv * HEAD_DIM, (kv + 1) * HEAD_DIM)
        q = _stack_heads(q_ref, kv)
        s_loc = jnp.where(inwin, _qk(q, kw[:, lanes]), NEG_INF)
        s_ctx = _qk(q, ck_ref[:, lanes])
        sink = _sink_column(sink_ref, kv, blk)
        m = jnp.maximum(jnp.maximum(jnp.max(s_loc, axis=-1, keepdims=True),
                                    jnp.max(s_ctx, axis=-1, keepdims=True)), sink)
        p_loc = jnp.exp(s_loc - m)
        p_ctx = jnp.exp(s_ctx - m)
        den = (jnp.sum(p_loc, axis=-1, keepdims=True) + jnp.sum(p_ctx, axis=-1, keepdims=True)
               + jnp.exp(sink - m))
        o = (_bdot(p_loc, vw[:, lanes]) + _bdot(p_ctx, cv_ref[:, lanes])) / den
        for r in range(ATTN_GROUP):
            h = kv * ATTN_GROUP + r
            o_ref[:, h * HEAD_DIM:(h + 1) * HEAD_DIM] = o[r * blk:(r + 1) * blk]


def _lat_attn(aq, ak, av, ck, cv, sink_b, seq, first_tok, nseq):
    nblk = seq // ATTN_BLOCK
    fb_q = first_tok // ATTN_BLOCK
    fb_s = first_tok // seq
    return pl.pallas_call(
        _lat_attn_kernel,
        grid=(nseq, nblk),
        in_specs=[pl.BlockSpec((ATTN_BLOCK, ATTN_WIDTH), lambda b, i: (fb_q + b * nblk + i, 0)),
                  pl.BlockSpec((seq, KV_WIDTH), lambda b, i: (fb_s + b, 0)),
                  pl.BlockSpec((seq, KV_WIDTH), lambda b, i: (fb_s + b, 0)),
                  pl.BlockSpec((None,) + ck.shape[1:], lambda b, i: (b, 0, 0)),
                  pl.BlockSpec((None,) + cv.shape[1:], lambda b, i: (b, 0, 0)),
                  pl.BlockSpec(sink_b.shape, lambda b, i: (0, 0))],
        out_specs=pl.BlockSpec((ATTN_BLOCK, ATTN_WIDTH), lambda b, i: (b * nblk + i, 0)),
        out_shape=jax.ShapeDtypeStruct((nseq * seq, ATTN_WIDTH), F32),
        compiler_params=_params(("parallel", "arbitrary")),
        name="lat_attn",
    )(aq, ak, av, ck, cv, sink_b)


def _outproj_kernel(yp_ref, of_ref, ob_ref, gg_ref, ya_ref, x_ref, mod_ref, gn_ref, hm_ref, w_ref, n2_ref,
                    xo_ref, h2t_ref):
    o = of_ref[...] + ob_ref[...]
    ms = _fdot(o * o, hm_ref[...])
    gg = gg_ref[...]
    y = o * lax.rsqrt(ms + EPS) * gn_ref[...] * (gg * _sigmoid(gg))
    mix = jnp.concatenate([yp_ref[...], y, ya_ref[...]], axis=1)
    xn = x_ref[...] + mod_ref[2:3, :] * _bdot(mix, w_ref[...])
    xo_ref[...] = xn
    h2 = _rms(xn) * n2_ref[...] * (1.0 + mod_ref[4:5, :]) + mod_ref[3:4, :]
    h2t_ref[...] = pltpu.bitcast(h2.T.astype(BF16), jnp.uint32)


def _outproj(ypool, o_f, o_b, gg, yattn, x, modblk, gnorm, hmean, w_out, n2):
    n, d = x.shape
    tm = TOKEN_BLOCK
    row = lambda w: pl.BlockSpec((tm, w), lambda i: (i, 0))
    full = lambda a: pl.BlockSpec(a.shape, lambda i: (0,) * a.ndim)
    return pl.pallas_call(
        _outproj_kernel,
        grid=(n // tm,),
        in_specs=[row(POOL_WIDTH), row(GLA_WIDTH), row(GLA_WIDTH), row(GLA_WIDTH), row(ATTN_WIDTH), row(d),
                  pl.BlockSpec((None, 6, d), lambda i: (i, 0, 0)), full(gnorm), full(hmean), full(w_out), full(n2)],
        out_specs=[row(d), pl.BlockSpec((d // 2, tm), lambda i: (0, i))],
        out_shape=[jax.ShapeDtypeStruct((n, d), F32), jax.ShapeDtypeStruct((d // 2, n), jnp.uint32)],
        compiler_params=_params(("parallel",)),
        name="out_proj",
    )(ypool, o_f, o_b, gg, yattn, x, modblk, gnorm, hmean, w_out, n2)


def _peer_score_kernel(h2t_ref, wqt_ref, sk_ref, s1_ref, s2_ref):
    qt = jnp.dot(wqt_ref[...], pltpu.bitcast(h2t_ref[...], BF16), preferred_element_type=F32)
    for h in range(PEER_HEADS):
        for p, out in enumerate((s1_ref, s2_ref)):
            r = (2 * h + p) * PEER_HALF
            out[h] = jnp.dot(sk_ref[2 * h + p], qt[r:r + PEER_HALF].astype(BF16), preferred_element_type=F32)


def _peer_scores(h2t, wqt, sk):
    d, n = h2t.shape
    tb = TOKEN_BLOCK
    nkeys = sk.shape[1]
    out = pl.BlockSpec((PEER_HEADS, nkeys, tb), lambda i: (0, 0, i))
    return pl.pallas_call(
        _peer_score_kernel,
        grid=(n // tb,),
        in_specs=[pl.BlockSpec((d, tb), lambda i: (0, i)),
                  pl.BlockSpec(wqt.shape, lambda i: (0, 0)),
                  pl.BlockSpec(sk.shape, lambda i: (0, 0, 0))],
        out_specs=[out, out],
        out_shape=[jax.ShapeDtypeStruct((PEER_HEADS, nkeys, n), F32)] * 2,
        compiler_params=_params(("parallel",)),
        name="peer_scores",
    )(h2t, wqt, sk)


PEER_CANDS = [(a, b) for a in range(PEER_TOPK) for b in range(PEER_TOPK) if (a + 1) * (b + 1) <= PEER_TOPK]
PEER_CAND_ROWS = -(-len(PEER_CANDS) // SUBLANES) * SUBLANES


def _top_values(cur, count):
    vals = []
    for _ in range(count):
        m = jnp.max(cur, axis=0, keepdims=True)
        vals.append(m)
        cur = jnp.where(cur == m, -jnp.inf, cur)
    return vals


def _peer_gate_kernel(s1_ref, s2_ref, cnt_ref, a_ref, rk_ref, bx_ref, cand_ref):
    for h in range(PEER_HEADS):
        s1 = s1_ref[h]
        s2 = s2_ref[h]
        v1 = _top_values(s1, PEER_TOPK)
        v2 = _top_values(s2, PEER_TOPK)
        cand_ref[...] = jnp.full(cand_ref.shape, -jnp.inf, F32)
        for r, (a, b) in enumerate(PEER_CANDS):
            cand_ref[r:r + 1, :] = v1[a] + v2[b]
        best = _top_values(cand_ref[...], PEER_TOPK)
        z = None
        for val in best:
            e = jnp.exp(val - best[0])
            z = e if z is None else z + e
        tau = best[-1]
        rk = None
        cnt = None
        for b in range(PEER_TOPK):
            above = jnp.where(v2[b] > s2, 1.0, 0.0)
            reach = jnp.where(s1 + v2[b] >= tau, 1.0, 0.0)
            rk = above if rk is None else rk + above
            cnt = reach if cnt is None else cnt + reach
        cnt_ref[h] = cnt
        a_ref[h] = jnp.exp(s1 - v1[0]) * (0.5 / z)
        rk_ref[h] = pltpu.bitcast(rk.astype(BF16), jnp.uint32)
        bx_ref[h] = pltpu.bitcast(jnp.exp(s2 - v2[0]).astype(BF16), jnp.uint32)


def _peer_gates(s1, s2):
    _, nkeys, n = s1.shape
    tb = TOKEN_BLOCK
    blk = pl.BlockSpec((PEER_HEADS, nkeys, tb), lambda i: (0, 0, i))
    half = pl.BlockSpec((PEER_HEADS, nkeys // 2, tb), lambda i: (0, 0, i))
    return pl.pallas_call(
        _peer_gate_kernel,
        grid=(n // tb,),
        in_specs=[blk, blk],
        out_specs=[blk, blk, half, half],
        out_shape=[jax.ShapeDtypeStruct((PEER_HEADS, nkeys, n), F32),
                   jax.ShapeDtypeStruct((PEER_HEADS, nkeys, n), F32),
                   jax.ShapeDtypeStruct((PEER_HEADS, nkeys // 2, n), jnp.uint32),
                   jax.ShapeDtypeStruct((PEER_HEADS, nkeys // 2, n), jnp.uint32)],
        scratch_shapes=[pltpu.VMEM((PEER_CAND_ROWS, tb), F32)],
        compiler_params=_params(("parallel",)),
        name="peer_gates",
    )(s1, s2)


PEER_ROW_GROUP = 4
PEER_TILE_ROWS = 64


def _peer_dense_kernel(h2t_ref, u_ref, vt_ref, cnt_ref, a_ref, rk_ref, bx_ref, x_ref, mod_ref,
                       xo_ref, yt_ref, at0_ref, at1_ref, w_ref, *, nkeys):
    e = pl.program_id(1)
    per_step = 2 * u_ref.shape[0] // nkeys
    group_rows = PEER_ROW_GROUP * nkeys

    @pl.when(e == 0)
    def _():
        yt_ref[...] = jnp.zeros(yt_ref.shape, F32)
        at1_ref[...] = jnp.zeros(at1_ref.shape, F32)

    def step(at_new, at_old):
        w_new = w_ref
        live = jnp.where(e >= 1, 1.0, 0.0)
        first_key = pl.multiple_of(jnp.maximum(e - 1, 0) * per_step, per_step)
        zero = jnp.zeros((PEER_TILE_ROWS, LANES), BF16)

        def gate_tiles(i0, tt, jh):
            cols = slice(tt * LANES, (tt + 1) * LANES)
            words = slice(jh * PEER_TILE_ROWS // 2, (jh + 1) * PEER_TILE_ROWS // 2)
            g = [None] * PEER_ROW_GROUP
            for h in range(PEER_HEADS):
                rk = pltpu.bitcast(rk_ref[h, words, cols], BF16)
                bx = pltpu.bitcast(bx_ref[h, words, cols], BF16)
                cnts = cnt_ref[h, pl.ds(first_key, per_step), cols]
                arows = a_ref[h, pl.ds(first_key, per_step), cols] * live
                for ii in range(PEER_ROW_GROUP):
                    cnt = cnts[i0 + ii:i0 + ii + 1, :].astype(BF16)
                    term = jnp.where(rk < cnt, bx, zero) * arows[i0 + ii:i0 + ii + 1, :].astype(BF16)
                    g[ii] = term if g[ii] is None else g[ii] + term
            for ii in range(PEER_ROW_GROUP):
                r0 = (i0 + ii) * nkeys + jh * PEER_TILE_ROWS
                rows = slice(r0, r0 + PEER_TILE_ROWS)
                xb = at_old[rows, cols].astype(BF16)
                w_new[rows, cols] = g[ii] * (xb * (1.0 + lax.erf(xb * (2.0 ** -0.5))))

        at_new[...] = jnp.dot(pltpu.bitcast(u_ref[...], BF16), pltpu.bitcast(h2t_ref[...], BF16),
                              preferred_element_type=F32)
        for i0 in range(0, per_step, PEER_ROW_GROUP):
            for tt in range(at_old.shape[1] // LANES):
                for jh in range(nkeys // PEER_TILE_ROWS):
                    gate_tiles(i0, tt, jh)
            blk = slice(i0 * nkeys, i0 * nkeys + group_rows)
            yt_ref[...] += jnp.dot(pltpu.bitcast(vt_ref[...], BF16)[:, blk], w_new[blk, :],
                                   preferred_element_type=F32)

    @pl.when(e % 2 == 0)
    def _():
        step(at0_ref, at1_ref)

    @pl.when(e % 2 == 1)
    def _():
        step(at1_ref, at0_ref)

    @pl.when(e == pl.num_programs(1) - 1)
    def _():
        xo_ref[...] = x_ref[...] + mod_ref[5:6, :] * yt_ref[...].T


def _pack_row_pairs(x):
    bits = lax.bitcast_convert_type(x, jnp.uint16).astype(jnp.uint32)
    return bits[0::2] | (bits[1::2] << 16)


def _peer_dense(h2t, u, vt, cnt, a, rk, bx, x, modblk):
    n, d = x.shape
    nexp = 2 * u.shape[0]
    nkeys = cnt.shape[1]
    tb = TOKEN_BLOCK
    ne, _, eb = vt.shape
    assert nkeys == LANES and eb % (SUBLANES * nkeys) == 0 and SUBLANES % PEER_ROW_GROUP == 0 and ne * eb == nexp
    keyed = pl.BlockSpec((PEER_HEADS, nkeys, tb), lambda t, e: (0, 0, t))
    packed = pl.BlockSpec((PEER_HEADS, nkeys // 2, tb), lambda t, e: (0, 0, t))
    return pl.pallas_call(
        functools.partial(_peer_dense_kernel, nkeys=nkeys),
        grid=(n // tb, ne + 1),
        in_specs=[pl.BlockSpec((d // 2, tb), lambda t, e: (0, t)),
                  pl.BlockSpec((eb // 2, d), lambda t, e: (jnp.minimum(e, ne - 1), 0)),
                  pl.BlockSpec((None, d // 2, eb), lambda t, e: (jnp.maximum(e - 1, 0), 0, 0)),
                  keyed, keyed, packed, packed,
                  pl.BlockSpec((tb, d), lambda t, e: (t, 0)),
                  pl.BlockSpec((None, 6, d), lambda t, e: (t, 0, 0))],
        out_specs=pl.BlockSpec((tb, d), lambda t, e: (t, 0)),
        out_shape=jax.ShapeDtypeStruct((n, d), F32),
        scratch_shapes=[pltpu.VMEM((d, tb), F32), pltpu.VMEM((eb, tb), F32), pltpu.VMEM((eb, tb), F32),
                        pltpu.VMEM((eb, tb), BF16)],
        compiler_params=_params(("parallel", "arbitrary")),
        name="peer_dense",
    )(h2t, u, vt, cnt, a, rk, bx, x, modblk)


def _final_norm_kernel(x_ref, g_ref, o_ref):
    o_ref[...] = _rms(x_ref[...]) * g_ref[...]


def _final_norm(x, g):
    n, d = x.shape
    tm = TOKEN_BLOCK
    return pl.pallas_call(
        _final_norm_kernel,
        grid=(n // tm,),
        in_specs=[pl.BlockSpec((tm, d), lambda i: (i, 0)), pl.BlockSpec((1, d), lambda i: (0, 0))],
        out_specs=pl.BlockSpec((tm, d), lambda i: (i, 0)),
        out_shape=jax.ShapeDtypeStruct((n, d), F32),
        compiler_params=_params(("parallel",)),
        name="final_norm",
    )(x, g.reshape(1, d))


def _rope_swap_columns(width):
    half = HEAD_DIM // 2
    nf = half // 2
    perm = np.zeros(width, np.int32)
    sign = np.zeros(width, np.float32)
    for c in range(width):
        r = c % half
        if r < nf:
            perm[c], sign[c] = c + nf, -1.0
        else:
            perm[c], sign[c] = c - nf, 1.0
    return perm, sign


def _pack_w_in(w_in):
    d = w_in.shape[0]
    o_aq = POOL_WIDTH + 2 * GLA_QK + 2 * GLA_WIDTH + 2 * GLA_GATE_RANK
    o_ak = o_aq + ATTN_WIDTH
    o_av = o_ak + KV_WIDTH
    w_aq = w_in[:, o_aq:o_ak]
    w_ak = w_in[:, o_ak:o_av]
    pq, sq = _rope_swap_columns(ATTN_WIDTH)
    pk, sk = _rope_swap_columns(KV_WIDTH)
    gz = jnp.pad(w_in[:, o_aq - 2 * GLA_GATE_RANK:o_aq], ((0, 0), (0, LANES - 2 * GLA_GATE_RANK)))
    cols = [w_in[:, :o_aq - 2 * GLA_GATE_RANK], gz, w_aq, w_ak, w_in[:, o_av:],
            w_aq[:, pq] * sq[None, :], w_ak[:, pk] * sk[None, :]]
    return jnp.concatenate(cols, axis=1).astype(BF16)


def _block_diag(w):
    g, c, _ = w.shape
    eye = jnp.eye(g, dtype=w.dtype)
    return jnp.einsum('gcd,gh->gchd', w, eye).reshape(g * c, g * c)


def _rope_tables(n_ctx, n_lat_seq, n_lat_batch):
    half = HEAD_DIM // 2
    nf = half // 2
    freqs = ROPE_THETA ** (-jnp.arange(nf, dtype=F32) / nf)
    rows = n_lat_seq // GRID_W
    pos_row = jnp.repeat(jnp.arange(rows), GRID_W).astype(F32)
    pos_col = jnp.tile(jnp.arange(GRID_W), rows).astype(F32)
    ar = pos_row[:, None] * freqs[None, :]
    ac = pos_col[:, None] * freqs[None, :]
    cos = jnp.concatenate([jnp.cos(ar)] * 2 + [jnp.cos(ac)] * 2, axis=1)
    sin = jnp.concatenate([jnp.sin(ar)] * 2 + [jnp.sin(ac)] * 2, axis=1)
    reps = LANES // HEAD_DIM
    cos = jnp.tile(jnp.tile(cos, (1, reps)), (n_lat_batch, 1))
    sin = jnp.tile(jnp.tile(sin, (1, reps)), (n_lat_batch, 1))
    cos = jnp.concatenate([jnp.ones((n_ctx, LANES), F32), cos], axis=0)
    sin = jnp.concatenate([jnp.zeros((n_ctx, LANES), F32), sin], axis=0)
    return cos, sin


def kernel(x_prompt, x_sample, cache_k, cache_v, state_fwd, state_bwd, c, c_ctx, w_ada, b_ada, norm1_g, norm2_g,
           w_in, pool_w, pool_scale, gla_gate_w_f, gla_gate_b_f, gla_gate_w_b, gla_gate_b_b, gla_norm_g,
           attn_sink, w_out, peer_wq, peer_subkeys, peer_u, peer_v, final_norm_g):
    nb, seq, d = x_prompt.shape
    nlb, lseq, _ = x_sample.shape
    depth = w_ada.shape[0]
    n_ctx, n_lat = nb * seq, nlb * lseq
    n = n_ctx + n_lat
    tm = TOKEN_BLOCK
    nkeys = peer_subkeys.shape[3]
    assert d == D_MODEL and n_ctx % tm == 0 and lseq % tm == 0 and n_ctx % lseq == 0
    assert seq % GLA_CHUNK == 0 and lseq % GLA_BLOCK == 0 and lseq % GRID_W == 0 and lseq >= 3 * ATTN_BLOCK
    assert nlb + 1 <= SUBLANES and nkeys % SUBLANES == 0

    x0 = jnp.concatenate([x_prompt.reshape(n_ctx, d), x_sample.reshape(n_lat, d)], axis=0)
    cvec = jnp.zeros((SUBLANES, d), F32).at[0].set(c_ctx).at[1:1 + nlb].set(c)
    mods = _ada_mods(cvec, w_ada, b_ada)
    blk_row = np.concatenate([np.zeros(n_ctx // tm, np.int32),
                              1 + np.repeat(np.arange(nlb, dtype=np.int32), lseq // tm)])
    modblk = mods[:, blk_row, :].reshape(depth, n // tm, 6, d)
    cos, sin = _rope_tables(n_ctx, lseq, nlb)
    gla_consts = _gla_consts()
    hd = np.arange(GLA_WIDTH) // GLA_DV
    hmean = jnp.asarray((hd[:, None] == hd[None, :]).astype(np.float32) / GLA_DV)
    zero_state = jnp.zeros((nb, GLA_WIDTH, GLA_QK), F32)

    def layer(x, lp):
        (mod_l, n1, n2, w_in_l, pool_w_l, pool_scale_l, gw_f, gb_f, gw_b, gb_b, gnorm, sink, w_out_l,
         wq, subk, pu, pv, ck, cv, sf, sb) = lp
        w_big = _pack_w_in(w_in_l)
        w2 = jnp.zeros((LANES, 2 * GLA_QK), F32)
        w2 = w2.at[:GLA_GATE_RANK, :GLA_QK].set(gw_f).at[GLA_GATE_RANK:2 * GLA_GATE_RANK, GLA_QK:].set(gw_b)
        b2 = jnp.concatenate([gb_f, gb_b]).reshape(1, 2 * GLA_QK)
        pp, gq, gk, gv, gg, la_f, la_b, aq, ak, av = _inproj(x, mod_l, n1.reshape(1, d), w_big, cos, sin, w2, b2)

        w_bd = _block_diag(pool_w_l).astype(BF16)
        scale = pool_scale_l.reshape(1, POOL_WIDTH)
        y_pool = jnp.concatenate([_pool(pp, w_bd, scale, seq, 0, nb),
                                  _pool(pp, w_bd, scale, lseq, n_ctx // lseq, nlb)], axis=0)

        lblk = min(GLA_BLOCK, lseq)
        of_c, sf_c = _gla(gq, gk, gv, la_f, zero_state, gla_consts, seq, 0, nb, False)
        ob_c, sb_c = _gla(gq, gk, gv, la_b, zero_state, gla_consts, seq, 0, nb, True)
        of_l, _ = _gla(gq, gk, gv, la_f, _state_to_blockdiag_t(sf), gla_consts, lseq, n_ctx // lblk, nlb, False)
        ob_l, _ = _gla(gq, gk, gv, la_b, _state_to_blockdiag_t(sb), gla_consts, lseq, n_ctx // lblk, nlb, True)
        o_f = jnp.concatenate([of_c, of_l], axis=0)
        o_b = jnp.concatenate([ob_c, ob_l], axis=0)

        sink_b = jnp.broadcast_to(sink.reshape(ATTN_HEADS, 1), (ATTN_HEADS, LANES))
        y_attn = jnp.concatenate([_ctx_attn(aq, ak, av, sink_b, seq, nb),
                                  _lat_attn(aq, ak, av, ck, cv, sink_b, lseq, n_ctx, nlb)], axis=0)

        x1, h2t = _outproj(y_pool, o_f, o_b, gg, y_attn, x, mod_l, gnorm.reshape(1, GLA_WIDTH), hmean,
                           w_out_l.astype(BF16), n2.reshape(1, d))

        wqt = wq.T.astype(BF16)
        sk = subk.reshape(2 * PEER_HEADS, nkeys, PEER_HALF).astype(BF16)
        s1, s2 = _peer_scores(h2t, wqt, sk)
        cnt, a, rk, bx = _peer_gates(s1, s2)
        nexp = pu.shape[0]
        eb = min(PEER_EXPERT_BLOCK, nexp)
        vt = _pack_row_pairs(pv.T.astype(BF16)).reshape(d // 2, nexp // eb, eb).swapaxes(0, 1)
        x2 = _peer_dense(h2t, _pack_row_pairs(pu.astype(BF16)), vt, cnt, a, rk, bx, x1, mod_l)
        return x2, (ak[:n_ctx], av[:n_ctx], _blockdiag_t_to_state(sf_c), _blockdiag_t_to_state(sb_c))

    past = cache_k.shape[2]
    xs = (modblk, norm1_g, norm2_g, w_in, pool_w, pool_scale, gla_gate_w_f, gla_gate_b_f, gla_gate_w_b,
          gla_gate_b_b, gla_norm_g, attn_sink, w_out, peer_wq, peer_subkeys, peer_u, peer_v,
          jnp.swapaxes(cache_k, 0, 1).reshape(depth, nlb, past, KV_WIDTH),
          jnp.swapaxes(cache_v, 0, 1).reshape(depth, nlb, past, KV_WIDTH),
          jnp.swapaxes(state_fwd, 0, 1), jnp.swapaxes(state_bwd, 0, 1))
    x_fin, (ks, vs, sfs, sbs) = lax.scan(layer, x0, xs)

    y = _final_norm(x_fin, final_norm_g)
    y_prompt = y[:n_ctx].reshape(nb, seq, d)
    y_sample = y[n_ctx:].reshape(nlb, lseq, d)
    new_k = jnp.swapaxes(ks.reshape(depth, nb, seq, ATTN_KV_HEADS, HEAD_DIM), 0, 1)
    new_v = jnp.swapaxes(vs.reshape(depth, nb, seq, ATTN_KV_HEADS, HEAD_DIM), 0, 1)
    return (y_prompt, y_sample, new_k, new_v, jnp.swapaxes(sfs, 0, 1), jnp.swapaxes(sbs, 0, 1))
```

```python
import functools
import math

import numpy as np
import jax
import jax.numpy as jnp
from jax import lax
from jax.experimental import pallas as pl
from jax.experimental.pallas import tpu as pltpu

F32 = jnp.float32
BF16 = jnp.bfloat16
HIGHEST = lax.Precision.HIGHEST

D_MODEL = 1024
GRID_W = 64
EPS = 1e-6
POOL_WIDTH = 256
POOL_GROUPS = 4
POOL_GROUP_DIM = 64
POOL_WINDOWS = (2, 4, 8, 16)
GLA_HEADS = 4
GLA_DV = 64
GLA_DK = 32
GLA_QK = GLA_HEADS * GLA_DK
GLA_WIDTH = GLA_HEADS * GLA_DV
GLA_GATE_RANK = 16
GLA_GATE_TAU = 16.0
GLA_CHUNK = 64
ATTN_HEADS = 8
ATTN_KV_HEADS = 2
ATTN_GROUP = 4
HEAD_DIM = 64
ATTN_WIDTH = ATTN_HEADS * HEAD_DIM
KV_WIDTH = ATTN_KV_HEADS * HEAD_DIM
WINDOW = 128
ATTN_BLOCK = 128
ROPE_THETA = 10000.0
NEG_INF = -1e30
PEER_HEADS = 8
PEER_HALF = 64
PEER_TOPK = 16

LANES = 128
SUBLANES = 8
VMEM_LIMIT = 56 * 1024 * 1024

TOKEN_BLOCK = 512
GLA_BLOCK = 256
PEER_EXPERT_BLOCK = 1024

C_POOL, C_GQ, C_GK, C_GV, C_GG, C_GZ, C_AQ, C_AK, C_AV, C_AQS, C_AKS, C_END = (
    0, 256, 384, 512, 768, 1024, 1152, 1664, 1792, 1920, 2432, 2560)


def _params(sem):
    return pltpu.CompilerParams(dimension_semantics=sem, vmem_limit_bytes=VMEM_LIMIT)


def _bdot(a, b):
    return jnp.dot(a.astype(BF16), b.astype(BF16), preferred_element_type=F32)


def _fdot(a, b):
    return jnp.dot(a, b, preferred_element_type=F32, precision=HIGHEST)


def _sigmoid(x):
    return 1.0 / (1.0 + jnp.exp(-x))


def _rms(x):
    return x * lax.rsqrt(jnp.mean(x * x, axis=-1, keepdims=True) + EPS)


def _ada_kernel(c_ref, w_ref, b_ref, o_ref):
    c = c_ref[...]
    o_ref[...] = _fdot(c * _sigmoid(c), w_ref[...]) + b_ref[...]


def _ada_mods(cvec, w_ada, b_ada):
    depth, d, six_d = w_ada.shape
    nj = six_d // d
    return pl.pallas_call(
        _ada_kernel,
        grid=(depth, nj),
        in_specs=[pl.BlockSpec((SUBLANES, d), lambda l, j: (0, 0)),
                  pl.BlockSpec((None, d, d), lambda l, j: (l, 0, j)),
                  pl.BlockSpec((None, 1, d), lambda l, j: (l, 0, j))],
        out_specs=pl.BlockSpec((None, SUBLANES, d), lambda l, j: (l, 0, j)),
        out_shape=jax.ShapeDtypeStruct((depth, SUBLANES, six_d), F32),
        compiler_params=_params(("parallel", "parallel")),
        name="ada_mod",
    )(cvec, w_ada, b_ada.reshape(depth, 1, six_d))


def _inproj_kernel(x_ref, mod_ref, n1_ref, w_ref, cos_ref, sin_ref, w2_ref, b2_ref,
                   pp_ref, gq_ref, gk_ref, gv_ref, gg_ref, laf_ref, lab_ref, aq_ref, ak_ref, av_ref):
    h = _rms(x_ref[...]) * n1_ref[...]
    h = (h * (1.0 + mod_ref[1:2, :]) + mod_ref[0:1, :]).astype(BF16)

    def proj(lo, hi):
        return jnp.dot(h, w_ref[:, lo:hi], preferred_element_type=F32)

    pp_ref[...] = proj(C_POOL, C_GQ)
    gq_ref[...] = proj(C_GQ, C_GK) * (GLA_DK ** -0.5)
    gk_ref[...] = proj(C_GK, C_GV)
    gv_ref[...] = proj(C_GV, C_GG)
    gg_ref[...] = proj(C_GG, C_GZ)
    logit = _fdot(proj(C_GZ, C_AQ), w2_ref[...]) + b2_ref[...]
    la = (jnp.minimum(logit, 0.0) - jnp.log1p(jnp.exp(-jnp.abs(logit)))) * (1.0 / GLA_GATE_TAU)
    laf_ref[...] = la[:, :GLA_QK]
    lab_ref[...] = la[:, GLA_QK:]
    cos = cos_ref[...]
    sin = sin_ref[...]
    cos4 = jnp.concatenate([cos] * (ATTN_WIDTH // LANES), axis=1)
    sin4 = jnp.concatenate([sin] * (ATTN_WIDTH // LANES), axis=1)
    aq_ref[...] = (proj(C_AQ, C_AK) * cos4 + proj(C_AQS, C_AKS) * sin4) * (HEAD_DIM ** -0.5)
    ak_ref[...] = proj(C_AK, C_AV) * cos + proj(C_AKS, C_END) * sin
    av_ref[...] = proj(C_AV, C_AQS)


def _inproj(x, modblk, n1, w_big, cos, sin, w2, b2):
    n, d = x.shape
    tm = TOKEN_BLOCK
    widths = (POOL_WIDTH, GLA_QK, GLA_QK, GLA_WIDTH, GLA_WIDTH, GLA_QK, GLA_QK, ATTN_WIDTH, KV_WIDTH, KV_WIDTH)
    row = lambda w: pl.BlockSpec((tm, w), lambda i: (i, 0))
    full = lambda a: pl.BlockSpec(a.shape, lambda i: (0,) * a.ndim)
    return pl.pallas_call(
        _inproj_kernel,
        grid=(n // tm,),
        in_specs=[row(d), pl.BlockSpec((None, 6, d), lambda i: (i, 0, 0)), full(n1), full(w_big),
                  row(LANES), row(LANES), full(w2), full(b2)],
        out_specs=[row(w) for w in widths],
        out_shape=[jax.ShapeDtypeStruct((n, w), F32) for w in widths],
        compiler_params=_params(("parallel",)),
        name="in_proj",
    )(x, modblk, n1, w_big, cos, sin, w2, b2)


def _pool_kernel(x_ref, w_ref, scale_ref, o_ref):
    t = x_ref.shape[0]
    pad = 32
    n = t + pad
    x = x_ref[...]
    xp = jnp.concatenate([x, jnp.zeros((pad, POOL_WIDTH), F32)], axis=0)
    back = {1: xp}
    for w in (2, 4, 8, 16):
        back[w] = back[w // 2] + pltpu.roll(back[w // 2], w // 2, 0)
    grp = lax.broadcasted_iota(jnp.int32, (1, POOL_WIDTH), 1) // POOL_GROUP_DIM
    tok = lax.broadcasted_iota(jnp.int32, (t, 1), 0)
    wsum = None
    cnt = None
    for g, w in enumerate(POOL_WINDOWS):
        left = w // 2
        right = w - 1 - left
        ws = back[w] if right == 0 else pltpu.roll(back[w], n - right, 0)
        ws = ws[:t]
        c = (jnp.minimum(tok + right + 1, t) - jnp.maximum(tok - left, 0)).astype(F32)
        if wsum is None:
            wsum, cnt = ws, jnp.broadcast_to(c, (t, POOL_WIDTH))
        else:
            wsum = jnp.where(grp == g, ws, wsum)
            cnt = jnp.where(grp == g, c, cnt)
    diff = wsum / cnt - x
    o_ref[...] = _bdot(diff, w_ref[...]) * scale_ref[...]


def _pool(pp, w_bd, scale, seq, first_block, nseq):
    return pl.pallas_call(
        _pool_kernel,
        grid=(nseq,),
        in_specs=[pl.BlockSpec((seq, POOL_WIDTH), lambda b: (first_block + b, 0)),
                  pl.BlockSpec(w_bd.shape, lambda b: (0, 0)),
                  pl.BlockSpec(scale.shape, lambda b: (0, 0))],
        out_specs=pl.BlockSpec((seq, POOL_WIDTH), lambda b: (b, 0)),
        out_shape=jax.ShapeDtypeStruct((nseq * seq, POOL_WIDTH), F32),
        compiler_params=_params(("parallel",)),
        name="pool_mix",
    )(pp, w_bd, scale)


def _gla_rows(reverse):
    out = []
    for s in range(GLA_CHUNK):
        g = s // SUBLANES
        out.append((0, SUBLANES * (g + 1)) if reverse else (SUBLANES * g, GLA_CHUNK))
    return out


GLA_PAIR_ROWS = sum(hi - lo for lo, hi in _gla_rows(False))


def _gla_kernel(q_ref, k_ref, v_ref, la_ref, s0_ref, hexp_ref, tri_ref, bd_ref,
                o_ref, sout_ref, st_ref, b_ref, p_ref, z_ref, *, reverse, nchunk):
    j = pl.program_id(1)

    @pl.when(j == 0)
    def _():
        st_ref[...] = s0_ref[...]

    rows = _gla_rows(reverse)
    ngrp = GLA_CHUNK // SUBLANES
    for c in (range(nchunk - 1, -1, -1) if reverse else range(nchunk)):
        r0 = c * GLA_CHUNK
        q = q_ref[r0:r0 + GLA_CHUNK, :]
        k = k_ref[r0:r0 + GLA_CHUNK, :]
        v = v_ref[r0:r0 + GLA_CHUNK, :]
        b = _fdot(tri_ref[...], la_ref[r0:r0 + GLA_CHUNK, :])
        b_ref[...] = b
        blast = b[0:1, :] if reverse else b[GLA_CHUNK - 1:GLA_CHUNK, :]
        st = st_ref[...]
        inter = lax.dot_general((q * jnp.exp(b)).astype(BF16), st.astype(BF16),
                                (((1,), (1,)), ((), ())), preferred_element_type=F32)
        off = 0
        tio = lax.broadcasted_iota(jnp.int32, (SUBLANES, 1), 0)
        for s, (lo, hi) in enumerate(rows):
            dlt = jnp.minimum(b[lo:hi] - b_ref[s:s + 1, :], 0.0)
            p = q[lo:hi] * k_ref[r0 + s:r0 + s + 1, :] * jnp.exp(dlt)
            p_ref[off:off + hi - lo, :] = p
            d0 = hi - lo - SUBLANES if reverse else 0
            valid = (tio <= s % SUBLANES) if reverse else (tio >= s % SUBLANES)
            p_ref[off + d0:off + d0 + SUBLANES, :] = jnp.where(valid, p[d0:d0 + SUBLANES], 0.0)
            off += hi - lo
        z_ref[...] = jnp.dot(p_ref[...].astype(BF16), hexp_ref[...], preferred_element_type=F32)
        oacc = [None] * ngrp
        off = 0
        for g in range(ngrp):
            lo, hi = rows[g * SUBLANES]
            cg = None
            for s in range(g * SUBLANES, (g + 1) * SUBLANES):
                term = z_ref[off:off + hi - lo, :] * v_ref[r0 + s:r0 + s + 1, :]
                cg = term if cg is None else cg + term
                off += hi - lo
            for rg in range(lo // SUBLANES, hi // SUBLANES):
                piece = cg[(rg - lo // SUBLANES) * SUBLANES:(rg - lo // SUBLANES + 1) * SUBLANES]
                oacc[rg] = piece if oacc[rg] is None else oacc[rg] + piece
        o_ref[r0:r0 + GLA_CHUNK, :] = jnp.concatenate(oacc, axis=0) + inter
        ke = k * jnp.exp(blast - b)
        kv = jnp.dot(v.T.astype(BF16), ke.astype(BF16), preferred_element_type=F32)
        st_ref[...] = st * jnp.exp(blast) + kv * bd_ref[...]

    @pl.when(j == pl.num_programs(1) - 1)
    def _():
        sout_ref[...] = st_ref[...]


def _gla_consts():
    hd = np.arange(GLA_QK) // GLA_DK
    he = np.arange(GLA_WIDTH) // GLA_DV
    hexp = (hd[:, None] == he[None, :]).astype(np.float32)
    bd = hexp.T.copy()
    t = np.arange(GLA_CHUNK)
    tri_f = (t[None, :] <= t[:, None]).astype(np.float32)
    tri_b = (t[None, :] >= t[:, None]).astype(np.float32)
    return jnp.asarray(hexp, BF16), jnp.asarray(bd, F32), jnp.asarray(tri_f), jnp.asarray(tri_b)


def _gla(gq, gk, gv, la, s0t, consts, seq, first_block, nseq, reverse):
    hexp, bd, tri_f, tri_b = consts
    tri = tri_b if reverse else tri_f
    blk = min(GLA_BLOCK, seq)
    nblk = seq // blk
    if reverse:
        tok = lambda b, j: (first_block + b * nblk + (nblk - 1 - j), 0)
        otok = lambda b, j: (b * nblk + (nblk - 1 - j), 0)
    else:
        tok = lambda b, j: (first_block + b * nblk + j, 0)
        otok = lambda b, j: (b * nblk + j, 0)
    const = lambda a: pl.BlockSpec(a.shape, lambda b, j: (0,) * a.ndim)
    return pl.pallas_call(
        functools.partial(_gla_kernel, reverse=reverse, nchunk=blk // GLA_CHUNK),
        grid=(nseq, nblk),
        in_specs=[pl.BlockSpec((blk, GLA_QK), tok), pl.BlockSpec((blk, GLA_QK), tok),
                  pl.BlockSpec((blk, GLA_WIDTH), tok), pl.BlockSpec((blk, GLA_QK), tok),
                  pl.BlockSpec((None, GLA_WIDTH, GLA_QK), lambda b, j: (b, 0, 0)),
                  const(hexp), const(tri), const(bd)],
        out_specs=[pl.BlockSpec((blk, GLA_WIDTH), otok),
                   pl.BlockSpec((None, GLA_WIDTH, GLA_QK), lambda b, j: (b, 0, 0))],
        out_shape=[jax.ShapeDtypeStruct((nseq * seq, GLA_WIDTH), F32),
                   jax.ShapeDtypeStruct((nseq, GLA_WIDTH, GLA_QK), F32)],
        scratch_shapes=[pltpu.VMEM((GLA_WIDTH, GLA_QK), F32), pltpu.VMEM((GLA_CHUNK, GLA_QK), F32),
                        pltpu.VMEM((GLA_PAIR_ROWS, GLA_QK), F32), pltpu.VMEM((GLA_PAIR_ROWS, GLA_WIDTH), F32)],
        compiler_params=_params(("parallel", "arbitrary")),
        name="gla_bwd" if reverse else "gla_fwd",
    )(gq, gk, gv, la, s0t, hexp, tri, bd)


def _state_to_blockdiag_t(s):
    b = s.shape[0]
    eye = jnp.eye(GLA_HEADS, dtype=s.dtype)
    return jnp.einsum('bhde,hg->bhegd', s, eye).reshape(b, GLA_WIDTH, GLA_QK)


def _blockdiag_t_to_state(st):
    b = st.shape[0]
    eye = jnp.eye(GLA_HEADS, dtype=st.dtype)
    return jnp.einsum('bhegd,hg->bhde', st.reshape(b, GLA_HEADS, GLA_DV, GLA_HEADS, GLA_DK), eye)


def _stack_heads(q_ref, kv):
    return jnp.concatenate([q_ref[:, (kv * ATTN_GROUP + r) * HEAD_DIM:(kv * ATTN_GROUP + r + 1) * HEAD_DIM]
                            for r in range(ATTN_GROUP)], axis=0)


def _sink_column(sink_ref, kv, rows):
    return jnp.concatenate([jnp.broadcast_to(sink_ref[kv * ATTN_GROUP + r:kv * ATTN_GROUP + r + 1, 0:1], (rows, 1))
                            for r in range(ATTN_GROUP)], axis=0)


def _qk(q, k):
    return lax.dot_general(q.astype(BF16), k.astype(BF16), (((1,), (1,)), ((), ())), preferred_element_type=F32)


def _ctx_attn_kernel(q_ref, k_ref, v_ref, sink_ref, o_ref):
    t = q_ref.shape[0]
    for kv in range(ATTN_KV_HEADS):
        k = k_ref[:, kv * HEAD_DIM:(kv + 1) * HEAD_DIM]
        v = v_ref[:, kv * HEAD_DIM:(kv + 1) * HEAD_DIM]
        s = _qk(_stack_heads(q_ref, kv), k)
        sink = _sink_column(sink_ref, kv, t)
        m = jnp.maximum(jnp.max(s, axis=-1, keepdims=True), sink)
        p = jnp.exp(s - m)
        den = jnp.sum(p, axis=-1, keepdims=True) + jnp.exp(sink - m)
        o = _bdot(p, v) / den
        for r in range(ATTN_GROUP):
            h = kv * ATTN_GROUP + r
            o_ref[:, h * HEAD_DIM:(h + 1) * HEAD_DIM] = o[r * t:(r + 1) * t]


def _ctx_attn(aq, ak, av, sink_b, seq, nseq):
    tok = lambda w: pl.BlockSpec((seq, w), lambda b: (b, 0))
    return pl.pallas_call(
        _ctx_attn_kernel,
        grid=(nseq,),
        in_specs=[tok(ATTN_WIDTH), tok(KV_WIDTH), tok(KV_WIDTH), pl.BlockSpec(sink_b.shape, lambda b: (0, 0))],
        out_specs=tok(ATTN_WIDTH),
        out_shape=jax.ShapeDtypeStruct((nseq * seq, ATTN_WIDTH), F32),
        compiler_params=_params(("parallel",)),
        name="ctx_attn",
    )(aq, ak, av, sink_b)


def _lat_attn_kernel(q_ref, k_ref, v_ref, ck_ref, cv_ref, sink_ref, o_ref):
    t = k_ref.shape[0]
    blk = ATTN_BLOCK
    span = 3 * blk
    i = pl.program_id(1)
    start = pl.multiple_of(jnp.clip((i - 1) * blk, 0, t - span), blk)
    kw = k_ref[pl.ds(start, span), :]
    vw = v_ref[pl.ds(start, span), :]
    qpos = i * blk + lax.broadcasted_iota(jnp.int32, (blk, 1), 0)
    kpos = start + lax.broadcasted_iota(jnp.int32, (1, span), 1)
    inwin = jnp.abs(qpos - kpos) <= WINDOW
    inwin = jnp.concatenate([inwin] * ATTN_GROUP, axis=0)
    for kv in range(ATTN_KV_HEADS):
        lanes = slice(kv * HEAD_DIM, (kv + 1) * HEAD_DIM)
        q = _stack_heads(q_ref, kv)
        s_loc = jnp.where(inwin, _qk(q, kw[:, lanes]), NEG_INF)
        s_ctx = _qk(q, ck_ref[:, lanes])
        sink = _sink_column(sink_ref, kv, blk)
        m = jnp.maximum(jnp.maximum(jnp.max(s_loc, axis=-1, keepdims=True),
                                    jnp.max(s_ctx, axis=-1, keepdims=True)), sink)
        p_loc = jnp.exp(s_loc - m)
        p_ctx = jnp.exp(s_ctx - m)
        den = (jnp.sum(p_loc, axis=-1, keepdims=True) + jnp.sum(p_ctx, axis=-1, keepdims=True)
               + jnp.exp(sink - m))
        o = (_bdot(p_loc, vw[:, lanes]) + _bdot(p_ctx, cv_ref[:, lanes])) / den
        for r in range(ATTN_GROUP):
            h = kv * ATTN_GROUP + r
            o_ref[:, h * HEAD_DIM:(h + 1) * HEAD_DIM] = o[r * blk:(r + 1) * blk]


def _lat_attn(aq, ak, av, ck, cv, sink_b, seq, first_tok, nseq):
    nblk = seq // ATTN_BLOCK
    fb_q = first_tok // ATTN_BLOCK
    fb_s = first_tok // seq
    return pl.pallas_call(
        _lat_attn_kernel,
        grid=(nseq, nblk),
        in_specs=[pl.BlockSpec((ATTN_BLOCK, ATTN_WIDTH), lambda b, i: (fb_q + b * nblk + i, 0)),
                  pl.BlockSpec((seq, KV_WIDTH), lambda b, i: (fb_s + b, 0)),
                  pl.BlockSpec((seq, KV_WIDTH), lambda b, i: (fb_s + b, 0)),
                  pl.BlockSpec((None,) + ck.shape[1:], lambda b, i: (b, 0, 0)),
                  pl.BlockSpec((None,) + cv.shape[1:], lambda b, i: (b, 0, 0)),
                  pl.BlockSpec(sink_b.shape, lambda b, i: (0, 0))],
        out_specs=pl.BlockSpec((ATTN_BLOCK, ATTN_WIDTH), lambda b, i: (b * nblk + i, 0)),
        out_shape=jax.ShapeDtypeStruct((nseq * seq, ATTN_WIDTH), F32),
        compiler_params=_params(("parallel", "arbitrary")),
        name="lat_attn",
    )(aq, ak, av, ck, cv, sink_b)


def _outproj_kernel(yp_ref, of_ref, ob_ref, gg_ref, ya_ref, x_ref, mod_ref, gn_ref, hm_ref, w_ref, n2_ref,
                    xo_ref, h2t_ref):
    o = of_ref[...] + ob_ref[...]
    ms = _fdot(o * o, hm_ref[...])
    gg = gg_ref[...]
    y = o * lax.rsqrt(ms + EPS) * gn_ref[...] * (gg * _sigmoid(gg))
    mix = jnp.concatenate([yp_ref[...], y, ya_ref[...]], axis=1)
    xn = x_ref[...] + mod_ref[2:3, :] * _bdot(mix, w_ref[...])
    xo_ref[...] = xn
    h2 = _rms(xn) * n2_ref[...] * (1.0 + mod_ref[4:5, :]) + mod_ref[3:4, :]
    h2t_ref[...] = pltpu.bitcast(h2.T.astype(BF16), jnp.uint32)


def _outproj(ypool, o_f, o_b, gg, yattn, x, modblk, gnorm, hmean, w_out, n2):
    n, d = x.shape
    tm = TOKEN_BLOCK
    row = lambda w: pl.BlockSpec((tm, w), lambda i: (i, 0))
    full = lambda a: pl.BlockSpec(a.shape, lambda i: (0,) * a.ndim)
    return pl.pallas_call(
        _outproj_kernel,
        grid=(n // tm,),
        in_specs=[row(POOL_WIDTH), row(GLA_WIDTH), row(GLA_WIDTH), row(GLA_WIDTH), row(ATTN_WIDTH), row(d),
                  pl.BlockSpec((None, 6, d), lambda i: (i, 0, 0)), full(gnorm), full(hmean), full(w_out), full(n2)],
        out_specs=[row(d), pl.BlockSpec((d // 2, tm), lambda i: (0, i))],
        out_shape=[jax.ShapeDtypeStruct((n, d), F32), jax.ShapeDtypeStruct((d // 2, n), jnp.uint32)],
        compiler_params=_params(("parallel",)),
        name="out_proj",
    )(ypool, o_f, o_b, gg, yattn, x, modblk, gnorm, hmean, w_out, n2)


def _peer_score_kernel(h2t_ref, wqt_ref, sk_ref, s1_ref, s2_ref):
    qt = jnp.dot(wqt_ref[...], pltpu.bitcast(h2t_ref[...], BF16), preferred_element_type=F32)
    for h in range(PEER_HEADS):
        for p, out in enumerate((s1_ref, s2_ref)):
            r = (2 * h + p) * PEER_HALF
            out[h] = jnp.dot(sk_ref[2 * h + p], qt[r:r + PEER_HALF].astype(BF16), preferred_element_type=F32)


def _peer_scores(h2t, wqt, sk):
    d, n = h2t.shape
    tb = TOKEN_BLOCK
    nkeys = sk.shape[1]
    out = pl.BlockSpec((PEER_HEADS, nkeys, tb), lambda i: (0, 0, i))
    return pl.pallas_call(
        _peer_score_kernel,
        grid=(n // tb,),
        in_specs=[pl.BlockSpec((d, tb), lambda i: (0, i)),
                  pl.BlockSpec(wqt.shape, lambda i: (0, 0)),
                  pl.BlockSpec(sk.shape, lambda i: (0, 0, 0))],
        out_specs=[out, out],
        out_shape=[jax.ShapeDtypeStruct((PEER_HEADS, nkeys, n), F32)] * 2,
        compiler_params=_params(("parallel",)),
        name="peer_scores",
    )(h2t, wqt, sk)


PEER_CANDS = [(a, b) for a in range(PEER_TOPK) for b in range(PEER_TOPK) if (a + 1) * (b + 1) <= PEER_TOPK]
PEER_CAND_ROWS = -(-len(PEER_CANDS) // SUBLANES) * SUBLANES


def _top_values(cur, count, with_rank=False):
    vals = []
    rank = jnp.full(cur.shape, float(count), F32) if with_rank else None
    for r in range(count):
        m = jnp.max(cur, axis=0, keepdims=True)
        vals.append(m)
        hit = cur == m
        if with_rank:
            rank = jnp.where(hit, float(r), rank)
        cur = jnp.where(hit, -jnp.inf, cur)
    return (vals, rank) if with_rank else vals


PEER_DENSE_RANKS = 4
PEER_GATE_TILE = 512
assert all(b < PEER_DENSE_RANKS for a, b in PEER_CANDS if a >= PEER_DENSE_RANKS)


def _peer_gate_kernel(s1_ref, s2_ref, cnt_ref, a_ref, rk_ref, bx_ref, cand_ref):
    width = cand_ref.shape[1]
    ntile = s1_ref.shape[2] // width

    def one_tile(it, carry):
        h = it // ntile
        cols = pl.ds(pl.multiple_of((it % ntile) * width, width), width)
        s1 = s1_ref[h, :, cols]
        s2 = s2_ref[h, :, cols]
        v1 = _top_values(s1, PEER_TOPK)
        v2, rk = _top_values(s2, PEER_TOPK, with_rank=True)
        cand_ref[...] = jnp.full(cand_ref.shape, -jnp.inf, F32)
        for r, (a, b) in enumerate(PEER_CANDS):
            cand_ref[r:r + 1, :] = v1[a] + v2[b]
        best = _top_values(cand_ref[...], PEER_TOPK)
        z = None
        for val in best:
            e = jnp.exp(val - best[0])
            z = e if z is None else z + e
        tau = best[-1]
        cnt = None
        for b in range(PEER_DENSE_RANKS):
            reach = jnp.where(s1 + v2[b] >= tau, 1.0, 0.0)
            cnt = reach if cnt is None else cnt + reach
        for a in range(PEER_DENSE_RANKS):
            full = None
            for b in range(PEER_TOPK // (a + 1)):
                reach = jnp.where(v1[a] + v2[b] >= tau, 1.0, 0.0)
                full = reach if full is None else full + reach
            cnt = jnp.where(s1 == v1[a], full, cnt)
        cnt_ref[h, :, cols] = cnt
        a_ref[h, :, cols] = jnp.exp(s1 - v1[0]) * (0.5 / z)
        rk_ref[h, :, cols] = pltpu.bitcast(rk.astype(BF16), jnp.uint32)
        bx_ref[h, :, cols] = pltpu.bitcast(jnp.exp(s2 - v2[0]).astype(BF16), jnp.uint32)
        return carry

    lax.fori_loop(0, PEER_HEADS * ntile, one_tile, 0)


def _peer_gates(s1, s2):
    _, nkeys, n = s1.shape
    tb = TOKEN_BLOCK
    blk = pl.BlockSpec((PEER_HEADS, nkeys, tb), lambda i: (0, 0, i))
    half = pl.BlockSpec((PEER_HEADS, nkeys // 2, tb), lambda i: (0, 0, i))
    return pl.pallas_call(
        _peer_gate_kernel,
        grid=(n // tb,),
        in_specs=[blk, blk],
        out_specs=[blk, blk, half, half],
        out_shape=[jax.ShapeDtypeStruct((PEER_HEADS, nkeys, n), F32),
                   jax.ShapeDtypeStruct((PEER_HEADS, nkeys, n), F32),
                   jax.ShapeDtypeStruct((PEER_HEADS, nkeys // 2, n), jnp.uint32),
                   jax.ShapeDtypeStruct((PEER_HEADS, nkeys // 2, n), jnp.uint32)],
        scratch_shapes=[pltpu.VMEM((PEER_CAND_ROWS, PEER_GATE_TILE), F32)],
        compiler_params=_params(("parallel",)),
        name="peer_gates",
    )(s1, s2)


PEER_ROW_GROUP = 4
PEER_TILE_ROWS = 64


def _peer_dense_kernel(h2t_ref, u_ref, vt_ref, cnt_ref, a_ref, rk_ref, bx_ref, x_ref, mod_ref,
                       xo_ref, yt_ref, at0_ref, at1_ref, w_ref, *, nkeys):
    e = pl.program_id(1)
    per_step = u_ref.shape[0] // nkeys
    group_rows = PEER_ROW_GROUP * nkeys

    @pl.when(e == 0)
    def _():
        yt_ref[...] = jnp.zeros(yt_ref.shape, F32)
        at1_ref[...] = jnp.zeros(at1_ref.shape, F32)

    def step(at_new, at_old):
        w_new = w_ref
        live = jnp.where(e >= 1, 1.0, 0.0)
        first_key = pl.multiple_of(jnp.maximum(e - 1, 0) * per_step, per_step)
        zero = jnp.zeros((PEER_TILE_ROWS, LANES), BF16)

        def gate_tiles(i0, tt, jh):
            cols = slice(tt * LANES, (tt + 1) * LANES)
            words = slice(jh * PEER_TILE_ROWS // 2, (jh + 1) * PEER_TILE_ROWS // 2)
            g = [None] * PEER_ROW_GROUP
            for h in range(PEER_HEADS):
                rk = pltpu.bitcast(rk_ref[h, words, cols], BF16)
                bx = pltpu.bitcast(bx_ref[h, words, cols], BF16)
                cnts = cnt_ref[h, pl.ds(first_key, per_step), cols]
                arows = a_ref[h, pl.ds(first_key, per_step), cols] * live
                for ii in range(PEER_ROW_GROUP):
                    cnt = cnts[i0 + ii:i0 + ii + 1, :].astype(BF16)
                    term = jnp.where(rk < cnt, bx, zero) * arows[i0 + ii:i0 + ii + 1, :].astype(BF16)
                    g[ii] = term if g[ii] is None else g[ii] + term
            for ii in range(PEER_ROW_GROUP):
                r0 = (i0 + ii) * nkeys + jh * PEER_TILE_ROWS
                rows = slice(r0, r0 + PEER_TILE_ROWS)
                xb = at_old[rows, cols].astype(BF16)
                w_new[rows, cols] = g[ii] * (xb * (1.0 + lax.erf(xb * (2.0 ** -0.5))))

        at_new[...] = jnp.dot(u_ref[...], pltpu.bitcast(h2t_ref[...], BF16), preferred_element_type=F32)
        for i0 in range(0, per_step, PEER_ROW_GROUP):
            for tt in range(at_old.shape[1] // LANES):
                for jh in range(nkeys // PEER_TILE_ROWS):
                    gate_tiles(i0, tt, jh)
            blk = slice(i0 * nkeys, i0 * nkeys + group_rows)
            yt_ref[...] += jnp.dot(vt_ref[:, blk], w_new[blk, :], preferred_element_type=F32)

    @pl.when(e % 2 == 0)
    def _():
        step(at0_ref, at1_ref)

    @pl.when(e % 2 == 1)
    def _():
        step(at1_ref, at0_ref)

    @pl.when(e == pl.num_programs(1) - 1)
    def _():
        xo_ref[...] = x_ref[...] + mod_ref[5:6, :] * yt_ref[...].T


def _peer_dense(h2t, u, vt, cnt, a, rk, bx, x, modblk):
    n, d = x.shape
    nexp = u.shape[0]
    nkeys = cnt.shape[1]
    tb = TOKEN_BLOCK
    eb = min(PEER_EXPERT_BLOCK, nexp)
    ne = nexp // eb
    assert nkeys == LANES and eb % (SUBLANES * nkeys) == 0 and SUBLANES % PEER_ROW_GROUP == 0 and ne * eb == nexp
    keyed = pl.BlockSpec((PEER_HEADS, nkeys, tb), lambda t, e: (0, 0, t))
    packed = pl.BlockSpec((PEER_HEADS, nkeys // 2, tb), lambda t, e: (0, 0, t))
    return pl.pallas_call(
        functools.partial(_peer_dense_kernel, nkeys=nkeys),
        grid=(n // tb, ne + 1),
        in_specs=[pl.BlockSpec((d // 2, tb), lambda t, e: (0, t)),
                  pl.BlockSpec((eb, d), lambda t, e: (jnp.minimum(e, ne - 1), 0)),
                  pl.BlockSpec((d, eb), lambda t, e: (0, jnp.maximum(e - 1, 0))),
                  keyed, keyed, packed, packed,
                  pl.BlockSpec((tb, d), lambda t, e: (t, 0)),
                  pl.BlockSpec((None, 6, d), lambda t, e: (t, 0, 0))],
        out_specs=pl.BlockSpec((tb, d), lambda t, e: (t, 0)),
        out_shape=jax.ShapeDtypeStruct((n, d), F32),
        scratch_shapes=[pltpu.VMEM((d, tb), F32), pltpu.VMEM((eb, tb), F32), pltpu.VMEM((eb, tb), F32),
                        pltpu.VMEM((eb, tb), BF16)],
        compiler_params=_params(("parallel", "arbitrary")),
        name="peer_dense",
    )(h2t, u, vt, cnt, a, rk, bx, x, modblk)


def _final_norm_kernel(x_ref, g_ref, o_ref):
    o_ref[...] = _rms(x_ref[...]) * g_ref[...]


def _final_norm(x, g):
    n, d = x.shape
    tm = TOKEN_BLOCK
    return pl.pallas_call(
        _final_norm_kernel,
        grid=(n // tm,),
        in_specs=[pl.BlockSpec((tm, d), lambda i: (i, 0)), pl.BlockSpec((1, d), lambda i: (0, 0))],
        out_specs=pl.BlockSpec((tm, d), lambda i: (i, 0)),
        out_shape=jax.ShapeDtypeStruct((n, d), F32),
        compiler_params=_params(("parallel",)),
        name="final_norm",
    )(x, g.reshape(1, d))


def _rope_swap_columns(width):
    half = HEAD_DIM // 2
    nf = half // 2
    perm = np.zeros(width, np.int32)
    sign = np.zeros(width, np.float32)
    for c in range(width):
        r = c % half
        if r < nf:
            perm[c], sign[c] = c + nf, -1.0
        else:
            perm[c], sign[c] = c - nf, 1.0
    return perm, sign


def _pack_w_in(w_in):
    d = w_in.shape[0]
    o_aq = POOL_WIDTH + 2 * GLA_QK + 2 * GLA_WIDTH + 2 * GLA_GATE_RANK
    o_ak = o_aq + ATTN_WIDTH
    o_av = o_ak + KV_WIDTH
    w_aq = w_in[:, o_aq:o_ak]
    w_ak = w_in[:, o_ak:o_av]
    pq, sq = _rope_swap_columns(ATTN_WIDTH)
    pk, sk = _rope_swap_columns(KV_WIDTH)
    gz = jnp.pad(w_in[:, o_aq - 2 * GLA_GATE_RANK:o_aq], ((0, 0), (0, LANES - 2 * GLA_GATE_RANK)))
    cols = [w_in[:, :o_aq - 2 * GLA_GATE_RANK], gz, w_aq, w_ak, w_in[:, o_av:],
            w_aq[:, pq] * sq[None, :], w_ak[:, pk] * sk[None, :]]
    return jnp.concatenate(cols, axis=1).astype(BF16)


def _block_diag(w):
    g, c, _ = w.shape
    eye = jnp.eye(g, dtype=w.dtype)
    return jnp.einsum('gcd,gh->gchd', w, eye).reshape(g * c, g * c)


def _rope_tables(n_ctx, n_lat_seq, n_lat_batch):
    half = HEAD_DIM // 2
    nf = half // 2
    freqs = ROPE_THETA ** (-jnp.arange(nf, dtype=F32) / nf)
    rows = n_lat_seq // GRID_W
    pos_row = jnp.repeat(jnp.arange(rows), GRID_W).astype(F32)
    pos_col = jnp.tile(jnp.arange(GRID_W), rows).astype(F32)
    ar = pos_row[:, None] * freqs[None, :]
    ac = pos_col[:, None] * freqs[None, :]
    cos = jnp.concatenate([jnp.cos(ar)] * 2 + [jnp.cos(ac)] * 2, axis=1)
    sin = jnp.concatenate([jnp.sin(ar)] * 2 + [jnp.sin(ac)] * 2, axis=1)
    reps = LANES // HEAD_DIM
    cos = jnp.tile(jnp.tile(cos, (1, reps)), (n_lat_batch, 1))
    sin = jnp.tile(jnp.tile(sin, (1, reps)), (n_lat_batch, 1))
    cos = jnp.concatenate([jnp.ones((n_ctx, LANES), F32), cos], axis=0)
    sin = jnp.concatenate([jnp.zeros((n_ctx, LANES), F32), sin], axis=0)
    return cos, sin


def kernel(x_prompt, x_sample, cache_k, cache_v, state_fwd, state_bwd, c, c_ctx, w_ada, b_ada, norm1_g, norm2_g,
           w_in, pool_w, pool_scale, gla_gate_w_f, gla_gate_b_f, gla_gate_w_b, gla_gate_b_b, gla_norm_g,
           attn_sink, w_out, peer_wq, peer_subkeys, peer_u, peer_v, final_norm_g):
    nb, seq, d = x_prompt.shape
    nlb, lseq, _ = x_sample.shape
    depth = w_ada.shape[0]
    n_ctx, n_lat = nb * seq, nlb * lseq
    n = n_ctx + n_lat
    tm = TOKEN_BLOCK
    nkeys = peer_subkeys.shape[3]
    assert d == D_MODEL and n_ctx % tm == 0 and lseq % tm == 0 and n_ctx % lseq == 0
    assert seq % GLA_CHUNK == 0 and lseq % GLA_BLOCK == 0 and lseq % GRID_W == 0 and lseq >= 3 * ATTN_BLOCK
    assert nlb + 1 <= SUBLANES and nkeys % SUBLANES == 0

    x0 = jnp.concatenate([x_prompt.reshape(n_ctx, d), x_sample.reshape(n_lat, d)], axis=0)
    cvec = jnp.zeros((SUBLANES, d), F32).at[0].set(c_ctx).at[1:1 + nlb].set(c)
    mods = _ada_mods(cvec, w_ada, b_ada)
    blk_row = np.concatenate([np.zeros(n_ctx // tm, np.int32),
                              1 + np.repeat(np.arange(nlb, dtype=np.int32), lseq // tm)])
    modblk = mods[:, blk_row, :].reshape(depth, n // tm, 6, d)
    cos, sin = _rope_tables(n_ctx, lseq, nlb)
    gla_consts = _gla_consts()
    hd = np.arange(GLA_WIDTH) // GLA_DV
    hmean = jnp.asarray((hd[:, None] == hd[None, :]).astype(np.float32) / GLA_DV)
    zero_state = jnp.zeros((nb, GLA_WIDTH, GLA_QK), F32)

    def layer(x, lp):
        (mod_l, n1, n2, w_in_l, pool_w_l, pool_scale_l, gw_f, gb_f, gw_b, gb_b, gnorm, sink, w_out_l,
         wq, subk, pu, pv, ck, cv, sf, sb) = lp
        w_big = _pack_w_in(w_in_l)
        w2 = jnp.zeros((LANES, 2 * GLA_QK), F32)
        w2 = w2.at[:GLA_GATE_RANK, :GLA_QK].set(gw_f).at[GLA_GATE_RANK:2 * GLA_GATE_RANK, GLA_QK:].set(gw_b)
        b2 = jnp.concatenate([gb_f, gb_b]).reshape(1, 2 * GLA_QK)
        pp, gq, gk, gv, gg, la_f, la_b, aq, ak, av = _inproj(x, mod_l, n1.reshape(1, d), w_big, cos, sin, w2, b2)

        w_bd = _block_diag(pool_w_l).astype(BF16)
        scale = pool_scale_l.reshape(1, POOL_WIDTH)
        y_pool = jnp.concatenate([_pool(pp, w_bd, scale, seq, 0, nb),
                                  _pool(pp, w_bd, scale, lseq, n_ctx // lseq, nlb)], axis=0)

        lblk = min(GLA_BLOCK, lseq)
        of_c, sf_c = _gla(gq, gk, gv, la_f, zero_state, gla_consts, seq, 0, nb, False)
        ob_c, sb_c = _gla(gq, gk, gv, la_b, zero_state, gla_consts, seq, 0, nb, True)
        of_l, _ = _gla(gq, gk, gv, la_f, _state_to_blockdiag_t(sf), gla_consts, lseq, n_ctx // lblk, nlb, False)
        ob_l, _ = _gla(gq, gk, gv, la_b, _state_to_blockdiag_t(sb), gla_consts, lseq, n_ctx // lblk, nlb, True)
        o_f = jnp.concatenate([of_c, of_l], axis=0)
        o_b = jnp.concatenate([ob_c, ob_l], axis=0)

        sink_b = jnp.broadcast_to(sink.reshape(ATTN_HEADS, 1), (ATTN_HEADS, LANES))
        y_attn = jnp.concatenate([_ctx_attn(aq, ak, av, sink_b, seq, nb),
                                  _lat_attn(aq, ak, av, ck, cv, sink_b, lseq, n_ctx, nlb)], axis=0)

        x1, h2t = _outproj(y_pool, o_f, o_b, gg, y_attn, x, mod_l, gnorm.reshape(1, GLA_WIDTH), hmean,
                           w_out_l.astype(BF16), n2.reshape(1, d))

        wqt = wq.T.astype(BF16)
        sk = subk.reshape(2 * PEER_HEADS, nkeys, PEER_HALF).astype(BF16)
        s1, s2 = _peer_scores(h2t, wqt, sk)
        cnt, a, rk, bx = _peer_gates(s1, s2)
        x2 = _peer_dense(h2t, pu.astype(BF16), pv.T.astype(BF16), cnt, a, rk, bx, x1, mod_l)
        return x2, (ak[:n_ctx], av[:n_ctx], _blockdiag_t_to_state(sf_c), _blockdiag_t_to_state(sb_c))

    past = cache_k.shape[2]
    xs = (modblk, norm1_g, norm2_g, w_in, pool_w, pool_scale, gla_gate_w_f, gla_gate_b_f, gla_gate_w_b,
          gla_gate_b_b, gla_norm_g, attn_sink, w_out, peer_wq, peer_subkeys, peer_u, peer_v,
          jnp.swapaxes(cache_k, 0, 1).reshape(depth, nlb, past, KV_WIDTH),
          jnp.swapaxes(cache_v, 0, 1).reshape(depth, nlb, past, KV_WIDTH),
          jnp.swapaxes(state_fwd, 0, 1), jnp.swapaxes(state_bwd, 0, 1))
    x_fin, (ks, vs, sfs, sbs) = lax.scan(layer, x0, xs)

    y = _final_norm(x_fin, final_norm_g)
    y_prompt = y[:n_ctx].reshape(nb, seq, d)
    y_sample = y[n_ctx:].reshape(nlb, lseq, d)
    new_k = jnp.swapaxes(ks.reshape(depth, nb, seq, ATTN_KV_HEADS, HEAD_DIM), 0, 1)
    new_v = jnp.swapaxes(vs.reshape(depth, nb, seq, ATTN_KV_HEADS, HEAD_DIM), 0, 1)
    return (y_prompt, y_sample, new_k, new_v, jnp.swapaxes(sfs, 0, 1), jnp.swapaxes(sbs, 0, 1))
```

```python
import functools
import math

import numpy as np
import jax
import jax.numpy as jnp
from jax import lax
from jax.experimental import pallas as pl
from jax.experimental.pallas import tpu as pltpu

F32 = jnp.float32
BF16 = jnp.bfloat16
HIGHEST = lax.Precision.HIGHEST

D_MODEL = 1024
GRID_W = 64
EPS = 1e-6
POOL_WIDTH = 256
POOL_GROUPS = 4
POOL_GROUP_DIM = 64
POOL_WINDOWS = (2, 4, 8, 16)
GLA_HEADS = 4
GLA_DV = 64
GLA_DK = 32
GLA_QK = GLA_HEADS * GLA_DK
GLA_WIDTH = GLA_HEADS * GLA_DV
GLA_GATE_RANK = 16
GLA_GATE_TAU = 16.0
GLA_CHUNK = 64
ATTN_HEADS = 8
ATTN_KV_HEADS = 2
ATTN_GROUP = 4
HEAD_DIM = 64
ATTN_WIDTH = ATTN_HEADS * HEAD_DIM
KV_WIDTH = ATTN_KV_HEADS * HEAD_DIM
WINDOW = 128
ATTN_BLOCK = 128
ROPE_THETA = 10000.0
NEG_INF = -1e30
PEER_HEADS = 8
PEER_HALF = 64
PEER_TOPK = 16

LANES = 128
SUBLANES = 8
VMEM_LIMIT = 56 * 1024 * 1024

TOKEN_BLOCK = 512
GLA_BLOCK = 256
PEER_EXPERT_BLOCK = 2048

C_POOL, C_GQ, C_GK, C_GV, C_GG, C_GZ, C_AQ, C_AK, C_AV, C_AQS, C_AKS, C_END = (
    0, 256, 384, 512, 768, 1024, 1152, 1664, 1792, 1920, 2432, 2560)


def _params(sem):
    return pltpu.CompilerParams(dimension_semantics=sem, vmem_limit_bytes=VMEM_LIMIT)


def _bdot(a, b):
    return jnp.dot(a.astype(BF16), b.astype(BF16), preferred_element_type=F32)


def _fdot(a, b):
    return jnp.dot(a, b, preferred_element_type=F32, precision=HIGHEST)


def _sigmoid(x):
    return 1.0 / (1.0 + jnp.exp(-x))


def _rms(x):
    return x * lax.rsqrt(jnp.mean(x * x, axis=-1, keepdims=True) + EPS)


def _ada_kernel(c_ref, w_ref, b_ref, o_ref):
    c = c_ref[...]
    o_ref[...] = _fdot(c * _sigmoid(c), w_ref[...]) + b_ref[...]


def _ada_mods(cvec, w_ada, b_ada):
    depth, d, six_d = w_ada.shape
    nj = six_d // d
    return pl.pallas_call(
        _ada_kernel,
        grid=(depth, nj),
        in_specs=[pl.BlockSpec((SUBLANES, d), lambda l, j: (0, 0)),
                  pl.BlockSpec((None, d, d), lambda l, j: (l, 0, j)),
                  pl.BlockSpec((None, 1, d), lambda l, j: (l, 0, j))],
        out_specs=pl.BlockSpec((None, SUBLANES, d), lambda l, j: (l, 0, j)),
        out_shape=jax.ShapeDtypeStruct((depth, SUBLANES, six_d), F32),
        compiler_params=_params(("parallel", "parallel")),
        name="ada_mod",
    )(cvec, w_ada, b_ada.reshape(depth, 1, six_d))


def _inproj_kernel(x_ref, mod_ref, n1_ref, w_ref, cos_ref, sin_ref, w2_ref, b2_ref,
                   pp_ref, gq_ref, gk_ref, gv_ref, gg_ref, laf_ref, lab_ref, aq_ref, ak_ref, av_ref):
    h = _rms(x_ref[...]) * n1_ref[...]
    h = (h * (1.0 + mod_ref[1:2, :]) + mod_ref[0:1, :]).astype(BF16)

    def proj(lo, hi):
        return jnp.dot(h, w_ref[:, lo:hi], preferred_element_type=F32)

    pp_ref[...] = proj(C_POOL, C_GQ)
    gq_ref[...] = proj(C_GQ, C_GK) * (GLA_DK ** -0.5)
    gk_ref[...] = proj(C_GK, C_GV)
    gv_ref[...] = proj(C_GV, C_GG)
    gg_ref[...] = proj(C_GG, C_GZ)
    logit = _fdot(proj(C_GZ, C_AQ), w2_ref[...]) + b2_ref[...]
    la = (jnp.minimum(logit, 0.0) - jnp.log1p(jnp.exp(-jnp.abs(logit)))) * (1.0 / GLA_GATE_TAU)
    laf_ref[...] = la[:, :GLA_QK]
    lab_ref[...] = la[:, GLA_QK:]
    cos = cos_ref[...]
    sin = sin_ref[...]
    cos4 = jnp.concatenate([cos] * (ATTN_WIDTH // LANES), axis=1)
    sin4 = jnp.concatenate([sin] * (ATTN_WIDTH // LANES), axis=1)
    aq_ref[...] = (proj(C_AQ, C_AK) * cos4 + proj(C_AQS, C_AKS) * sin4) * (HEAD_DIM ** -0.5)
    ak_ref[...] = proj(C_AK, C_AV) * cos + proj(C_AKS, C_END) * sin
    av_ref[...] = proj(C_AV, C_AQS)


def _inproj(x, modblk, n1, w_big, cos, sin, w2, b2):
    n, d = x.shape
    tm = TOKEN_BLOCK
    widths = (POOL_WIDTH, GLA_QK, GLA_QK, GLA_WIDTH, GLA_WIDTH, GLA_QK, GLA_QK, ATTN_WIDTH, KV_WIDTH, KV_WIDTH)
    row = lambda w: pl.BlockSpec((tm, w), lambda i: (i, 0))
    full = lambda a: pl.BlockSpec(a.shape, lambda i: (0,) * a.ndim)
    return pl.pallas_call(
        _inproj_kernel,
        grid=(n // tm,),
        in_specs=[row(d), pl.BlockSpec((None, 6, d), lambda i: (i, 0, 0)), full(n1), full(w_big),
                  row(LANES), row(LANES), full(w2), full(b2)],
        out_specs=[row(w) for w in widths],
        out_shape=[jax.ShapeDtypeStruct((n, w), F32) for w in widths],
        compiler_params=_params(("parallel",)),
        name="in_proj",
    )(x, modblk, n1, w_big, cos, sin, w2, b2)


def _pool_kernel(x_ref, w_ref, scale_ref, o_ref):
    t = x_ref.shape[0]
    pad = 32
    n = t + pad
    x = x_ref[...]
    xp = jnp.concatenate([x, jnp.zeros((pad, POOL_WIDTH), F32)], axis=0)
    back = {1: xp}
    for w in (2, 4, 8, 16):
        back[w] = back[w // 2] + pltpu.roll(back[w // 2], w // 2, 0)
    grp = lax.broadcasted_iota(jnp.int32, (1, POOL_WIDTH), 1) // POOL_GROUP_DIM
    tok = lax.broadcasted_iota(jnp.int32, (t, 1), 0)
    wsum = None
    cnt = None
    for g, w in enumerate(POOL_WINDOWS):
        left = w // 2
        right = w - 1 - left
        ws = back[w] if right == 0 else pltpu.roll(back[w], n - right, 0)
        ws = ws[:t]
        c = (jnp.minimum(tok + right + 1, t) - jnp.maximum(tok - left, 0)).astype(F32)
        if wsum is None:
            wsum, cnt = ws, jnp.broadcast_to(c, (t, POOL_WIDTH))
        else:
            wsum = jnp.where(grp == g, ws, wsum)
            cnt = jnp.where(grp == g, c, cnt)
    diff = wsum / cnt - x
    o_ref[...] = _bdot(diff, w_ref[...]) * scale_ref[...]


def _pool(pp, w_bd, scale, seq, first_block, nseq):
    return pl.pallas_call(
        _pool_kernel,
        grid=(nseq,),
        in_specs=[pl.BlockSpec((seq, POOL_WIDTH), lambda b: (first_block + b, 0)),
                  pl.BlockSpec(w_bd.shape, lambda b: (0, 0)),
                  pl.BlockSpec(scale.shape, lambda b: (0, 0))],
        out_specs=pl.BlockSpec((seq, POOL_WIDTH), lambda b: (b, 0)),
        out_shape=jax.ShapeDtypeStruct((nseq * seq, POOL_WIDTH), F32),
        compiler_params=_params(("parallel",)),
        name="pool_mix",
    )(pp, w_bd, scale)


def _gla_rows(reverse):
    out = []
    for s in range(GLA_CHUNK):
        g = s // SUBLANES
        out.append((0, SUBLANES * (g + 1)) if reverse else (SUBLANES * g, GLA_CHUNK))
    return out


GLA_PAIR_ROWS = sum(hi - lo for lo, hi in _gla_rows(False))


def _gla_kernel(q_ref, k_ref, v_ref, la_ref, s0_ref, hexp_ref, tri_ref, bd_ref,
                o_ref, sout_ref, st_ref, b_ref, p_ref, z_ref, *, reverse, nchunk):
    j = pl.program_id(1)

    @pl.when(j == 0)
    def _():
        st_ref[...] = s0_ref[...]

    rows = _gla_rows(reverse)
    ngrp = GLA_CHUNK // SUBLANES
    for c in (range(nchunk - 1, -1, -1) if reverse else range(nchunk)):
        r0 = c * GLA_CHUNK
        q = q_ref[r0:r0 + GLA_CHUNK, :]
        k = k_ref[r0:r0 + GLA_CHUNK, :]
        v = v_ref[r0:r0 + GLA_CHUNK, :]
        b = _fdot(tri_ref[...], la_ref[r0:r0 + GLA_CHUNK, :])
        b_ref[...] = b
        blast = b[0:1, :] if reverse else b[GLA_CHUNK - 1:GLA_CHUNK, :]
        st = st_ref[...]
        inter = lax.dot_general((q * jnp.exp(b)).astype(BF16), st.astype(BF16),
                                (((1,), (1,)), ((), ())), preferred_element_type=F32)
        off = 0
        tio = lax.broadcasted_iota(jnp.int32, (SUBLANES, 1), 0)
        for s, (lo, hi) in enumerate(rows):
            dlt = jnp.minimum(b[lo:hi] - b_ref[s:s + 1, :], 0.0)
            p = q[lo:hi] * k_ref[r0 + s:r0 + s + 1, :] * jnp.exp(dlt)
            p_ref[off:off + hi - lo, :] = p
            d0 = hi - lo - SUBLANES if reverse else 0
            valid = (tio <= s % SUBLANES) if reverse else (tio >= s % SUBLANES)
            p_ref[off + d0:off + d0 + SUBLANES, :] = jnp.where(valid, p[d0:d0 + SUBLANES], 0.0)
            off += hi - lo
        z_ref[...] = jnp.dot(p_ref[...].astype(BF16), hexp_ref[...], preferred_element_type=F32)
        oacc = [None] * ngrp
        off = 0
        for g in range(ngrp):
            lo, hi = rows[g * SUBLANES]
            cg = None
            for s in range(g * SUBLANES, (g + 1) * SUBLANES):
                term = z_ref[off:off + hi - lo, :] * v_ref[r0 + s:r0 + s + 1, :]
                cg = term if cg is None else cg + term
                off += hi - lo
            for rg in range(lo // SUBLANES, hi // SUBLANES):
                piece = cg[(rg - lo // SUBLANES) * SUBLANES:(rg - lo // SUBLANES + 1) * SUBLANES]
                oacc[rg] = piece if oacc[rg] is None else oacc[rg] + piece
        o_ref[r0:r0 + GLA_CHUNK, :] = jnp.concatenate(oacc, axis=0) + inter
        ke = k * jnp.exp(blast - b)
        kv = jnp.dot(v.T.astype(BF16), ke.astype(BF16), preferred_element_type=F32)
        st_ref[...] = st * jnp.exp(blast) + kv * bd_ref[...]

    @pl.when(j == pl.num_programs(1) - 1)
    def _():
        sout_ref[...] = st_ref[...]


def _gla_consts():
    hd = np.arange(GLA_QK) // GLA_DK
    he = np.arange(GLA_WIDTH) // GLA_DV
    hexp = (hd[:, None] == he[None, :]).astype(np.float32)
    bd = hexp.T.copy()
    t = np.arange(GLA_CHUNK)
    tri_f = (t[None, :] <= t[:, None]).astype(np.float32)
    tri_b = (t[None, :] >= t[:, None]).astype(np.float32)
    return jnp.asarray(hexp, BF16), jnp.asarray(bd, F32), jnp.asarray(tri_f), jnp.asarray(tri_b)


def _gla(gq, gk, gv, la, s0t, consts, seq, first_block, nseq, reverse):
    hexp, bd, tri_f, tri_b = consts
    tri = tri_b if reverse else tri_f
    blk = min(GLA_BLOCK, seq)
    nblk = seq // blk
    if reverse:
        tok = lambda b, j: (first_block + b * nblk + (nblk - 1 - j), 0)
        otok = lambda b, j: (b * nblk + (nblk - 1 - j), 0)
    else:
        tok = lambda b, j: (first_block + b * nblk + j, 0)
        otok = lambda b, j: (b * nblk + j, 0)
    const = lambda a: pl.BlockSpec(a.shape, lambda b, j: (0,) * a.ndim)
    return pl.pallas_call(
        functools.partial(_gla_kernel, reverse=reverse, nchunk=blk // GLA_CHUNK),
        grid=(nseq, nblk),
        in_specs=[pl.BlockSpec((blk, GLA_QK), tok), pl.BlockSpec((blk, GLA_QK), tok),
                  pl.BlockSpec((blk, GLA_WIDTH), tok), pl.BlockSpec((blk, GLA_QK), tok),
                  pl.BlockSpec((None, GLA_WIDTH, GLA_QK), lambda b, j: (b, 0, 0)),
                  const(hexp), const(tri), const(bd)],
        out_specs=[pl.BlockSpec((blk, GLA_WIDTH), otok),
                   pl.BlockSpec((None, GLA_WIDTH, GLA_QK), lambda b, j: (b, 0, 0))],
        out_shape=[jax.ShapeDtypeStruct((nseq * seq, GLA_WIDTH), F32),
                   jax.ShapeDtypeStruct((nseq, GLA_WIDTH, GLA_QK), F32)],
        scratch_shapes=[pltpu.VMEM((GLA_WIDTH, GLA_QK), F32), pltpu.VMEM((GLA_CHUNK, GLA_QK), F32),
                        pltpu.VMEM((GLA_PAIR_ROWS, GLA_QK), F32), pltpu.VMEM((GLA_PAIR_ROWS, GLA_WIDTH), F32)],
        compiler_params=_params(("parallel", "arbitrary")),
        name="gla_bwd" if reverse else "gla_fwd",
    )(gq, gk, gv, la, s0t, hexp, tri, bd)


def _state_to_blockdiag_t(s):
    b = s.shape[0]
    eye = jnp.eye(GLA_HEADS, dtype=s.dtype)
    return jnp.einsum('bhde,hg->bhegd', s, eye).reshape(b, GLA_WIDTH, GLA_QK)


def _blockdiag_t_to_state(st):
    b = st.shape[0]
    eye = jnp.eye(GLA_HEADS, dtype=st.dtype)
    return jnp.einsum('bhegd,hg->bhde', st.reshape(b, GLA_HEADS, GLA_DV, GLA_HEADS, GLA_DK), eye)


def _stack_heads(q_ref, kv):
    return jnp.concatenate([q_ref[:, (kv * ATTN_GROUP + r) * HEAD_DIM:(kv * ATTN_GROUP + r + 1) * HEAD_DIM]
                            for r in range(ATTN_GROUP)], axis=0)


def _sink_column(sink_ref, kv, rows):
    return jnp.concatenate([jnp.broadcast_to(sink_ref[kv * ATTN_GROUP + r:kv * ATTN_GROUP + r + 1, 0:1], (rows, 1))
                            for r in range(ATTN_GROUP)], axis=0)


def _qk(q, k):
    return lax.dot_general(q.astype(BF16), k.astype(BF16), (((1,), (1,)), ((), ())), preferred_element_type=F32)


def _ctx_attn_kernel(q_ref, k_ref, v_ref, sink_ref, o_ref):
    t = q_ref.shape[0]
    for kv in range(ATTN_KV_HEADS):
        k = k_ref[:, kv * HEAD_DIM:(kv + 1) * HEAD_DIM]
        v = v_ref[:, kv * HEAD_DIM:(kv + 1) * HEAD_DIM]
        s = _qk(_stack_heads(q_ref, kv), k)
        sink = _sink_column(sink_ref, kv, t)
        m = jnp.maximum(jnp.max(s, axis=-1, keepdims=True), sink)
        p = jnp.exp(s - m)
        den = jnp.sum(p, axis=-1, keepdims=True) + jnp.exp(sink - m)
        o = _bdot(p, v) / den
        for r in range(ATTN_GROUP):
            h = kv * ATTN_GROUP + r
            o_ref[:, h * HEAD_DIM:(h + 1) * HEAD_DIM] = o[r * t:(r + 1) * t]


def _ctx_attn(aq, ak, av, sink_b, seq, nseq):
    tok = lambda w: pl.BlockSpec((seq, w), lambda b: (b, 0))
    return pl.pallas_call(
        _ctx_attn_kernel,
        grid=(nseq,),
        in_specs=[tok(ATTN_WIDTH), tok(KV_WIDTH), tok(KV_WIDTH), pl.BlockSpec(sink_b.shape, lambda b: (0, 0))],
        out_specs=tok(ATTN_WIDTH),
        out_shape=jax.ShapeDtypeStruct((nseq * seq, ATTN_WIDTH), F32),
        compiler_params=_params(("parallel",)),
        name="ctx_attn",
    )(aq, ak, av, sink_b)


def _lat_attn_kernel(q_ref, k_ref, v_ref, ck_ref, cv_ref, sink_ref, o_ref):
    t = k_ref.shape[0]
    blk = ATTN_BLOCK
    span = 3 * blk
    i = pl.program_id(1)
    start = pl.multiple_of(jnp.clip((i - 1) * blk, 0, t - span), blk)
    kw = k_ref[pl.ds(start, span), :]
    vw = v_ref[pl.ds(start, span), :]
    qpos = i * blk + lax.broadcasted_iota(jnp.int32, (blk, 1), 0)
    kpos = start + lax.broadcasted_iota(jnp.int32, (1, span), 1)
    inwin = jnp.abs(qpos - kpos) <= WINDOW
    inwin = jnp.concatenate([inwin] * ATTN_GROUP, axis=0)
    for kv in range(ATTN_KV_HEADS):
        lanes = slice(kv * HEAD_DIM, (kv + 1) * HEAD_DIM)
        q = _stack_heads(q_ref, kv)
        s_loc = jnp.where(inwin, _qk(q, kw[:, lanes]), NEG_INF)
        s_ctx = _qk(q, ck_ref[:, lanes])
        sink = _sink_column(sink_ref, kv, blk)
        m = jnp.maximum(jnp.maximum(jnp.max(s_loc, axis=-1, keepdims=True),
                                    jnp.max(s_ctx, axis=-1, keepdims=True)), sink)
        p_loc = jnp.exp(s_loc - m)
        p_ctx = jnp.exp(s_ctx - m)
        den = (jnp.sum(p_loc, axis=-1, keepdims=True) + jnp.sum(p_ctx, axis=-1, keepdims=True)
               + jnp.exp(sink - m))
        o = (_bdot(p_loc, vw[:, lanes]) + _bdot(p_ctx, cv_ref[:, lanes])) / den
        for r in range(ATTN_GROUP):
            h = kv * ATTN_GROUP + r
            o_ref[:, h * HEAD_DIM:(h + 1) * HEAD_DIM] = o[r * blk:(r + 1) * blk]


def _lat_attn(aq, ak, av, ck, cv, sink_b, seq, first_tok, nseq):
    nblk = seq // ATTN_BLOCK
    fb_q = first_tok // ATTN_BLOCK
    fb_s = first_tok // seq
    return pl.pallas_call(
        _lat_attn_kernel,
        grid=(nseq, nblk),
        in_specs=[pl.BlockSpec((ATTN_BLOCK, ATTN_WIDTH), lambda b, i: (fb_q + b * nblk + i, 0)),
                  pl.BlockSpec((seq, KV_WIDTH), lambda b, i: (fb_s + b, 0)),
                  pl.BlockSpec((seq, KV_WIDTH), lambda b, i: (fb_s + b, 0)),
                  pl.BlockSpec((None,) + ck.shape[1:], lambda b, i: (b, 0, 0)),
                  pl.BlockSpec((None,) + cv.shape[1:], lambda b, i: (b, 0, 0)),
                  pl.BlockSpec(sink_b.shape, lambda b, i: (0, 0))],
        out_specs=pl.BlockSpec((ATTN_BLOCK, ATTN_WIDTH), lambda b, i: (b * nblk + i, 0)),
        out_shape=jax.ShapeDtypeStruct((nseq * seq, ATTN_WIDTH), F32),
        compiler_params=_params(("parallel", "arbitrary")),
        name="lat_attn",
    )(aq, ak, av, ck, cv, sink_b)


def _outproj_kernel(yp_ref, of_ref, ob_ref, gg_ref, ya_ref, x_ref, mod_ref, gn_ref, hm_ref, w_ref, n2_ref,
                    xo_ref, h2t_ref):
    o = of_ref[...] + ob_ref[...]
    ms = _fdot(o * o, hm_ref[...])
    gg = gg_ref[...]
    y = o * lax.rsqrt(ms + EPS) * gn_ref[...] * (gg * _sigmoid(gg))
    mix = jnp.concatenate([yp_ref[...], y, ya_ref[...]], axis=1)
    xn = x_ref[...] + mod_ref[2:3, :] * _bdot(mix, w_ref[...])
    xo_ref[...] = xn
    h2 = _rms(xn) * n2_ref[...] * (1.0 + mod_ref[4:5, :]) + mod_ref[3:4, :]
    h2t_ref[...] = pltpu.bitcast(h2.T.astype(BF16), jnp.uint32)


def _outproj(ypool, o_f, o_b, gg, yattn, x, modblk, gnorm, hmean, w_out, n2):
    n, d = x.shape
    tm = TOKEN_BLOCK
    row = lambda w: pl.BlockSpec((tm, w), lambda i: (i, 0))
    full = lambda a: pl.BlockSpec(a.shape, lambda i: (0,) * a.ndim)
    return pl.pallas_call(
        _outproj_kernel,
        grid=(n // tm,),
        in_specs=[row(POOL_WIDTH), row(GLA_WIDTH), row(GLA_WIDTH), row(GLA_WIDTH), row(ATTN_WIDTH), row(d),
                  pl.BlockSpec((None, 6, d), lambda i: (i, 0, 0)), full(gnorm), full(hmean), full(w_out), full(n2)],
        out_specs=[row(d), pl.BlockSpec((d // 2, tm), lambda i: (0, i))],
        out_shape=[jax.ShapeDtypeStruct((n, d), F32), jax.ShapeDtypeStruct((d // 2, n), jnp.uint32)],
        compiler_params=_params(("parallel",)),
        name="out_proj",
    )(ypool, o_f, o_b, gg, yattn, x, modblk, gnorm, hmean, w_out, n2)


def _peer_score_kernel(h2t_ref, wqt_ref, sk_ref, s1_ref, s2_ref):
    qt = jnp.dot(wqt_ref[...], pltpu.bitcast(h2t_ref[...], BF16), preferred_element_type=F32)
    for h in range(PEER_HEADS):
        for p, out in enumerate((s1_ref, s2_ref)):
            r = (2 * h + p) * PEER_HALF
            out[h] = jnp.dot(sk_ref[2 * h + p], qt[r:r + PEER_HALF].astype(BF16), preferred_element_type=F32)


def _peer_scores(h2t, wqt, sk):
    d, n = h2t.shape
    tb = TOKEN_BLOCK
    nkeys = sk.shape[1]
    out = pl.BlockSpec((PEER_HEADS, nkeys, tb), lambda i: (0, 0, i))
    return pl.pallas_call(
        _peer_score_kernel,
        grid=(n // tb,),
        in_specs=[pl.BlockSpec((d, tb), lambda i: (0, i)),
                  pl.BlockSpec(wqt.shape, lambda i: (0, 0)),
                  pl.BlockSpec(sk.shape, lambda i: (0, 0, 0))],
        out_specs=[out, out],
        out_shape=[jax.ShapeDtypeStruct((PEER_HEADS, nkeys, n), F32)] * 2,
        compiler_params=_params(("parallel",)),
        name="peer_scores",
    )(h2t, wqt, sk)


PEER_CANDS = [(a, b) for a in range(PEER_TOPK) for b in range(PEER_TOPK) if (a + 1) * (b + 1) <= PEER_TOPK]
PEER_CAND_ROWS = -(-len(PEER_CANDS) // SUBLANES) * SUBLANES


def _top_values(cur, count, with_rank=False):
    vals = []
    rank = jnp.full(cur.shape, float(count), F32) if with_rank else None
    for r in range(count):
        m = jnp.max(cur, axis=0, keepdims=True)
        vals.append(m)
        hit = cur == m
        if with_rank:
            rank = jnp.where(hit, float(r), rank)
        cur = jnp.where(hit, -jnp.inf, cur)
    return (vals, rank) if with_rank else vals


PEER_DENSE_RANKS = 4
PEER_GATE_TILE = 512
assert all(b < PEER_DENSE_RANKS for a, b in PEER_CANDS if a >= PEER_DENSE_RANKS)


def _peer_gate_kernel(s1_ref, s2_ref, cnt_ref, a_ref, rk_ref, bx_ref, cand_ref):
    width = cand_ref.shape[1]
    ntile = s1_ref.shape[2] // width

    def one_tile(it, carry):
        h = it // ntile
        cols = pl.ds(pl.multiple_of((it % ntile) * width, width), width)
        s1 = s1_ref[h, :, cols]
        s2 = s2_ref[h, :, cols]
        v1 = _top_values(s1, PEER_TOPK)
        v2, rk = _top_values(s2, PEER_TOPK, with_rank=True)
        cand_ref[...] = jnp.full(cand_ref.shape, -jnp.inf, F32)
        for r, (a, b) in enumerate(PEER_CANDS):
            cand_ref[r:r + 1, :] = v1[a] + v2[b]
        best = _top_values(cand_ref[...], PEER_TOPK)
        z = None
        for val in best:
            e = jnp.exp(val - best[0])
            z = e if z is None else z + e
        tau = best[-1]
        cnt = None
        for b in range(PEER_DENSE_RANKS):
            reach = jnp.where(s1 + v2[b] >= tau, 1.0, 0.0)
            cnt = reach if cnt is None else cnt + reach
        for a in range(PEER_DENSE_RANKS):
            full = None
            for b in range(PEER_TOPK // (a + 1)):
                reach = jnp.where(v1[a] + v2[b] >= tau, 1.0, 0.0)
                full = reach if full is None else full + reach
            cnt = jnp.where(s1 == v1[a], full, cnt)
        cnt_ref[h, :, cols] = cnt
        a_ref[h, :, cols] = jnp.exp(s1 - v1[0]) * (0.5 / z)
        rk_ref[h, :, cols] = pltpu.bitcast(rk.astype(BF16), jnp.uint32)
        bx_ref[h, :, cols] = pltpu.bitcast(jnp.exp(s2 - v2[0]).astype(BF16), jnp.uint32)
        return carry

    lax.fori_loop(0, PEER_HEADS * ntile, one_tile, 0)


def _peer_gates(s1, s2):
    _, nkeys, n = s1.shape
    tb = TOKEN_BLOCK
    blk = pl.BlockSpec((PEER_HEADS, nkeys, tb), lambda i: (0, 0, i))
    half = pl.BlockSpec((PEER_HEADS, nkeys // 2, tb), lambda i: (0, 0, i))
    return pl.pallas_call(
        _peer_gate_kernel,
        grid=(n // tb,),
        in_specs=[blk, blk],
        out_specs=[blk, blk, half, half],
        out_shape=[jax.ShapeDtypeStruct((PEER_HEADS, nkeys, n), F32),
                   jax.ShapeDtypeStruct((PEER_HEADS, nkeys, n), F32),
                   jax.ShapeDtypeStruct((PEER_HEADS, nkeys // 2, n), jnp.uint32),
                   jax.ShapeDtypeStruct((PEER_HEADS, nkeys // 2, n), jnp.uint32)],
        scratch_shapes=[pltpu.VMEM((PEER_CAND_ROWS, PEER_GATE_TILE), F32)],
        compiler_params=_params(("parallel",)),
        name="peer_gates",
    )(s1, s2)


PEER_ROW_GROUP = 4
PEER_TILE_ROWS = 128


def _peer_dense_kernel(h2t_ref, u_ref, vt_ref, cnt_ref, a_ref, rk_ref, bx_ref, x_ref, mod_ref,
                       xo_ref, yt_ref, at_ref, w_ref, *, nkeys):
    e = pl.program_id(1)
    per_step = u_ref.shape[0] // nkeys

    @pl.when(e == 0)
    def _():
        yt_ref[...] = jnp.zeros(yt_ref.shape, F32)

    at_ref[...] = jnp.dot(u_ref[...], pltpu.bitcast(h2t_ref[...], BF16), preferred_element_type=F32)
    first_key = pl.multiple_of(e * per_step, SUBLANES)
    zero = jnp.zeros((PEER_TILE_ROWS, LANES), BF16)
    for i8 in range(0, per_step, SUBLANES):
        for tt in range(at_ref.shape[1] // LANES):
            cols = slice(tt * LANES, (tt + 1) * LANES)
            for jh in range(nkeys // PEER_TILE_ROWS):
                words = slice(jh * PEER_TILE_ROWS // 2, (jh + 1) * PEER_TILE_ROWS // 2)
                for i0 in range(i8, i8 + SUBLANES, PEER_ROW_GROUP):
                    g = [None] * PEER_ROW_GROUP
                    for h in range(PEER_HEADS):
                        rk = pltpu.bitcast(rk_ref[h, words, cols], BF16)
                        bx = pltpu.bitcast(bx_ref[h, words, cols], BF16)
                        cnts = cnt_ref[h, pl.ds(first_key + i8, SUBLANES), cols]
                        arows = a_ref[h, pl.ds(first_key + i8, SUBLANES), cols]
                        for ii in range(PEER_ROW_GROUP):
                            r = i0 - i8 + ii
                            term = (jnp.where(rk < cnts[r:r + 1, :].astype(BF16), bx, zero)
                                    * arows[r:r + 1, :].astype(BF16))
                            g[ii] = term if g[ii] is None else g[ii] + term
                    for ii in range(PEER_ROW_GROUP):
                        r0 = (i0 + ii) * nkeys + jh * PEER_TILE_ROWS
                        rows = slice(r0, r0 + PEER_TILE_ROWS)
                        xb = at_ref[rows, cols].astype(BF16)
                        w_ref[rows, cols] = g[ii] * (xb * (1.0 + lax.erf(xb * (2.0 ** -0.5))))
    yt_ref[...] += jnp.dot(vt_ref[...], w_ref[...], preferred_element_type=F32)

    @pl.when(e == pl.num_programs(1) - 1)
    def _():
        xo_ref[...] = x_ref[...] + mod_ref[5:6, :] * yt_ref[...].T


def _peer_dense(h2t, u, vt, cnt, a, rk, bx, x, modblk):
    n, d = x.shape
    nexp = u.shape[0]
    nkeys = cnt.shape[1]
    tb = TOKEN_BLOCK
    eb = min(PEER_EXPERT_BLOCK, nexp)
    ne = nexp // eb
    assert nkeys == LANES and eb % (SUBLANES * nkeys) == 0 and SUBLANES % PEER_ROW_GROUP == 0 and ne * eb == nexp
    keyed = pl.BlockSpec((PEER_HEADS, nkeys, tb), lambda t, e: (0, 0, t))
    packed = pl.BlockSpec((PEER_HEADS, nkeys // 2, tb), lambda t, e: (0, 0, t))
    return pl.pallas_call(
        functools.partial(_peer_dense_kernel, nkeys=nkeys),
        grid=(n // tb, ne),
        in_specs=[pl.BlockSpec((d // 2, tb), lambda t, e: (0, t)),
                  pl.BlockSpec((eb, d), lambda t, e: (e, 0)),
                  pl.BlockSpec((d, eb), lambda t, e: (0, e)),
                  keyed, keyed, packed, packed,
                  pl.BlockSpec((tb, d), lambda t, e: (t, 0)),
                  pl.BlockSpec((None, 6, d), lambda t, e: (t, 0, 0))],
        out_specs=pl.BlockSpec((tb, d), lambda t, e: (t, 0)),
        out_shape=jax.ShapeDtypeStruct((n, d), F32),
        scratch_shapes=[pltpu.VMEM((d, tb), F32), pltpu.VMEM((eb, tb), F32), pltpu.VMEM((eb, tb), BF16)],
        compiler_params=_params(("parallel", "arbitrary")),
        name="peer_dense",
    )(h2t, u, vt, cnt, a, rk, bx, x, modblk)


def _final_norm_kernel(x_ref, g_ref, o_ref):
    o_ref[...] = _rms(x_ref[...]) * g_ref[...]


def _final_norm(x, g):
    n, d = x.shape
    tm = TOKEN_BLOCK
    return pl.pallas_call(
        _final_norm_kernel,
        grid=(n // tm,),
        in_specs=[pl.BlockSpec((tm, d), lambda i: (i, 0)), pl.BlockSpec((1, d), lambda i: (0, 0))],
        out_specs=pl.BlockSpec((tm, d), lambda i: (i, 0)),
        out_shape=jax.ShapeDtypeStruct((n, d), F32),
        compiler_params=_params(("parallel",)),
        name="final_norm",
    )(x, g.reshape(1, d))


def _rope_swap_columns(width):
    half = HEAD_DIM // 2
    nf = half // 2
    perm = np.zeros(width, np.int32)
    sign = np.zeros(width, np.float32)
    for c in range(width):
        r = c % half
        if r < nf:
            perm[c], sign[c] = c + nf, -1.0
        else:
            perm[c], sign[c] = c - nf, 1.0
    return perm, sign


def _pack_w_in(w_in):
    d = w_in.shape[0]
    o_aq = POOL_WIDTH + 2 * GLA_QK + 2 * GLA_WIDTH + 2 * GLA_GATE_RANK
    o_ak = o_aq + ATTN_WIDTH
    o_av = o_ak + KV_WIDTH
    w_aq = w_in[:, o_aq:o_ak]
    w_ak = w_in[:, o_ak:o_av]
    pq, sq = _rope_swap_columns(ATTN_WIDTH)
    pk, sk = _rope_swap_columns(KV_WIDTH)
    gz = jnp.pad(w_in[:, o_aq - 2 * GLA_GATE_RANK:o_aq], ((0, 0), (0, LANES - 2 * GLA_GATE_RANK)))
    cols = [w_in[:, :o_aq - 2 * GLA_GATE_RANK], gz, w_aq, w_ak, w_in[:, o_av:],
            w_aq[:, pq] * sq[None, :], w_ak[:, pk] * sk[None, :]]
    return jnp.concatenate(cols, axis=1).astype(BF16)


def _block_diag(w):
    g, c, _ = w.shape
    eye = jnp.eye(g, dtype=w.dtype)
    return jnp.einsum('gcd,gh->gchd', w, eye).reshape(g * c, g * c)


def _rope_tables(n_ctx, n_lat_seq, n_lat_batch):
    half = HEAD_DIM // 2
    nf = half // 2
    freqs = ROPE_THETA ** (-jnp.arange(nf, dtype=F32) / nf)
    rows = n_lat_seq // GRID_W
    pos_row = jnp.repeat(jnp.arange(rows), GRID_W).astype(F32)
    pos_col = jnp.tile(jnp.arange(GRID_W), rows).astype(F32)
    ar = pos_row[:, None] * freqs[None, :]
    ac = pos_col[:, None] * freqs[None, :]
    cos = jnp.concatenate([jnp.cos(ar)] * 2 + [jnp.cos(ac)] * 2, axis=1)
    sin = jnp.concatenate([jnp.sin(ar)] * 2 + [jnp.sin(ac)] * 2, axis=1)
    reps = LANES // HEAD_DIM
    cos = jnp.tile(jnp.tile(cos, (1, reps)), (n_lat_batch, 1))
    sin = jnp.tile(jnp.tile(sin, (1, reps)), (n_lat_batch, 1))
    cos = jnp.concatenate([jnp.ones((n_ctx, LANES), F32), cos], axis=0)
    sin = jnp.concatenate([jnp.zeros((n_ctx, LANES), F32), sin], axis=0)
    return cos, sin


def kernel(x_prompt, x_sample, cache_k, cache_v, state_fwd, state_bwd, c, c_ctx, w_ada, b_ada, norm1_g, norm2_g,
           w_in, pool_w, pool_scale, gla_gate_w_f, gla_gate_b_f, gla_gate_w_b, gla_gate_b_b, gla_norm_g,
           attn_sink, w_out, peer_wq, peer_subkeys, peer_u, peer_v, final_norm_g):
    nb, seq, d = x_prompt.shape
    nlb, lseq, _ = x_sample.shape
    depth = w_ada.shape[0]
    n_ctx, n_lat = nb * seq, nlb * lseq
    n = n_ctx + n_lat
    tm = TOKEN_BLOCK
    nkeys = peer_subkeys.shape[3]
    assert d == D_MODEL and n_ctx % tm == 0 and lseq % tm == 0 and n_ctx % lseq == 0
    assert seq % GLA_CHUNK == 0 and lseq % GLA_BLOCK == 0 and lseq % GRID_W == 0 and lseq >= 3 * ATTN_BLOCK
    assert nlb + 1 <= SUBLANES and nkeys % SUBLANES == 0

    x0 = jnp.concatenate([x_prompt.reshape(n_ctx, d), x_sample.reshape(n_lat, d)], axis=0)
    cvec = jnp.zeros((SUBLANES, d), F32).at[0].set(c_ctx).at[1:1 + nlb].set(c)
    mods = _ada_mods(cvec, w_ada, b_ada)
    blk_row = np.concatenate([np.zeros(n_ctx // tm, np.int32),
                              1 + np.repeat(np.arange(nlb, dtype=np.int32), lseq // tm)])
    modblk = mods[:, blk_row, :].reshape(depth, n // tm, 6, d)
    cos, sin = _rope_tables(n_ctx, lseq, nlb)
    gla_consts = _gla_consts()
    hd = np.arange(GLA_WIDTH) // GLA_DV
    hmean = jnp.asarray((hd[:, None] == hd[None, :]).astype(np.float32) / GLA_DV)
    zero_state = jnp.zeros((nb, GLA_WIDTH, GLA_QK), F32)

    def layer(x, lp):
        (mod_l, n1, n2, w_in_l, pool_w_l, pool_scale_l, gw_f, gb_f, gw_b, gb_b, gnorm, sink, w_out_l,
         wq, subk, pu, pv, ck, cv, sf, sb) = lp
        w_big = _pack_w_in(w_in_l)
        w2 = jnp.zeros((LANES, 2 * GLA_QK), F32)
        w2 = w2.at[:GLA_GATE_RANK, :GLA_QK].set(gw_f).at[GLA_GATE_RANK:2 * GLA_GATE_RANK, GLA_QK:].set(gw_b)
        b2 = jnp.concatenate([gb_f, gb_b]).reshape(1, 2 * GLA_QK)
        pp, gq, gk, gv, gg, la_f, la_b, aq, ak, av = _inproj(x, mod_l, n1.reshape(1, d), w_big, cos, sin, w2, b2)

        w_bd = _block_diag(pool_w_l).astype(BF16)
        scale = pool_scale_l.reshape(1, POOL_WIDTH)
        y_pool = jnp.concatenate([_pool(pp, w_bd, scale, seq, 0, nb),
                                  _pool(pp, w_bd, scale, lseq, n_ctx // lseq, nlb)], axis=0)

        lblk = min(GLA_BLOCK, lseq)
        of_c, sf_c = _gla(gq, gk, gv, la_f, zero_state, gla_consts, seq, 0, nb, False)
        ob_c, sb_c = _gla(gq, gk, gv, la_b, zero_state, gla_consts, seq, 0, nb, True)
        of_l, _ = _gla(gq, gk, gv, la_f, _state_to_blockdiag_t(sf), gla_consts, lseq, n_ctx // lblk, nlb, False)
        ob_l, _ = _gla(gq, gk, gv, la_b, _state_to_blockdiag_t(sb), gla_consts, lseq, n_ctx // lblk, nlb, True)
        o_f = jnp.concatenate([of_c, of_l], axis=0)
        o_b = jnp.concatenate([ob_c, ob_l], axis=0)

        sink_b = jnp.broadcast_to(sink.reshape(ATTN_HEADS, 1), (ATTN_HEADS, LANES))
        y_attn = jnp.concatenate([_ctx_attn(aq, ak, av, sink_b, seq, nb),
                                  _lat_attn(aq, ak, av, ck, cv, sink_b, lseq, n_ctx, nlb)], axis=0)

        x1, h2t = _outproj(y_pool, o_f, o_b, gg, y_attn, x, mod_l, gnorm.reshape(1, GLA_WIDTH), hmean,
                           w_out_l.astype(BF16), n2.reshape(1, d))

        wqt = wq.T.astype(BF16)
        sk = subk.reshape(2 * PEER_HEADS, nkeys, PEER_HALF).astype(BF16)
        s1, s2 = _peer_scores(h2t, wqt, sk)
        cnt, a, rk, bx = _peer_gates(s1, s2)
        x2 = _peer_dense(h2t, pu.astype(BF16), pv.T.astype(BF16), cnt, a, rk, bx, x1, mod_l)
        return x2, (ak[:n_ctx], av[:n_ctx], _blockdiag_t_to_state(sf_c), _blockdiag_t_to_state(sb_c))

    past = cache_k.shape[2]
    xs = (modblk, norm1_g, norm2_g, w_in, pool_w, pool_scale, gla_gate_w_f, gla_gate_b_f, gla_gate_w_b,
          gla_gate_b_b, gla_norm_g, attn_sink, w_out, peer_wq, peer_subkeys, peer_u, peer_v,
          jnp.swapaxes(cache_k, 0, 1).reshape(depth, nlb, past, KV_WIDTH),
          jnp.swapaxes(cache_v, 0, 1).reshape(depth, nlb, past, KV_WIDTH),
          jnp.swapaxes(state_fwd, 0, 1), jnp.swapaxes(state_bwd, 0, 1))
    x_fin, (ks, vs, sfs, sbs) = lax.scan(layer, x0, xs)

    y = _final_norm(x_fin, final_norm_g)
    y_prompt = y[:n_ctx].reshape(nb, seq, d)
    y_sample = y[n_ctx:].reshape(nlb, lseq, d)
    new_k = jnp.swapaxes(ks.reshape(depth, nb, seq, ATTN_KV_HEADS, HEAD_DIM), 0, 1)
    new_v = jnp.swapaxes(vs.reshape(depth, nb, seq, ATTN_KV_HEADS, HEAD_DIM), 0, 1)
    return (y_prompt, y_sample, new_k, new_v, jnp.swapaxes(sfs, 0, 1), jnp.swapaxes(sbs, 0, 1))
```

```python
import functools
import math

import numpy as np
import jax
import jax.numpy as jnp
from jax import lax
from jax.experimental import pallas as pl
from jax.experimental.pallas import tpu as pltpu

F32 = jnp.float32
BF16 = jnp.bfloat16
HIGHEST = lax.Precision.HIGHEST

D_MODEL = 1024
GRID_W = 64
EPS = 1e-6
POOL_WIDTH = 256
POOL_GROUPS = 4
POOL_GROUP_DIM = 64
POOL_WINDOWS = (2, 4, 8, 16)
GLA_HEADS = 4
GLA_DV = 64
GLA_DK = 32
GLA_QK = GLA_HEADS * GLA_DK
GLA_WIDTH = GLA_HEADS * GLA_DV
GLA_GATE_RANK = 16
GLA_GATE_TAU = 16.0
GLA_CHUNK = 32
ATTN_HEADS = 8
ATTN_KV_HEADS = 2
ATTN_GROUP = 4
HEAD_DIM = 64
ATTN_WIDTH = ATTN_HEADS * HEAD_DIM
KV_WIDTH = ATTN_KV_HEADS * HEAD_DIM
WINDOW = 128
ATTN_BLOCK = 128
ROPE_THETA = 10000.0
NEG_INF = -1e30
PEER_HEADS = 8
PEER_HALF = 64
PEER_TOPK = 16

LANES = 128
SUBLANES = 8
VMEM_LIMIT = 56 * 1024 * 1024

TOKEN_BLOCK = 512
GLA_BLOCK = 512
PEER_EXPERT_BLOCK = 2048

C_POOL, C_GQ, C_GK, C_GV, C_GG, C_GZ, C_AQ, C_AK, C_AV, C_AQS, C_AKS, C_END = (
    0, 256, 384, 512, 768, 1024, 1152, 1664, 1792, 1920, 2432, 2560)


def _params(sem):
    return pltpu.CompilerParams(dimension_semantics=sem, vmem_limit_bytes=VMEM_LIMIT)


def _bdot(a, b):
    return jnp.dot(a.astype(BF16), b.astype(BF16), preferred_element_type=F32)


def _fdot(a, b):
    return jnp.dot(a, b, preferred_element_type=F32, precision=HIGHEST)


def _sigmoid(x):
    return 1.0 / (1.0 + jnp.exp(-x))


def _rms(x):
    return x * lax.rsqrt(jnp.mean(x * x, axis=-1, keepdims=True) + EPS)


def _ada_kernel(c_ref, w_ref, b_ref, o_ref):
    c = c_ref[...]
    o_ref[...] = _fdot(c * _sigmoid(c), w_ref[...]) + b_ref[...]


def _ada_mods(cvec, w_ada, b_ada):
    depth, d, six_d = w_ada.shape
    nj = six_d // d
    return pl.pallas_call(
        _ada_kernel,
        grid=(depth, nj),
        in_specs=[pl.BlockSpec((SUBLANES, d), lambda l, j: (0, 0)),
                  pl.BlockSpec((None, d, d), lambda l, j: (l, 0, j)),
                  pl.BlockSpec((None, 1, d), lambda l, j: (l, 0, j))],
        out_specs=pl.BlockSpec((None, SUBLANES, d), lambda l, j: (l, 0, j)),
        out_shape=jax.ShapeDtypeStruct((depth, SUBLANES, six_d), F32),
        compiler_params=_params(("parallel", "parallel")),
        name="ada_mod",
    )(cvec, w_ada, b_ada.reshape(depth, 1, six_d))


def _inproj_kernel(x_ref, mod_ref, n1_ref, w_ref, cos_ref, sin_ref, w2_ref, b2_ref,
                   pp_ref, gq_ref, gk_ref, gv_ref, gg_ref, laf_ref, lab_ref, aq_ref, ak_ref, av_ref):
    h = _rms(x_ref[...]) * n1_ref[...]
    h = (h * (1.0 + mod_ref[1:2, :]) + mod_ref[0:1, :]).astype(BF16)

    def proj(lo, hi):
        return jnp.dot(h, w_ref[:, lo:hi], preferred_element_type=F32)

    pp_ref[...] = proj(C_POOL, C_GQ)
    gq_ref[...] = proj(C_GQ, C_GK) * (GLA_DK ** -0.5)
    gk_ref[...] = proj(C_GK, C_GV)
    gv_ref[...] = proj(C_GV, C_GG)
    gg_ref[...] = proj(C_GG, C_GZ)
    logit = _fdot(proj(C_GZ, C_AQ), w2_ref[...]) + b2_ref[...]
    la = (jnp.minimum(logit, 0.0) - jnp.log1p(jnp.exp(-jnp.abs(logit)))) * (1.0 / GLA_GATE_TAU)
    laf_ref[...] = la[:, :GLA_QK]
    lab_ref[...] = la[:, GLA_QK:]
    cos = cos_ref[...]
    sin = sin_ref[...]
    cos4 = jnp.concatenate([cos] * (ATTN_WIDTH // LANES), axis=1)
    sin4 = jnp.concatenate([sin] * (ATTN_WIDTH // LANES), axis=1)
    aq_ref[...] = (proj(C_AQ, C_AK) * cos4 + proj(C_AQS, C_AKS) * sin4) * (HEAD_DIM ** -0.5)
    ak_ref[...] = proj(C_AK, C_AV) * cos + proj(C_AKS, C_END) * sin
    av_ref[...] = proj(C_AV, C_AQS)


def _inproj(x, modblk, n1, w_big, cos, sin, w2, b2):
    n, d = x.shape
    tm = TOKEN_BLOCK
    widths = (POOL_WIDTH, GLA_QK, GLA_QK, GLA_WIDTH, GLA_WIDTH, GLA_QK, GLA_QK, ATTN_WIDTH, KV_WIDTH, KV_WIDTH)
    row = lambda w: pl.BlockSpec((tm, w), lambda i: (i, 0))
    full = lambda a: pl.BlockSpec(a.shape, lambda i: (0,) * a.ndim)
    return pl.pallas_call(
        _inproj_kernel,
        grid=(n // tm,),
        in_specs=[row(d), pl.BlockSpec((None, 6, d), lambda i: (i, 0, 0)), full(n1), full(w_big),
                  row(LANES), row(LANES), full(w2), full(b2)],
        out_specs=[row(w) for w in widths],
        out_shape=[jax.ShapeDtypeStruct((n, w), F32) for w in widths],
        compiler_params=_params(("parallel",)),
        name="in_proj",
    )(x, modblk, n1, w_big, cos, sin, w2, b2)


def _pool_kernel(x_ref, w_ref, scale_ref, o_ref):
    t = x_ref.shape[0]
    pad = 32
    n = t + pad
    x = x_ref[...]
    xp = jnp.concatenate([x, jnp.zeros((pad, POOL_WIDTH), F32)], axis=0)
    back = {1: xp}
    for w in (2, 4, 8, 16):
        back[w] = back[w // 2] + pltpu.roll(back[w // 2], w // 2, 0)
    grp = lax.broadcasted_iota(jnp.int32, (1, POOL_WIDTH), 1) // POOL_GROUP_DIM
    tok = lax.broadcasted_iota(jnp.int32, (t, 1), 0)
    wsum = None
    cnt = None
    for g, w in enumerate(POOL_WINDOWS):
        left = w // 2
        right = w - 1 - left
        ws = back[w] if right == 0 else pltpu.roll(back[w], n - right, 0)
        ws = ws[:t]
        c = (jnp.minimum(tok + right + 1, t) - jnp.maximum(tok - left, 0)).astype(F32)
        if wsum is None:
            wsum, cnt = ws, jnp.broadcast_to(c, (t, POOL_WIDTH))
        else:
            wsum = jnp.where(grp == g, ws, wsum)
            cnt = jnp.where(grp == g, c, cnt)
    diff = wsum / cnt - x
    o_ref[...] = _bdot(diff, w_ref[...]) * scale_ref[...]


def _pool(pp, w_bd, scale, seq, first_block, nseq):
    return pl.pallas_call(
        _pool_kernel,
        grid=(nseq,),
        in_specs=[pl.BlockSpec((seq, POOL_WIDTH), lambda b: (first_block + b, 0)),
                  pl.BlockSpec(w_bd.shape, lambda b: (0, 0)),
                  pl.BlockSpec(scale.shape, lambda b: (0, 0))],
        out_specs=pl.BlockSpec((seq, POOL_WIDTH), lambda b: (b, 0)),
        out_shape=jax.ShapeDtypeStruct((nseq * seq, POOL_WIDTH), F32),
        compiler_params=_params(("parallel",)),
        name="pool_mix",
    )(pp, w_bd, scale)


def _gla_rows(reverse):
    out = []
    for s in range(GLA_CHUNK):
        g = s // SUBLANES
        out.append((0, SUBLANES * (g + 1)) if reverse else (SUBLANES * g, GLA_CHUNK))
    return out


GLA_PAIR_ROWS = sum(hi - lo for lo, hi in _gla_rows(False))


def _gla_kernel(q_ref, k_ref, v_ref, la_ref, s0_ref, hexp_ref, tri_ref, bd_ref,
                o_ref, sout_ref, st_ref, b_ref, p_ref, z_ref, *, reverse, nchunk):
    j = pl.program_id(1)

    @pl.when(j == 0)
    def _():
        st_ref[...] = s0_ref[...]

    rows = _gla_rows(reverse)
    ngrp = GLA_CHUNK // SUBLANES
    order = list(range(nchunk - 1, -1, -1) if reverse else range(nchunk))
    chunk = lambda c: slice(c * GLA_CHUNK, (c + 1) * GLA_CHUNK)
    for c in order:
        b_ref[chunk(c), :] = _fdot(tri_ref[...], la_ref[chunk(c), :])
    tio = lax.broadcasted_iota(jnp.int32, (SUBLANES, 1), 0)
    for c in order:
        r0 = c * GLA_CHUNK
        off = c * GLA_PAIR_ROWS
        q = q_ref[chunk(c), :]
        b = b_ref[chunk(c), :]
        for s, (lo, hi) in enumerate(rows):
            dlt = jnp.minimum(b[lo:hi] - b_ref[r0 + s:r0 + s + 1, :], 0.0)
            p = q[lo:hi] * k_ref[r0 + s:r0 + s + 1, :] * jnp.exp(dlt)
            p_ref[off:off + hi - lo, :] = p
            d0 = hi - lo - SUBLANES if reverse else 0
            valid = (tio <= s % SUBLANES) if reverse else (tio >= s % SUBLANES)
            p_ref[off + d0:off + d0 + SUBLANES, :] = jnp.where(valid, p[d0:d0 + SUBLANES], 0.0)
            off += hi - lo
    z_ref[...] = jnp.dot(p_ref[...].astype(BF16), hexp_ref[...], preferred_element_type=F32)
    for c in order:
        r0 = c * GLA_CHUNK
        off = c * GLA_PAIR_ROWS
        oacc = [None] * ngrp
        for g in range(ngrp):
            lo, hi = rows[g * SUBLANES]
            cg = None
            for s in range(g * SUBLANES, (g + 1) * SUBLANES):
                term = z_ref[off:off + hi - lo, :] * v_ref[r0 + s:r0 + s + 1, :]
                cg = term if cg is None else cg + term
                off += hi - lo
            for rg in range(lo // SUBLANES, hi // SUBLANES):
                piece = cg[(rg - lo // SUBLANES) * SUBLANES:(rg - lo // SUBLANES + 1) * SUBLANES]
                oacc[rg] = piece if oacc[rg] is None else oacc[rg] + piece
        o_ref[chunk(c), :] = jnp.concatenate(oacc, axis=0)
    kvs, decay = {}, {}
    for c in order:
        b = b_ref[chunk(c), :]
        blast = b[0:1, :] if reverse else b[GLA_CHUNK - 1:GLA_CHUNK, :]
        ke = k_ref[chunk(c), :] * jnp.exp(blast - b)
        kvs[c] = jnp.dot(v_ref[chunk(c), :].T.astype(BF16), ke.astype(BF16),
                         preferred_element_type=F32) * bd_ref[...]
        decay[c] = jnp.exp(blast)
    st = st_ref[...]
    for c in order:
        qe = (q_ref[chunk(c), :] * jnp.exp(b_ref[chunk(c), :])).astype(BF16)
        o_ref[chunk(c), :] += lax.dot_general(qe, st.astype(BF16), (((1,), (1,)), ((), ())),
                                              preferred_element_type=F32)
        st = st * decay[c] + kvs[c]
    st_ref[...] = st

    @pl.when(j == pl.num_programs(1) - 1)
    def _():
        sout_ref[...] = st_ref[...]


def _gla_consts():
    hd = np.arange(GLA_QK) // GLA_DK
    he = np.arange(GLA_WIDTH) // GLA_DV
    hexp = (hd[:, None] == he[None, :]).astype(np.float32)
    bd = hexp.T.copy()
    t = np.arange(GLA_CHUNK)
    tri_f = (t[None, :] <= t[:, None]).astype(np.float32)
    tri_b = (t[None, :] >= t[:, None]).astype(np.float32)
    return jnp.asarray(hexp, BF16), jnp.asarray(bd, F32), jnp.asarray(tri_f), jnp.asarray(tri_b)


def _gla(gq, gk, gv, la, s0t, consts, seq, first_block, nseq, reverse):
    hexp, bd, tri_f, tri_b = consts
    tri = tri_b if reverse else tri_f
    blk = min(GLA_BLOCK, seq)
    nblk = seq // blk
    if reverse:
        tok = lambda b, j: (first_block + b * nblk + (nblk - 1 - j), 0)
        otok = lambda b, j: (b * nblk + (nblk - 1 - j), 0)
    else:
        tok = lambda b, j: (first_block + b * nblk + j, 0)
        otok = lambda b, j: (b * nblk + j, 0)
    const = lambda a: pl.BlockSpec(a.shape, lambda b, j: (0,) * a.ndim)
    return pl.pallas_call(
        functools.partial(_gla_kernel, reverse=reverse, nchunk=blk // GLA_CHUNK),
        grid=(nseq, nblk),
        in_specs=[pl.BlockSpec((blk, GLA_QK), tok), pl.BlockSpec((blk, GLA_QK), tok),
                  pl.BlockSpec((blk, GLA_WIDTH), tok), pl.BlockSpec((blk, GLA_QK), tok),
                  pl.BlockSpec((None, GLA_WIDTH, GLA_QK), lambda b, j: (b, 0, 0)),
                  const(hexp), const(tri), const(bd)],
        out_specs=[pl.BlockSpec((blk, GLA_WIDTH), otok),
                   pl.BlockSpec((None, GLA_WIDTH, GLA_QK), lambda b, j: (b, 0, 0))],
        out_shape=[jax.ShapeDtypeStruct((nseq * seq, GLA_WIDTH), F32),
                   jax.ShapeDtypeStruct((nseq, GLA_WIDTH, GLA_QK), F32)],
        scratch_shapes=[pltpu.VMEM((GLA_WIDTH, GLA_QK), F32), pltpu.VMEM((blk, GLA_QK), F32),
                        pltpu.VMEM((blk // GLA_CHUNK * GLA_PAIR_ROWS, GLA_QK), F32),
                        pltpu.VMEM((blk // GLA_CHUNK * GLA_PAIR_ROWS, GLA_WIDTH), F32)],
        compiler_params=_params(("parallel", "arbitrary")),
        name="gla_bwd" if reverse else "gla_fwd",
    )(gq, gk, gv, la, s0t, hexp, tri, bd)


def _state_to_blockdiag_t(s):
    b = s.shape[0]
    eye = jnp.eye(GLA_HEADS, dtype=s.dtype)
    return jnp.einsum('bhde,hg->bhegd', s, eye).reshape(b, GLA_WIDTH, GLA_QK)


def _blockdiag_t_to_state(st):
    b = st.shape[0]
    eye = jnp.eye(GLA_HEADS, dtype=st.dtype)
    return jnp.einsum('bhegd,hg->bhde', st.reshape(b, GLA_HEADS, GLA_DV, GLA_HEADS, GLA_DK), eye)


ATTN_SUBBLOCKS = 4


def _stack_heads(q_ref, kv, rows=slice(None)):
    return jnp.concatenate([q_ref[rows, (kv * ATTN_GROUP + r) * HEAD_DIM:(kv * ATTN_GROUP + r + 1) * HEAD_DIM]
                            for r in range(ATTN_GROUP)], axis=0)


def _sink_column(sink_ref, kv, rows):
    return jnp.concatenate([jnp.broadcast_to(sink_ref[kv * ATTN_GROUP + r:kv * ATTN_GROUP + r + 1, 0:1], (rows, 1))
                            for r in range(ATTN_GROUP)], axis=0)


def _qk(q, k):
    return lax.dot_general(q.astype(BF16), k.astype(BF16), (((1,), (1,)), ((), ())), preferred_element_type=F32)


def _ctx_attn_kernel(q_ref, k_ref, v_ref, sink_ref, o_ref, *, seq):
    t = seq
    chains = [(slice(sb * seq, (sb + 1) * seq), kv)
              for sb in range(q_ref.shape[0] // seq) for kv in range(ATTN_KV_HEADS)]
    lanes = lambda kv: slice(kv * HEAD_DIM, (kv + 1) * HEAD_DIM)
    s = [_qk(_stack_heads(q_ref, kv, rows), k_ref[rows, lanes(kv)]) for rows, kv in chains]
    sink = [_sink_column(sink_ref, kv, t) for _, kv in chains]
    m = [jnp.maximum(jnp.max(s[c], axis=-1, keepdims=True), sink[c]) for c in range(len(chains))]
    p = [jnp.exp(s[c] - m[c]) for c in range(len(chains))]
    den = [jnp.sum(p[c], axis=-1, keepdims=True) + jnp.exp(sink[c] - m[c]) for c in range(len(chains))]
    for c, (rows, kv) in enumerate(chains):
        o = _bdot(p[c], v_ref[rows, lanes(kv)]) / den[c]
        for r in range(ATTN_GROUP):
            h = kv * ATTN_GROUP + r
            o_ref[rows, h * HEAD_DIM:(h + 1) * HEAD_DIM] = o[r * t:(r + 1) * t]


def _ctx_attn(aq, ak, av, sink_b, seq, nseq):
    per = ATTN_SUBBLOCKS if nseq % ATTN_SUBBLOCKS == 0 else 1
    tok = lambda w: pl.BlockSpec((per * seq, w), lambda b: (b, 0))
    return pl.pallas_call(
        functools.partial(_ctx_attn_kernel, seq=seq),
        grid=(nseq // per,),
        in_specs=[tok(ATTN_WIDTH), tok(KV_WIDTH), tok(KV_WIDTH), pl.BlockSpec(sink_b.shape, lambda b: (0, 0))],
        out_specs=tok(ATTN_WIDTH),
        out_shape=jax.ShapeDtypeStruct((nseq * seq, ATTN_WIDTH), F32),
        compiler_params=_params(("parallel",)),
        name="ctx_attn",
    )(aq, ak, av, sink_b)


def _lat_attn_kernel(q_ref, k_ref, v_ref, ck_ref, cv_ref, sink_ref, o_ref):
    t = k_ref.shape[0]
    blk = ATTN_BLOCK
    span = 3 * blk
    per = q_ref.shape[0] // blk
    lanes = lambda kv: slice(kv * HEAD_DIM, (kv + 1) * HEAD_DIM)
    chains, starts, inwin = [], [], []
    for sb in range(per):
        i = pl.program_id(1) * per + sb
        start = pl.multiple_of(jnp.clip((i - 1) * blk, 0, t - span), blk)
        qpos = i * blk + lax.broadcasted_iota(jnp.int32, (blk, 1), 0)
        kpos = start + lax.broadcasted_iota(jnp.int32, (1, span), 1)
        win = jnp.concatenate([jnp.abs(qpos - kpos) <= WINDOW] * ATTN_GROUP, axis=0)
        for kv in range(ATTN_KV_HEADS):
            chains.append((slice(sb * blk, (sb + 1) * blk), kv))
            starts.append(start)
            inwin.append(win)
    n = len(chains)
    q = [_stack_heads(q_ref, kv, rows) for rows, kv in chains]
    s_loc = [jnp.where(inwin[c], _qk(q[c], k_ref[pl.ds(starts[c], span), :][:, lanes(chains[c][1])]), NEG_INF)
             for c in range(n)]
    s_ctx = [_qk(q[c], ck_ref[:, lanes(chains[c][1])]) for c in range(n)]
    sink = [_sink_column(sink_ref, kv, blk) for _, kv in chains]
    m = [jnp.maximum(jnp.maximum(jnp.max(s_loc[c], axis=-1, keepdims=True),
                                 jnp.max(s_ctx[c], axis=-1, keepdims=True)), sink[c]) for c in range(n)]
    p_loc = [jnp.exp(s_loc[c] - m[c]) for c in range(n)]
    p_ctx = [jnp.exp(s_ctx[c] - m[c]) for c in range(n)]
    den = [jnp.sum(p_loc[c], axis=-1, keepdims=True) + jnp.sum(p_ctx[c], axis=-1, keepdims=True)
           + jnp.exp(sink[c] - m[c]) for c in range(n)]
    for c, (rows, kv) in enumerate(chains):
        o = (_bdot(p_loc[c], v_ref[pl.ds(starts[c], span), :][:, lanes(kv)])
             + _bdot(p_ctx[c], cv_ref[:, lanes(kv)])) / den[c]
        for r in range(ATTN_GROUP):
            h = kv * ATTN_GROUP + r
            o_ref[rows, h * HEAD_DIM:(h + 1) * HEAD_DIM] = o[r * blk:(r + 1) * blk]


def _lat_attn(aq, ak, av, ck, cv, sink_b, seq, first_tok, nseq):
    qrows = ATTN_SUBBLOCKS * ATTN_BLOCK
    assert seq % qrows == 0 and first_tok % qrows == 0
    nblk = seq // qrows
    fb_q = first_tok // qrows
    fb_s = first_tok // seq
    return pl.pallas_call(
        _lat_attn_kernel,
        grid=(nseq, nblk),
        in_specs=[pl.BlockSpec((qrows, ATTN_WIDTH), lambda b, i: (fb_q + b * nblk + i, 0)),
                  pl.BlockSpec((seq, KV_WIDTH), lambda b, i: (fb_s + b, 0)),
                  pl.BlockSpec((seq, KV_WIDTH), lambda b, i: (fb_s + b, 0)),
                  pl.BlockSpec((None,) + ck.shape[1:], lambda b, i: (b, 0, 0)),
                  pl.BlockSpec((None,) + cv.shape[1:], lambda b, i: (b, 0, 0)),
                  pl.BlockSpec(sink_b.shape, lambda b, i: (0, 0))],
        out_specs=pl.BlockSpec((qrows, ATTN_WIDTH), lambda b, i: (b * nblk + i, 0)),
        out_shape=jax.ShapeDtypeStruct((nseq * seq, ATTN_WIDTH), F32),
        compiler_params=_params(("parallel", "arbitrary")),
        name="lat_attn",
    )(aq, ak, av, ck, cv, sink_b)


def _outproj_kernel(ypc_ref, ypl_ref, ofc_ref, ofl_ref, obc_ref, obl_ref, yac_ref, yal_ref,
                    gg_ref, x_ref, mod_ref, gn_ref, hm_ref, w_ref, n2_ref, xo_ref, h2t_ref, *, ctx_blocks):
    is_ctx = pl.program_id(0) < ctx_blocks
    pick = lambda c_ref, l_ref: jnp.where(is_ctx, c_ref[...], l_ref[...])
    o = pick(ofc_ref, ofl_ref) + pick(obc_ref, obl_ref)
    ms = _fdot(o * o, hm_ref[...])
    gg = gg_ref[...]
    y = o * lax.rsqrt(ms + EPS) * gn_ref[...] * (gg * _sigmoid(gg))
    mix = jnp.concatenate([pick(ypc_ref, ypl_ref), y, pick(yac_ref, yal_ref)], axis=1)
    xn = x_ref[...] + mod_ref[2:3, :] * _bdot(mix, w_ref[...])
    xo_ref[...] = xn
    h2 = _rms(xn) * n2_ref[...] * (1.0 + mod_ref[4:5, :]) + mod_ref[3:4, :]
    h2t_ref[...] = pltpu.bitcast(h2.T.astype(BF16), jnp.uint32)


def _outproj(ypool, o_f, o_b, yattn, gg, x, modblk, gnorm, hmean, w_out, n2):
    n, d = x.shape
    tm = TOKEN_BLOCK
    cb = ypool[0].shape[0] // tm
    lb = ypool[1].shape[0] // tm
    row = lambda w: pl.BlockSpec((tm, w), lambda i: (i, 0))
    ctx = lambda w: pl.BlockSpec((tm, w), lambda i: (jnp.minimum(i, cb - 1), 0))
    lat = lambda w: pl.BlockSpec((tm, w), lambda i: (jnp.clip(i - cb, 0, lb - 1), 0))
    full = lambda a: pl.BlockSpec(a.shape, lambda i: (0,) * a.ndim)
    pair = lambda w: [ctx(w), lat(w)]
    return pl.pallas_call(
        functools.partial(_outproj_kernel, ctx_blocks=cb),
        grid=(n // tm,),
        in_specs=pair(POOL_WIDTH) + pair(GLA_WIDTH) + pair(GLA_WIDTH) + pair(ATTN_WIDTH) + [
            row(GLA_WIDTH), row(d), pl.BlockSpec((None, 6, d), lambda i: (i, 0, 0)),
            full(gnorm), full(hmean), full(w_out), full(n2)],
        out_specs=[row(d), pl.BlockSpec((d // 2, tm), lambda i: (0, i))],
        out_shape=[jax.ShapeDtypeStruct((n, d), F32), jax.ShapeDtypeStruct((d // 2, n), jnp.uint32)],
        compiler_params=_params(("parallel",)),
        name="out_proj",
    )(*ypool, *o_f, *o_b, *yattn, gg, x, modblk, gnorm, hmean, w_out, n2)


def _peer_score_kernel(h2t_ref, wqt_ref, sk_ref, s1_ref, s2_ref):
    qt = jnp.dot(wqt_ref[...], pltpu.bitcast(h2t_ref[...], BF16), preferred_element_type=F32)
    for h in range(PEER_HEADS):
        for p, out in enumerate((s1_ref, s2_ref)):
            r = (2 * h + p) * PEER_HALF
            out[h] = jnp.dot(sk_ref[2 * h + p], qt[r:r + PEER_HALF].astype(BF16), preferred_element_type=F32)


def _peer_scores(h2t, wqt, sk):
    d, n = h2t.shape
    tb = TOKEN_BLOCK
    nkeys = sk.shape[1]
    out = pl.BlockSpec((PEER_HEADS, nkeys, tb), lambda i: (0, 0, i))
    return pl.pallas_call(
        _peer_score_kernel,
        grid=(n // tb,),
        in_specs=[pl.BlockSpec((d, tb), lambda i: (0, i)),
                  pl.BlockSpec(wqt.shape, lambda i: (0, 0)),
                  pl.BlockSpec(sk.shape, lambda i: (0, 0, 0))],
        out_specs=[out, out],
        out_shape=[jax.ShapeDtypeStruct((PEER_HEADS, nkeys, n), F32)] * 2,
        compiler_params=_params(("parallel",)),
        name="peer_scores",
    )(h2t, wqt, sk)


PEER_CANDS = [(a, b) for a in range(PEER_TOPK) for b in range(PEER_TOPK) if (a + 1) * (b + 1) <= PEER_TOPK]
PEER_CAND_ROWS = -(-len(PEER_CANDS) // SUBLANES) * SUBLANES


def _top_values_many(arrays, count, with_rank):
    cur = list(arrays)
    vals = [[] for _ in cur]
    rank = [jnp.full(c.shape, float(count), F32) if w else None for c, w in zip(cur, with_rank)]
    for r in range(count):
        for k in range(len(cur)):
            m = jnp.max(cur[k], axis=0, keepdims=True)
            vals[k].append(m)
            hit = cur[k] == m
            if with_rank[k]:
                rank[k] = jnp.where(hit, float(r), rank[k])
            cur[k] = jnp.where(hit, -jnp.inf, cur[k])
    return vals, rank


def _top_values(cur, count):
    return _top_values_many([cur], count, [False])[0][0]


PEER_DENSE_RANKS = 4
PEER_GATE_TILE = 512
assert all(b < PEER_DENSE_RANKS for a, b in PEER_CANDS if a >= PEER_DENSE_RANKS)


def _peer_gate_kernel(s1_ref, s2_ref, cnt_ref, a_ref, rk_ref, bx_ref, cand_ref):
    width = cand_ref.shape[1]
    ntile = s1_ref.shape[2] // width

    def one_tile(it, carry):
        h = it // ntile
        cols = pl.ds(pl.multiple_of((it % ntile) * width, width), width)
        s1 = s1_ref[h, :, cols]
        s2 = s2_ref[h, :, cols]
        (v1, v2), (_, rk) = _top_values_many([s1, s2], PEER_TOPK, [False, True])
        cand_ref[...] = jnp.full(cand_ref.shape, -jnp.inf, F32)
        for r, (a, b) in enumerate(PEER_CANDS):
            cand_ref[r:r + 1, :] = v1[a] + v2[b]
        best = _top_values(cand_ref[...], PEER_TOPK)
        z = None
        for val in best:
            e = jnp.exp(val - best[0])
            z = e if z is None else z + e
        tau = best[-1]
        cnt = None
        for b in range(PEER_DENSE_RANKS):
            reach = jnp.where(s1 + v2[b] >= tau, 1.0, 0.0)
            cnt = reach if cnt is None else cnt + reach
        for a in range(PEER_DENSE_RANKS):
            full = None
            for b in range(PEER_TOPK // (a + 1)):
                reach = jnp.where(v1[a] + v2[b] >= tau, 1.0, 0.0)
                full = reach if full is None else full + reach
            cnt = jnp.where(s1 == v1[a], full, cnt)
        cnt_ref[h, :, cols] = cnt
        a_ref[h, :, cols] = jnp.exp(s1 - v1[0]) * (0.5 / z)
        rk_ref[h, :, cols] = pltpu.bitcast(rk.astype(BF16), jnp.uint32)
        bx_ref[h, :, cols] = pltpu.bitcast(jnp.exp(s2 - v2[0]).astype(BF16), jnp.uint32)
        return carry

    lax.fori_loop(0, PEER_HEADS * ntile, one_tile, 0)


def _peer_gates(s1, s2):
    _, nkeys, n = s1.shape
    tb = TOKEN_BLOCK
    blk = pl.BlockSpec((PEER_HEADS, nkeys, tb), lambda i: (0, 0, i))
    half = pl.BlockSpec((PEER_HEADS, nkeys // 2, tb), lambda i: (0, 0, i))
    return pl.pallas_call(
        _peer_gate_kernel,
        grid=(n // tb,),
        in_specs=[blk, blk],
        out_specs=[blk, blk, half, half],
        out_shape=[jax.ShapeDtypeStruct((PEER_HEADS, nkeys, n), F32),
                   jax.ShapeDtypeStruct((PEER_HEADS, nkeys, n), F32),
                   jax.ShapeDtypeStruct((PEER_HEADS, nkeys // 2, n), jnp.uint32),
                   jax.ShapeDtypeStruct((PEER_HEADS, nkeys // 2, n), jnp.uint32)],
        scratch_shapes=[pltpu.VMEM((PEER_CAND_ROWS, PEER_GATE_TILE), F32)],
        compiler_params=_params(("parallel",)),
        name="peer_gates",
    )(s1, s2)


PEER_ROW_GROUP = 4
PEER_TILE_ROWS = 128


def _peer_dense_kernel(h2t_ref, u_ref, vt_ref, cnt_ref, a_ref, rk_ref, bx_ref, x_ref, mod_ref,
                       xo_ref, yt_ref, at_ref, w_ref, *, nkeys):
    e = pl.program_id(1)
    per_step = u_ref.shape[0] // nkeys

    @pl.when(e == 0)
    def _():
        yt_ref[...] = jnp.zeros(yt_ref.shape, F32)

    at_ref[...] = jnp.dot(u_ref[...], pltpu.bitcast(h2t_ref[...], BF16), preferred_element_type=F32)
    first_key = pl.multiple_of(e * per_step, SUBLANES)
    zero = jnp.zeros((PEER_TILE_ROWS, LANES), BF16)
    for i8 in range(0, per_step, SUBLANES):
        for tt in range(at_ref.shape[1] // LANES):
            cols = slice(tt * LANES, (tt + 1) * LANES)
            for jh in range(nkeys // PEER_TILE_ROWS):
                words = slice(jh * PEER_TILE_ROWS // 2, (jh + 1) * PEER_TILE_ROWS // 2)
                for i0 in range(i8, i8 + SUBLANES, PEER_ROW_GROUP):
                    g = [None] * PEER_ROW_GROUP
                    for h in range(PEER_HEADS):
                        rk = pltpu.bitcast(rk_ref[h, words, cols], BF16)
                        bx = pltpu.bitcast(bx_ref[h, words, cols], BF16)
                        cnts = cnt_ref[h, pl.ds(first_key + i8, SUBLANES), cols]
                        arows = a_ref[h, pl.ds(first_key + i8, SUBLANES), cols]
                        for ii in range(PEER_ROW_GROUP):
                            r = i0 - i8 + ii
                            term = (jnp.where(rk < cnts[r:r + 1, :].astype(BF16), bx, zero)
                                    * arows[r:r + 1, :].astype(BF16))
                            g[ii] = term if g[ii] is None else g[ii] + term
                    for ii in range(PEER_ROW_GROUP):
                        r0 = (i0 + ii) * nkeys + jh * PEER_TILE_ROWS
                        rows = slice(r0, r0 + PEER_TILE_ROWS)
                        xb = at_ref[rows, cols].astype(BF16)
                        w_ref[rows, cols] = g[ii] * (xb * (1.0 + lax.erf(xb * (2.0 ** -0.5))))
    yt_ref[...] += jnp.dot(vt_ref[...], w_ref[...], preferred_element_type=F32)

    @pl.when(e == pl.num_programs(1) - 1)
    def _():
        xo_ref[...] = x_ref[...] + mod_ref[5:6, :] * yt_ref[...].T


def _peer_dense(h2t, u, vt, cnt, a, rk, bx, x, modblk):
    n, d = x.shape
    nexp = u.shape[0]
    nkeys = cnt.shape[1]
    tb = TOKEN_BLOCK
    eb = min(PEER_EXPERT_BLOCK, nexp)
    ne = nexp // eb
    assert nkeys == LANES and eb % (SUBLANES * nkeys) == 0 and SUBLANES % PEER_ROW_GROUP == 0 and ne * eb == nexp
    keyed = pl.BlockSpec((PEER_HEADS, nkeys, tb), lambda t, e: (0, 0, t))
    packed = pl.BlockSpec((PEER_HEADS, nkeys // 2, tb), lambda t, e: (0, 0, t))
    return pl.pallas_call(
        functools.partial(_peer_dense_kernel, nkeys=nkeys),
        grid=(n // tb, ne),
        in_specs=[pl.BlockSpec((d // 2, tb), lambda t, e: (0, t)),
                  pl.BlockSpec((eb, d), lambda t, e: (e, 0)),
                  pl.BlockSpec((d, eb), lambda t, e: (0, e)),
                  keyed, keyed, packed, packed,
                  pl.BlockSpec((tb, d), lambda t, e: (t, 0)),
                  pl.BlockSpec((None, 6, d), lambda t, e: (t, 0, 0))],
        out_specs=pl.BlockSpec((tb, d), lambda t, e: (t, 0)),
        out_shape=jax.ShapeDtypeStruct((n, d), F32),
        scratch_shapes=[pltpu.VMEM((d, tb), F32), pltpu.VMEM((eb, tb), F32), pltpu.VMEM((eb, tb), BF16)],
        compiler_params=_params(("parallel", "arbitrary")),
        name="peer_dense",
    )(h2t, u, vt, cnt, a, rk, bx, x, modblk)


def _final_norm_kernel(x_ref, g_ref, o_ref):
    o_ref[...] = _rms(x_ref[...]) * g_ref[...]


def _final_norm(x, g):
    n, d = x.shape
    tm = TOKEN_BLOCK
    return pl.pallas_call(
        _final_norm_kernel,
        grid=(n // tm,),
        in_specs=[pl.BlockSpec((tm, d), lambda i: (i, 0)), pl.BlockSpec((1, d), lambda i: (0, 0))],
        out_specs=pl.BlockSpec((tm, d), lambda i: (i, 0)),
        out_shape=jax.ShapeDtypeStruct((n, d), F32),
        compiler_params=_params(("parallel",)),
        name="final_norm",
    )(x, g.reshape(1, d))


def _rope_swap_columns(width):
    half = HEAD_DIM // 2
    nf = half // 2
    perm = np.zeros(width, np.int32)
    sign = np.zeros(width, np.float32)
    for c in range(width):
        r = c % half
        if r < nf:
            perm[c], sign[c] = c + nf, -1.0
        else:
            perm[c], sign[c] = c - nf, 1.0
    return perm, sign


def _pack_w_in(w_in):
    d = w_in.shape[0]
    o_aq = POOL_WIDTH + 2 * GLA_QK + 2 * GLA_WIDTH + 2 * GLA_GATE_RANK
    o_ak = o_aq + ATTN_WIDTH
    o_av = o_ak + KV_WIDTH
    w_aq = w_in[:, o_aq:o_ak]
    w_ak = w_in[:, o_ak:o_av]
    pq, sq = _rope_swap_columns(ATTN_WIDTH)
    pk, sk = _rope_swap_columns(KV_WIDTH)
    gz = jnp.pad(w_in[:, o_aq - 2 * GLA_GATE_RANK:o_aq], ((0, 0), (0, LANES - 2 * GLA_GATE_RANK)))
    cols = [w_in[:, :o_aq - 2 * GLA_GATE_RANK], gz, w_aq, w_ak, w_in[:, o_av:],
            w_aq[:, pq] * sq[None, :], w_ak[:, pk] * sk[None, :]]
    return jnp.concatenate(cols, axis=1).astype(BF16)


def _block_diag(w):
    g, c, _ = w.shape
    eye = jnp.eye(g, dtype=w.dtype)
    return jnp.einsum('gcd,gh->gchd', w, eye).reshape(g * c, g * c)


def _rope_tables(n_ctx, n_lat_seq, n_lat_batch):
    half = HEAD_DIM // 2
    nf = half // 2
    freqs = ROPE_THETA ** (-jnp.arange(nf, dtype=F32) / nf)
    rows = n_lat_seq // GRID_W
    pos_row = jnp.repeat(jnp.arange(rows), GRID_W).astype(F32)
    pos_col = jnp.tile(jnp.arange(GRID_W), rows).astype(F32)
    ar = pos_row[:, None] * freqs[None, :]
    ac = pos_col[:, None] * freqs[None, :]
    cos = jnp.concatenate([jnp.cos(ar)] * 2 + [jnp.cos(ac)] * 2, axis=1)
    sin = jnp.concatenate([jnp.sin(ar)] * 2 + [jnp.sin(ac)] * 2, axis=1)
    reps = LANES // HEAD_DIM
    cos = jnp.tile(jnp.tile(cos, (1, reps)), (n_lat_batch, 1))
    sin = jnp.tile(jnp.tile(sin, (1, reps)), (n_lat_batch, 1))
    cos = jnp.concatenate([jnp.ones((n_ctx, LANES), F32), cos], axis=0)
    sin = jnp.concatenate([jnp.zeros((n_ctx, LANES), F32), sin], axis=0)
    return cos, sin


def kernel(x_prompt, x_sample, cache_k, cache_v, state_fwd, state_bwd, c, c_ctx, w_ada, b_ada, norm1_g, norm2_g,
           w_in, pool_w, pool_scale, gla_gate_w_f, gla_gate_b_f, gla_gate_w_b, gla_gate_b_b, gla_norm_g,
           attn_sink, w_out, peer_wq, peer_subkeys, peer_u, peer_v, final_norm_g):
    nb, seq, d = x_prompt.shape
    nlb, lseq, _ = x_sample.shape
    depth = w_ada.shape[0]
    n_ctx, n_lat = nb * seq, nlb * lseq
    n = n_ctx + n_lat
    tm = TOKEN_BLOCK
    nkeys = peer_subkeys.shape[3]
    assert d == D_MODEL and n_ctx % tm == 0 and lseq % tm == 0 and n_ctx % lseq == 0
    assert seq % GLA_CHUNK == 0 and lseq % GLA_BLOCK == 0 and lseq % GRID_W == 0 and lseq >= 3 * ATTN_BLOCK
    assert nlb + 1 <= SUBLANES and nkeys % SUBLANES == 0

    x0 = jnp.concatenate([x_prompt.reshape(n_ctx, d), x_sample.reshape(n_lat, d)], axis=0)
    cvec = jnp.zeros((SUBLANES, d), F32).at[0].set(c_ctx).at[1:1 + nlb].set(c)
    mods = _ada_mods(cvec, w_ada, b_ada)
    blk_row = np.concatenate([np.zeros(n_ctx // tm, np.int32),
                              1 + np.repeat(np.arange(nlb, dtype=np.int32), lseq // tm)])
    modblk = mods[:, blk_row, :].reshape(depth, n // tm, 6, d)
    cos, sin = _rope_tables(n_ctx, lseq, nlb)
    gla_consts = _gla_consts()
    hd = np.arange(GLA_WIDTH) // GLA_DV
    hmean = jnp.asarray((hd[:, None] == hd[None, :]).astype(np.float32) / GLA_DV)
    zero_state = jnp.zeros((nb, GLA_WIDTH, GLA_QK), F32)

    def layer(x, lp):
        (mod_l, n1, n2, w_in_l, pool_w_l, pool_scale_l, gw_f, gb_f, gw_b, gb_b, gnorm, sink, w_out_l,
         wq, subk, pu, pv, ck, cv, sf, sb) = lp
        w_big = _pack_w_in(w_in_l)
        w2 = jnp.zeros((LANES, 2 * GLA_QK), F32)
        w2 = w2.at[:GLA_GATE_RANK, :GLA_QK].set(gw_f).at[GLA_GATE_RANK:2 * GLA_GATE_RANK, GLA_QK:].set(gw_b)
        b2 = jnp.concatenate([gb_f, gb_b]).reshape(1, 2 * GLA_QK)
        pp, gq, gk, gv, gg, la_f, la_b, aq, ak, av = _inproj(x, mod_l, n1.reshape(1, d), w_big, cos, sin, w2, b2)

        w_bd = _block_diag(pool_w_l).astype(BF16)
        scale = pool_scale_l.reshape(1, POOL_WIDTH)
        y_pool = (_pool(pp, w_bd, scale, seq, 0, nb), _pool(pp, w_bd, scale, lseq, n_ctx // lseq, nlb))

        lblk = min(GLA_BLOCK, lseq)
        of_c, sf_c = _gla(gq, gk, gv, la_f, zero_state, gla_consts, seq, 0, nb, False)
        ob_c, sb_c = _gla(gq, gk, gv, la_b, zero_state, gla_consts, seq, 0, nb, True)
        of_l, _ = _gla(gq, gk, gv, la_f, _state_to_blockdiag_t(sf), gla_consts, lseq, n_ctx // lblk, nlb, False)
        ob_l, _ = _gla(gq, gk, gv, la_b, _state_to_blockdiag_t(sb), gla_consts, lseq, n_ctx // lblk, nlb, True)

        sink_b = jnp.broadcast_to(sink.reshape(ATTN_HEADS, 1), (ATTN_HEADS, LANES))
        y_attn = (_ctx_attn(aq, ak, av, sink_b, seq, nb), _lat_attn(aq, ak, av, ck, cv, sink_b, lseq, n_ctx, nlb))

        x1, h2t = _outproj(y_pool, (of_c, of_l), (ob_c, ob_l), y_attn, gg, x, mod_l,
                           gnorm.reshape(1, GLA_WIDTH), hmean, w_out_l.astype(BF16), n2.reshape(1, d))

        wqt = wq.T.astype(BF16)
        sk = subk.reshape(2 * PEER_HEADS, nkeys, PEER_HALF).astype(BF16)
        s1, s2 = _peer_scores(h2t, wqt, sk)
        cnt, a, rk, bx = _peer_gates(s1, s2)
        x2 = _peer_dense(h2t, pu.astype(BF16), pv.T.astype(BF16), cnt, a, rk, bx, x1, mod_l)
        return x2, (ak[:n_ctx], av[:n_ctx], _blockdiag_t_to_state(sf_c), _blockdiag_t_to_state(sb_c))

    past = cache_k.shape[2]
    xs = (modblk, norm1_g, norm2_g, w_in, pool_w, pool_scale, gla_gate_w_f, gla_gate_b_f, gla_gate_w_b,
          gla_gate_b_b, gla_norm_g, attn_sink, w_out, peer_wq, peer_subkeys, peer_u, peer_v,
          jnp.swapaxes(cache_k, 0, 1).reshape(depth, nlb, past, KV_WIDTH),
          jnp.swapaxes(cache_v, 0, 1).reshape(depth, nlb, past, KV_WIDTH),
          jnp.swapaxes(state_fwd, 0, 1), jnp.swapaxes(state_bwd, 0, 1))
    x_fin, (ks, vs, sfs, sbs) = lax.scan(layer, x0, xs)

    y = _final_norm(x_fin, final_norm_g)
    y_prompt = y[:n_ctx].reshape(nb, seq, d)
    y_sample = y[n_ctx:].reshape(nlb, lseq, d)
    new_k = jnp.swapaxes(ks.reshape(depth, nb, seq, ATTN_KV_HEADS, HEAD_DIM), 0, 1)
    new_v = jnp.swapaxes(vs.reshape(depth, nb, seq, ATTN_KV_HEADS, HEAD_DIM), 0, 1)
    return (y_prompt, y_sample, new_k, new_v, jnp.swapaxes(sfs, 0, 1), jnp.swapaxes(sbs, 0, 1))
```

```python
import functools
import math

import numpy as np
import jax
import jax.numpy as jnp
from jax import lax
from jax.experimental import pallas as pl
from jax.experimental.pallas import tpu as pltpu

F32 = jnp.float32
BF16 = jnp.bfloat16
HIGHEST = lax.Precision.HIGHEST

D_MODEL = 1024
GRID_W = 64
EPS = 1e-6
POOL_WIDTH = 256
POOL_GROUPS = 4
POOL_GROUP_DIM = 64
POOL_WINDOWS = (2, 4, 8, 16)
GLA_HEADS = 4
GLA_DV = 64
GLA_DK = 32
GLA_QK = GLA_HEADS * GLA_DK
GLA_WIDTH = GLA_HEADS * GLA_DV
GLA_GATE_RANK = 16
GLA_GATE_TAU = 16.0
GLA_CHUNK = 32
ATTN_HEADS = 8
ATTN_KV_HEADS = 2
ATTN_GROUP = 4
HEAD_DIM = 64
ATTN_WIDTH = ATTN_HEADS * HEAD_DIM
KV_WIDTH = ATTN_KV_HEADS * HEAD_DIM
WINDOW = 128
ATTN_BLOCK = 128
ROPE_THETA = 10000.0
NEG_INF = -1e30
PEER_HEADS = 8
PEER_HALF = 64
PEER_TOPK = 16

LANES = 128
SUBLANES = 8
VMEM_LIMIT = 56 * 1024 * 1024

TOKEN_BLOCK = 512
GLA_BLOCK = 512
PEER_EXPERT_BLOCK = 2048

C_POOL, C_GQ, C_GK, C_GV, C_GG, C_GZ, C_AQ, C_AK, C_AV, C_AQS, C_AKS, C_END = (
    0, 256, 384, 512, 768, 1024, 1152, 1664, 1792, 1920, 2432, 2560)


def _params(sem):
    return pltpu.CompilerParams(dimension_semantics=sem, vmem_limit_bytes=VMEM_LIMIT)


def _bdot(a, b):
    return jnp.dot(a.astype(BF16), b.astype(BF16), preferred_element_type=F32)


def _fdot(a, b):
    return jnp.dot(a, b, preferred_element_type=F32, precision=HIGHEST)


def _sigmoid(x):
    return 1.0 / (1.0 + jnp.exp(-x))


def _rms(x):
    return x * lax.rsqrt(jnp.mean(x * x, axis=-1, keepdims=True) + EPS)


def _ada_kernel(c_ref, w_ref, b_ref, o_ref):
    c = c_ref[...]
    o_ref[...] = _fdot(c * _sigmoid(c), w_ref[...]) + b_ref[...]


def _ada_mods(cvec, w_ada, b_ada):
    depth, d, six_d = w_ada.shape
    nj = six_d // d
    return pl.pallas_call(
        _ada_kernel,
        grid=(depth, nj),
        in_specs=[pl.BlockSpec((SUBLANES, d), lambda l, j: (0, 0)),
                  pl.BlockSpec((None, d, d), lambda l, j: (l, 0, j)),
                  pl.BlockSpec((None, 1, d), lambda l, j: (l, 0, j))],
        out_specs=pl.BlockSpec((None, SUBLANES, d), lambda l, j: (l, 0, j)),
        out_shape=jax.ShapeDtypeStruct((depth, SUBLANES, six_d), F32),
        compiler_params=_params(("parallel", "parallel")),
        name="ada_mod",
    )(cvec, w_ada, b_ada.reshape(depth, 1, six_d))


def _inproj_kernel(x_ref, mod_ref, n1_ref, w_ref, cos_ref, sin_ref, w2_ref, b2_ref,
                   pp_ref, gq_ref, gk_ref, gv_ref, gg_ref, laf_ref, lab_ref, aq_ref, ak_ref, av_ref):
    h = _rms(x_ref[...]) * n1_ref[...]
    h = (h * (1.0 + mod_ref[1:2, :]) + mod_ref[0:1, :]).astype(BF16)

    def proj(lo, hi):
        return jnp.dot(h, w_ref[:, lo:hi], preferred_element_type=F32)

    pp_ref[...] = proj(C_POOL, C_GQ)
    gq_ref[...] = proj(C_GQ, C_GK) * (GLA_DK ** -0.5)
    gk_ref[...] = proj(C_GK, C_GV)
    gv_ref[...] = proj(C_GV, C_GG)
    gg_ref[...] = proj(C_GG, C_GZ)
    logit = _fdot(proj(C_GZ, C_AQ), w2_ref[...]) + b2_ref[...]
    la = (jnp.minimum(logit, 0.0) - jnp.log1p(jnp.exp(-jnp.abs(logit)))) * (1.0 / GLA_GATE_TAU)
    laf_ref[...] = la[:, :GLA_QK]
    lab_ref[...] = la[:, GLA_QK:]
    cos = cos_ref[...]
    sin = sin_ref[...]
    cos4 = jnp.concatenate([cos] * (ATTN_WIDTH // LANES), axis=1)
    sin4 = jnp.concatenate([sin] * (ATTN_WIDTH // LANES), axis=1)
    aq_ref[...] = (proj(C_AQ, C_AK) * cos4 + proj(C_AQS, C_AKS) * sin4) * (HEAD_DIM ** -0.5)
    ak_ref[...] = proj(C_AK, C_AV) * cos + proj(C_AKS, C_END) * sin
    av_ref[...] = proj(C_AV, C_AQS)


def _inproj(x, modblk, n1, w_big, cos, sin, w2, b2):
    n, d = x.shape
    tm = TOKEN_BLOCK
    widths = (POOL_WIDTH, GLA_QK, GLA_QK, GLA_WIDTH, GLA_WIDTH, GLA_QK, GLA_QK, ATTN_WIDTH, KV_WIDTH, KV_WIDTH)
    row = lambda w: pl.BlockSpec((tm, w), lambda i: (i, 0))
    full = lambda a: pl.BlockSpec(a.shape, lambda i: (0,) * a.ndim)
    return pl.pallas_call(
        _inproj_kernel,
        grid=(n // tm,),
        in_specs=[row(d), pl.BlockSpec((None, 6, d), lambda i: (i, 0, 0)), full(n1), full(w_big),
                  row(LANES), row(LANES), full(w2), full(b2)],
        out_specs=[row(w) for w in widths],
        out_shape=[jax.ShapeDtypeStruct((n, w), F32) for w in widths],
        compiler_params=_params(("parallel",)),
        name="in_proj",
    )(x, modblk, n1, w_big, cos, sin, w2, b2)


def _pool_kernel(x_ref, w_ref, scale_ref, o_ref):
    t = x_ref.shape[0]
    pad = 32
    n = t + pad
    x = x_ref[...]
    xp = jnp.concatenate([x, jnp.zeros((pad, POOL_WIDTH), F32)], axis=0)
    back = {1: xp}
    for w in (2, 4, 8, 16):
        back[w] = back[w // 2] + pltpu.roll(back[w // 2], w // 2, 0)
    grp = lax.broadcasted_iota(jnp.int32, (1, POOL_WIDTH), 1) // POOL_GROUP_DIM
    tok = lax.broadcasted_iota(jnp.int32, (t, 1), 0)
    wsum = None
    cnt = None
    for g, w in enumerate(POOL_WINDOWS):
        left = w // 2
        right = w - 1 - left
        ws = back[w] if right == 0 else pltpu.roll(back[w], n - right, 0)
        ws = ws[:t]
        c = (jnp.minimum(tok + right + 1, t) - jnp.maximum(tok - left, 0)).astype(F32)
        if wsum is None:
            wsum, cnt = ws, jnp.broadcast_to(c, (t, POOL_WIDTH))
        else:
            wsum = jnp.where(grp == g, ws, wsum)
            cnt = jnp.where(grp == g, c, cnt)
    diff = wsum / cnt - x
    o_ref[...] = _bdot(diff, w_ref[...]) * scale_ref[...]


def _pool(pp, w_bd, scale, seq, first_block, nseq):
    return pl.pallas_call(
        _pool_kernel,
        grid=(nseq,),
        in_specs=[pl.BlockSpec((seq, POOL_WIDTH), lambda b: (first_block + b, 0)),
                  pl.BlockSpec(w_bd.shape, lambda b: (0, 0)),
                  pl.BlockSpec(scale.shape, lambda b: (0, 0))],
        out_specs=pl.BlockSpec((seq, POOL_WIDTH), lambda b: (b, 0)),
        out_shape=jax.ShapeDtypeStruct((nseq * seq, POOL_WIDTH), F32),
        compiler_params=_params(("parallel",)),
        name="pool_mix",
    )(pp, w_bd, scale)


def _gla_rows(reverse):
    out = []
    for s in range(GLA_CHUNK):
        g = s // SUBLANES
        out.append((0, SUBLANES * (g + 1)) if reverse else (SUBLANES * g, GLA_CHUNK))
    return out


GLA_PAIR_ROWS = sum(hi - lo for lo, hi in _gla_rows(False))


def _gla_kernel(q_ref, k_ref, v_ref, la_ref, s0_ref, hexp_ref, tri_ref, bd_ref,
                o_ref, sout_ref, st_ref, b_ref, p_ref, z_ref, *, reverse, nchunk):
    j = pl.program_id(1)

    @pl.when(j == 0)
    def _():
        st_ref[...] = s0_ref[...]

    rows = _gla_rows(reverse)
    ngrp = GLA_CHUNK // SUBLANES
    order = list(range(nchunk - 1, -1, -1) if reverse else range(nchunk))
    chunk = lambda c: slice(c * GLA_CHUNK, (c + 1) * GLA_CHUNK)
    for c in order:
        b_ref[chunk(c), :] = _fdot(tri_ref[...], la_ref[chunk(c), :])
    tio = lax.broadcasted_iota(jnp.int32, (SUBLANES, 1), 0)
    for c in order:
        r0 = c * GLA_CHUNK
        off = c * GLA_PAIR_ROWS
        q = q_ref[chunk(c), :]
        b = b_ref[chunk(c), :]
        for s, (lo, hi) in enumerate(rows):
            dlt = jnp.minimum(b[lo:hi] - b_ref[r0 + s:r0 + s + 1, :], 0.0)
            p = q[lo:hi] * k_ref[r0 + s:r0 + s + 1, :] * jnp.exp(dlt)
            p_ref[off:off + hi - lo, :] = p
            d0 = hi - lo - SUBLANES if reverse else 0
            valid = (tio <= s % SUBLANES) if reverse else (tio >= s % SUBLANES)
            p_ref[off + d0:off + d0 + SUBLANES, :] = jnp.where(valid, p[d0:d0 + SUBLANES], 0.0)
            off += hi - lo
    z_ref[...] = jnp.dot(p_ref[...].astype(BF16), hexp_ref[...], preferred_element_type=F32)
    for c in order:
        r0 = c * GLA_CHUNK
        off = c * GLA_PAIR_ROWS
        oacc = [None] * ngrp
        for g in range(ngrp):
            lo, hi = rows[g * SUBLANES]
            cg = None
            for s in range(g * SUBLANES, (g + 1) * SUBLANES):
                term = z_ref[off:off + hi - lo, :] * v_ref[r0 + s:r0 + s + 1, :]
                cg = term if cg is None else cg + term
                off += hi - lo
            for rg in range(lo // SUBLANES, hi // SUBLANES):
                piece = cg[(rg - lo // SUBLANES) * SUBLANES:(rg - lo // SUBLANES + 1) * SUBLANES]
                oacc[rg] = piece if oacc[rg] is None else oacc[rg] + piece
        o_ref[chunk(c), :] = jnp.concatenate(oacc, axis=0)
    kvs, decay = {}, {}
    for c in order:
        b = b_ref[chunk(c), :]
        blast = b[0:1, :] if reverse else b[GLA_CHUNK - 1:GLA_CHUNK, :]
        ke = k_ref[chunk(c), :] * jnp.exp(blast - b)
        kvs[c] = jnp.dot(v_ref[chunk(c), :].T.astype(BF16), ke.astype(BF16),
                         preferred_element_type=F32) * bd_ref[...]
        decay[c] = jnp.exp(blast)
    st = st_ref[...]
    for c in order:
        qe = (q_ref[chunk(c), :] * jnp.exp(b_ref[chunk(c), :])).astype(BF16)
        o_ref[chunk(c), :] += lax.dot_general(qe, st.astype(BF16), (((1,), (1,)), ((), ())),
                                              preferred_element_type=F32)
        st = st * decay[c] + kvs[c]
    st_ref[...] = st

    @pl.when(j == pl.num_programs(1) - 1)
    def _():
        sout_ref[...] = st_ref[...]


def _gla_consts():
    hd = np.arange(GLA_QK) // GLA_DK
    he = np.arange(GLA_WIDTH) // GLA_DV
    hexp = (hd[:, None] == he[None, :]).astype(np.float32)
    bd = hexp.T.copy()
    t = np.arange(GLA_CHUNK)
    tri_f = (t[None, :] <= t[:, None]).astype(np.float32)
    tri_b = (t[None, :] >= t[:, None]).astype(np.float32)
    return jnp.asarray(hexp, BF16), jnp.asarray(bd, F32), jnp.asarray(tri_f), jnp.asarray(tri_b)


def _gla(gq, gk, gv, la, s0t, consts, seq, first_block, nseq, reverse):
    hexp, bd, tri_f, tri_b = consts
    tri = tri_b if reverse else tri_f
    blk = min(GLA_BLOCK, seq)
    nblk = seq // blk
    if reverse:
        tok = lambda b, j: (first_block + b * nblk + (nblk - 1 - j), 0)
        otok = lambda b, j: (b * nblk + (nblk - 1 - j), 0)
    else:
        tok = lambda b, j: (first_block + b * nblk + j, 0)
        otok = lambda b, j: (b * nblk + j, 0)
    const = lambda a: pl.BlockSpec(a.shape, lambda b, j: (0,) * a.ndim)
    return pl.pallas_call(
        functools.partial(_gla_kernel, reverse=reverse, nchunk=blk // GLA_CHUNK),
        grid=(nseq, nblk),
        in_specs=[pl.BlockSpec((blk, GLA_QK), tok), pl.BlockSpec((blk, GLA_QK), tok),
                  pl.BlockSpec((blk, GLA_WIDTH), tok), pl.BlockSpec((blk, GLA_QK), tok),
                  pl.BlockSpec((None, GLA_WIDTH, GLA_QK), lambda b, j: (b, 0, 0)),
                  const(hexp), const(tri), const(bd)],
        out_specs=[pl.BlockSpec((blk, GLA_WIDTH), otok),
                   pl.BlockSpec((None, GLA_WIDTH, GLA_QK), lambda b, j: (b, 0, 0))],
        out_shape=[jax.ShapeDtypeStruct((nseq * seq, GLA_WIDTH), F32),
                   jax.ShapeDtypeStruct((nseq, GLA_WIDTH, GLA_QK), F32)],
        scratch_shapes=[pltpu.VMEM((GLA_WIDTH, GLA_QK), F32), pltpu.VMEM((blk, GLA_QK), F32),
                        pltpu.VMEM((blk // GLA_CHUNK * GLA_PAIR_ROWS, GLA_QK), F32),
                        pltpu.VMEM((blk // GLA_CHUNK * GLA_PAIR_ROWS, GLA_WIDTH), F32)],
        compiler_params=_params(("parallel", "arbitrary")),
        name="gla_bwd" if reverse else "gla_fwd",
    )(gq, gk, gv, la, s0t, hexp, tri, bd)


def _state_to_blockdiag_t(s):
    b = s.shape[0]
    eye = jnp.eye(GLA_HEADS, dtype=s.dtype)
    return jnp.einsum('bhde,hg->bhegd', s, eye).reshape(b, GLA_WIDTH, GLA_QK)


def _blockdiag_t_to_state(st):
    b = st.shape[0]
    eye = jnp.eye(GLA_HEADS, dtype=st.dtype)
    return jnp.einsum('bhegd,hg->bhde', st.reshape(b, GLA_HEADS, GLA_DV, GLA_HEADS, GLA_DK), eye)


ATTN_SUBBLOCKS = 4


def _stack_heads(q_ref, kv, rows=slice(None)):
    return jnp.concatenate([q_ref[rows, (kv * ATTN_GROUP + r) * HEAD_DIM:(kv * ATTN_GROUP + r + 1) * HEAD_DIM]
                            for r in range(ATTN_GROUP)], axis=0)


def _sink_column(sink_ref, kv, rows):
    return jnp.concatenate([jnp.broadcast_to(sink_ref[kv * ATTN_GROUP + r:kv * ATTN_GROUP + r + 1, 0:1], (rows, 1))
                            for r in range(ATTN_GROUP)], axis=0)


def _qk(q, k):
    return lax.dot_general(q.astype(BF16), k.astype(BF16), (((1,), (1,)), ((), ())), preferred_element_type=F32)


def _ctx_attn_kernel(q_ref, k_ref, v_ref, sink_ref, o_ref, *, seq):
    t = seq
    chains = [(slice(sb * seq, (sb + 1) * seq), kv)
              for sb in range(q_ref.shape[0] // seq) for kv in range(ATTN_KV_HEADS)]
    lanes = lambda kv: slice(kv * HEAD_DIM, (kv + 1) * HEAD_DIM)
    s = [_qk(_stack_heads(q_ref, kv, rows), k_ref[rows, lanes(kv)]) for rows, kv in chains]
    sink = [_sink_column(sink_ref, kv, t) for _, kv in chains]
    m = [jnp.maximum(jnp.max(s[c], axis=-1, keepdims=True), sink[c]) for c in range(len(chains))]
    p = [jnp.exp(s[c] - m[c]) for c in range(len(chains))]
    den = [jnp.sum(p[c], axis=-1, keepdims=True) + jnp.exp(sink[c] - m[c]) for c in range(len(chains))]
    for c, (rows, kv) in enumerate(chains):
        o = _bdot(p[c], v_ref[rows, lanes(kv)]) / den[c]
        for r in range(ATTN_GROUP):
            h = kv * ATTN_GROUP + r
            o_ref[rows, h * HEAD_DIM:(h + 1) * HEAD_DIM] = o[r * t:(r + 1) * t]


def _ctx_attn(aq, ak, av, sink_b, seq, nseq):
    per = ATTN_SUBBLOCKS if nseq % ATTN_SUBBLOCKS == 0 else 1
    tok = lambda w: pl.BlockSpec((per * seq, w), lambda b: (b, 0))
    return pl.pallas_call(
        functools.partial(_ctx_attn_kernel, seq=seq),
        grid=(nseq // per,),
        in_specs=[tok(ATTN_WIDTH), tok(KV_WIDTH), tok(KV_WIDTH), pl.BlockSpec(sink_b.shape, lambda b: (0, 0))],
        out_specs=tok(ATTN_WIDTH),
        out_shape=jax.ShapeDtypeStruct((nseq * seq, ATTN_WIDTH), F32),
        compiler_params=_params(("parallel",)),
        name="ctx_attn",
    )(aq, ak, av, sink_b)


def _lat_attn_kernel(q_ref, k_ref, v_ref, ck_ref, cv_ref, sink_ref, o_ref):
    t = k_ref.shape[0]
    blk = ATTN_BLOCK
    span = 3 * blk
    per = q_ref.shape[0] // blk
    lanes = lambda kv: slice(kv * HEAD_DIM, (kv + 1) * HEAD_DIM)
    chains, starts, inwin = [], [], []
    for sb in range(per):
        i = pl.program_id(1) * per + sb
        start = pl.multiple_of(jnp.clip((i - 1) * blk, 0, t - span), blk)
        qpos = i * blk + lax.broadcasted_iota(jnp.int32, (blk, 1), 0)
        kpos = start + lax.broadcasted_iota(jnp.int32, (1, span), 1)
        win = jnp.concatenate([jnp.abs(qpos - kpos) <= WINDOW] * ATTN_GROUP, axis=0)
        for kv in range(ATTN_KV_HEADS):
            chains.append((slice(sb * blk, (sb + 1) * blk), kv))
            starts.append(start)
            inwin.append(win)
    n = len(chains)
    q = [_stack_heads(q_ref, kv, rows) for rows, kv in chains]
    s_loc = [jnp.where(inwin[c], _qk(q[c], k_ref[pl.ds(starts[c], span), :][:, lanes(chains[c][1])]), NEG_INF)
             for c in range(n)]
    s_ctx = [_qk(q[c], ck_ref[:, lanes(chains[c][1])]) for c in range(n)]
    sink = [_sink_column(sink_ref, kv, blk) for _, kv in chains]
    m = [jnp.maximum(jnp.maximum(jnp.max(s_loc[c], axis=-1, keepdims=True),
                                 jnp.max(s_ctx[c], axis=-1, keepdims=True)), sink[c]) for c in range(n)]
    p_loc = [jnp.exp(s_loc[c] - m[c]) for c in range(n)]
    p_ctx = [jnp.exp(s_ctx[c] - m[c]) for c in range(n)]
    den = [jnp.sum(p_loc[c], axis=-1, keepdims=True) + jnp.sum(p_ctx[c], axis=-1, keepdims=True)
           + jnp.exp(sink[c] - m[c]) for c in range(n)]
    for c, (rows, kv) in enumerate(chains):
        o = (_bdot(p_loc[c], v_ref[pl.ds(starts[c], span), :][:, lanes(kv)])
             + _bdot(p_ctx[c], cv_ref[:, lanes(kv)])) / den[c]
        for r in range(ATTN_GROUP):
            h = kv * ATTN_GROUP + r
            o_ref[rows, h * HEAD_DIM:(h + 1) * HEAD_DIM] = o[r * blk:(r + 1) * blk]


def _lat_attn(aq, ak, av, ck, cv, sink_b, seq, first_tok, nseq):
    qrows = ATTN_SUBBLOCKS * ATTN_BLOCK
    assert seq % qrows == 0 and first_tok % qrows == 0
    nblk = seq // qrows
    fb_q = first_tok // qrows
    fb_s = first_tok // seq
    return pl.pallas_call(
        _lat_attn_kernel,
        grid=(nseq, nblk),
        in_specs=[pl.BlockSpec((qrows, ATTN_WIDTH), lambda b, i: (fb_q + b * nblk + i, 0)),
                  pl.BlockSpec((seq, KV_WIDTH), lambda b, i: (fb_s + b, 0)),
                  pl.BlockSpec((seq, KV_WIDTH), lambda b, i: (fb_s + b, 0)),
                  pl.BlockSpec((None,) + ck.shape[1:], lambda b, i: (b, 0, 0)),
                  pl.BlockSpec((None,) + cv.shape[1:], lambda b, i: (b, 0, 0)),
                  pl.BlockSpec(sink_b.shape, lambda b, i: (0, 0))],
        out_specs=pl.BlockSpec((qrows, ATTN_WIDTH), lambda b, i: (b * nblk + i, 0)),
        out_shape=jax.ShapeDtypeStruct((nseq * seq, ATTN_WIDTH), F32),
        compiler_params=_params(("parallel", "arbitrary")),
        name="lat_attn",
    )(aq, ak, av, ck, cv, sink_b)


def _outproj_kernel(ypc_ref, ypl_ref, ofc_ref, ofl_ref, obc_ref, obl_ref, yac_ref, yal_ref,
                    gg_ref, x_ref, mod_ref, gn_ref, hm_ref, w_ref, n2_ref, xo_ref, h2t_ref, *, ctx_blocks):
    is_ctx = pl.program_id(0) < ctx_blocks
    pick = lambda c_ref, l_ref: jnp.where(is_ctx, c_ref[...], l_ref[...])
    o = pick(ofc_ref, ofl_ref) + pick(obc_ref, obl_ref)
    ms = _fdot(o * o, hm_ref[...])
    gg = gg_ref[...]
    y = o * lax.rsqrt(ms + EPS) * gn_ref[...] * (gg * _sigmoid(gg))
    mix = jnp.concatenate([pick(ypc_ref, ypl_ref), y, pick(yac_ref, yal_ref)], axis=1)
    xn = x_ref[...] + mod_ref[2:3, :] * _bdot(mix, w_ref[...])
    xo_ref[...] = xn
    h2 = _rms(xn) * n2_ref[...] * (1.0 + mod_ref[4:5, :]) + mod_ref[3:4, :]
    h2t_ref[...] = pltpu.bitcast(h2.T.astype(BF16), jnp.uint32)


def _outproj(ypool, o_f, o_b, yattn, gg, x, modblk, gnorm, hmean, w_out, n2):
    n, d = x.shape
    tm = TOKEN_BLOCK
    cb = ypool[0].shape[0] // tm
    lb = ypool[1].shape[0] // tm
    row = lambda w: pl.BlockSpec((tm, w), lambda i: (i, 0))
    ctx = lambda w: pl.BlockSpec((tm, w), lambda i: (jnp.minimum(i, cb - 1), 0))
    lat = lambda w: pl.BlockSpec((tm, w), lambda i: (jnp.clip(i - cb, 0, lb - 1), 0))
    full = lambda a: pl.BlockSpec(a.shape, lambda i: (0,) * a.ndim)
    pair = lambda w: [ctx(w), lat(w)]
    return pl.pallas_call(
        functools.partial(_outproj_kernel, ctx_blocks=cb),
        grid=(n // tm,),
        in_specs=pair(POOL_WIDTH) + pair(GLA_WIDTH) + pair(GLA_WIDTH) + pair(ATTN_WIDTH) + [
            row(GLA_WIDTH), row(d), pl.BlockSpec((None, 6, d), lambda i: (i, 0, 0)),
            full(gnorm), full(hmean), full(w_out), full(n2)],
        out_specs=[row(d), pl.BlockSpec((d // 2, tm), lambda i: (0, i))],
        out_shape=[jax.ShapeDtypeStruct((n, d), F32), jax.ShapeDtypeStruct((d // 2, n), jnp.uint32)],
        compiler_params=_params(("parallel",)),
        name="out_proj",
    )(*ypool, *o_f, *o_b, *yattn, gg, x, modblk, gnorm, hmean, w_out, n2)


def _peer_score_kernel(h2t_ref, wqt_ref, sk_ref, s1_ref, s2_ref):
    qt = jnp.dot(wqt_ref[...], pltpu.bitcast(h2t_ref[...], BF16), preferred_element_type=F32)
    for h in range(PEER_HEADS):
        for p, out in enumerate((s1_ref, s2_ref)):
            r = (2 * h + p) * PEER_HALF
            out[h] = jnp.dot(sk_ref[2 * h + p], qt[r:r + PEER_HALF].astype(BF16), preferred_element_type=F32)


def _peer_scores(h2t, wqt, sk):
    d, n = h2t.shape
    tb = TOKEN_BLOCK
    nkeys = sk.shape[1]
    out = pl.BlockSpec((PEER_HEADS, nkeys, tb), lambda i: (0, 0, i))
    return pl.pallas_call(
        _peer_score_kernel,
        grid=(n // tb,),
        in_specs=[pl.BlockSpec((d, tb), lambda i: (0, i)),
                  pl.BlockSpec(wqt.shape, lambda i: (0, 0)),
                  pl.BlockSpec(sk.shape, lambda i: (0, 0, 0))],
        out_specs=[out, out],
        out_shape=[jax.ShapeDtypeStruct((PEER_HEADS, nkeys, n), F32)] * 2,
        compiler_params=_params(("parallel",)),
        name="peer_scores",
    )(h2t, wqt, sk)


PEER_CANDS = [(a, b) for a in range(PEER_TOPK) for b in range(PEER_TOPK) if (a + 1) * (b + 1) <= PEER_TOPK]
PEER_CAND_ROWS = -(-len(PEER_CANDS) // SUBLANES) * SUBLANES


def _top_values_many(arrays, count, with_rank):
    cur = list(arrays)
    vals = [[] for _ in cur]
    rank = [jnp.full(c.shape, float(count), F32) if w else None for c, w in zip(cur, with_rank)]
    for r in range(count):
        for k in range(len(cur)):
            m = jnp.max(cur[k], axis=0, keepdims=True)
            vals[k].append(m)
            hit = cur[k] == m
            if with_rank[k]:
                rank[k] = jnp.where(hit, float(r), rank[k])
            cur[k] = jnp.where(hit, -jnp.inf, cur[k])
    return vals, rank


def _top_values(cur, count):
    return _top_values_many([cur], count, [False])[0][0]


def _bitonic_merge_desc(a):
    n = len(a)
    j = n // 2
    while j >= 1:
        for i in range(n):
            l = i ^ j
            if l > i:
                a[i], a[l] = jnp.maximum(a[i], a[l]), jnp.minimum(a[i], a[l])
        j //= 2
    return a


def _top16_sorted(s):
    n = PEER_TOPK
    assert s.shape[0] == n * SUBLANES
    a = [s[v * SUBLANES:(v + 1) * SUBLANES, :] for v in range(n)]
    k = 2
    while k <= n:
        j = k // 2
        while j >= 1:
            for i in range(n):
                l = i ^ j
                if l > i:
                    hi, lo = jnp.maximum(a[i], a[l]), jnp.minimum(a[i], a[l])
                    a[i], a[l] = (hi, lo) if (i & k) == 0 else (lo, hi)
            j //= 2
        k *= 2
    shift = SUBLANES // 2
    while shift >= 1:
        b = [pltpu.roll(x, SUBLANES - shift, 0) for x in a]
        a = _bitonic_merge_desc([jnp.maximum(a[v], b[n - 1 - v]) for v in range(n)])
        shift //= 2
    return [x[0:1, :] for x in a]


PEER_DENSE_RANKS = 4
PEER_GATE_TILE = 512
assert all(b < PEER_DENSE_RANKS for a, b in PEER_CANDS if a >= PEER_DENSE_RANKS)


def _peer_gate_kernel(s1_ref, s2_ref, cnt_ref, a_ref, rk_ref, bx_ref, cand_ref):
    width = cand_ref.shape[1]
    ntile = s1_ref.shape[2] // width

    def one_tile(it, carry):
        h = it // ntile
        cols = pl.ds(pl.multiple_of((it % ntile) * width, width), width)
        s1 = s1_ref[h, :, cols]
        s2 = s2_ref[h, :, cols]
        v1 = _top16_sorted(s1)
        v2 = _top16_sorted(s2)
        rk = None
        for b in range(PEER_TOPK):
            above = jnp.where(v2[b] > s2, 1.0, 0.0)
            rk = above if rk is None else rk + above
        cand_ref[...] = jnp.full(cand_ref.shape, -jnp.inf, F32)
        for r, (a, b) in enumerate(PEER_CANDS):
            cand_ref[r:r + 1, :] = v1[a] + v2[b]
        best = _top_values(cand_ref[...], PEER_TOPK)
        z = None
        for val in best:
            e = jnp.exp(val - best[0])
            z = e if z is None else z + e
        tau = best[-1]
        cnt = None
        for b in range(PEER_DENSE_RANKS):
            reach = jnp.where(s1 + v2[b] >= tau, 1.0, 0.0)
            cnt = reach if cnt is None else cnt + reach
        for a in range(PEER_DENSE_RANKS):
            full = None
            for b in range(PEER_TOPK // (a + 1)):
                reach = jnp.where(v1[a] + v2[b] >= tau, 1.0, 0.0)
                full = reach if full is None else full + reach
            cnt = jnp.where(s1 == v1[a], full, cnt)
        cnt_ref[h, :, cols] = cnt
        a_ref[h, :, cols] = jnp.exp(s1 - v1[0]) * (0.5 / z)
        rk_ref[h, :, cols] = pltpu.bitcast(rk.astype(BF16), jnp.uint32)
        bx_ref[h, :, cols] = pltpu.bitcast(jnp.exp(s2 - v2[0]).astype(BF16), jnp.uint32)
        return carry

    lax.fori_loop(0, PEER_HEADS * ntile, one_tile, 0)


def _peer_gates(s1, s2):
    _, nkeys, n = s1.shape
    tb = TOKEN_BLOCK
    blk = pl.BlockSpec((PEER_HEADS, nkeys, tb), lambda i: (0, 0, i))
    half = pl.BlockSpec((PEER_HEADS, nkeys // 2, tb), lambda i: (0, 0, i))
    return pl.pallas_call(
        _peer_gate_kernel,
        grid=(n // tb,),
        in_specs=[blk, blk],
        out_specs=[blk, blk, half, half],
        out_shape=[jax.ShapeDtypeStruct((PEER_HEADS, nkeys, n), F32),
                   jax.ShapeDtypeStruct((PEER_HEADS, nkeys, n), F32),
                   jax.ShapeDtypeStruct((PEER_HEADS, nkeys // 2, n), jnp.uint32),
                   jax.ShapeDtypeStruct((PEER_HEADS, nkeys // 2, n), jnp.uint32)],
        scratch_shapes=[pltpu.VMEM((PEER_CAND_ROWS, PEER_GATE_TILE), F32)],
        compiler_params=_params(("parallel",)),
        name="peer_gates",
    )(s1, s2)


PEER_ROW_GROUP = 4
PEER_TILE_ROWS = 128


def _peer_dense_kernel(h2t_ref, u_ref, vt_ref, cnt_ref, a_ref, rk_ref, bx_ref, x_ref, mod_ref,
                       xo_ref, yt_ref, at_ref, w_ref, *, nkeys):
    e = pl.program_id(1)
    per_step = u_ref.shape[0] // nkeys

    @pl.when(e == 0)
    def _():
        yt_ref[...] = jnp.zeros(yt_ref.shape, F32)

    at_ref[...] = jnp.dot(u_ref[...], pltpu.bitcast(h2t_ref[...], BF16), preferred_element_type=F32)
    first_key = pl.multiple_of(e * per_step, SUBLANES)
    zero = jnp.zeros((PEER_TILE_ROWS, LANES), BF16)
    for i8 in range(0, per_step, SUBLANES):
        for tt in range(at_ref.shape[1] // LANES):
            cols = slice(tt * LANES, (tt + 1) * LANES)
            for jh in range(nkeys // PEER_TILE_ROWS):
                words = slice(jh * PEER_TILE_ROWS // 2, (jh + 1) * PEER_TILE_ROWS // 2)
                for i0 in range(i8, i8 + SUBLANES, PEER_ROW_GROUP):
                    g = [None] * PEER_ROW_GROUP
                    for h in range(PEER_HEADS):
                        rk = pltpu.bitcast(rk_ref[h, words, cols], BF16)
                        bx = pltpu.bitcast(bx_ref[h, words, cols], BF16)
                        cnts = cnt_ref[h, pl.ds(first_key + i8, SUBLANES), cols]
                        arows = a_ref[h, pl.ds(first_key + i8, SUBLANES), cols]
                        for ii in range(PEER_ROW_GROUP):
                            r = i0 - i8 + ii
                            term = (jnp.where(rk < cnts[r:r + 1, :].astype(BF16), bx, zero)
                                    * arows[r:r + 1, :].astype(BF16))
                            g[ii] = term if g[ii] is None else g[ii] + term
                    for ii in range(PEER_ROW_GROUP):
                        r0 = (i0 + ii) * nkeys + jh * PEER_TILE_ROWS
                        rows = slice(r0, r0 + PEER_TILE_ROWS)
                        xb = at_ref[rows, cols].astype(BF16)
                        w_ref[rows, cols] = g[ii] * (xb * (1.0 + lax.erf(xb * (2.0 ** -0.5))))
    yt_ref[...] += jnp.dot(vt_ref[...], w_ref[...], preferred_element_type=F32)

    @pl.when(e == pl.num_programs(1) - 1)
    def _():
        xo_ref[...] = x_ref[...] + mod_ref[5:6, :] * yt_ref[...].T


def _peer_dense(h2t, u, vt, cnt, a, rk, bx, x, modblk):
    n, d = x.shape
    nexp = u.shape[0]
    nkeys = cnt.shape[1]
    tb = TOKEN_BLOCK
    eb = min(PEER_EXPERT_BLOCK, nexp)
    ne = nexp // eb
    assert nkeys == LANES and eb % (SUBLANES * nkeys) == 0 and SUBLANES % PEER_ROW_GROUP == 0 and ne * eb == nexp
    keyed = pl.BlockSpec((PEER_HEADS, nkeys, tb), lambda t, e: (0, 0, t))
    packed = pl.BlockSpec((PEER_HEADS, nkeys // 2, tb), lambda t, e: (0, 0, t))
    return pl.pallas_call(
        functools.partial(_peer_dense_kernel, nkeys=nkeys),
        grid=(n // tb, ne),
        in_specs=[pl.BlockSpec((d // 2, tb), lambda t, e: (0, t)),
                  pl.BlockSpec((eb, d), lambda t, e: (e, 0)),
                  pl.BlockSpec((d, eb), lambda t, e: (0, e)),
                  keyed, keyed, packed, packed,
                  pl.BlockSpec((tb, d), lambda t, e: (t, 0)),
                  pl.BlockSpec((None, 6, d), lambda t, e: (t, 0, 0))],
        out_specs=pl.BlockSpec((tb, d), lambda t, e: (t, 0)),
        out_shape=jax.ShapeDtypeStruct((n, d), F32),
        scratch_shapes=[pltpu.VMEM((d, tb), F32), pltpu.VMEM((eb, tb), F32), pltpu.VMEM((eb, tb), BF16)],
        compiler_params=_params(("parallel", "arbitrary")),
        name="peer_dense",
    )(h2t, u, vt, cnt, a, rk, bx, x, modblk)


def _final_norm_kernel(x_ref, g_ref, o_ref):
    o_ref[...] = _rms(x_ref[...]) * g_ref[...]


def _final_norm(x, g):
    n, d = x.shape
    tm = TOKEN_BLOCK
    return pl.pallas_call(
        _final_norm_kernel,
        grid=(n // tm,),
        in_specs=[pl.BlockSpec((tm, d), lambda i: (i, 0)), pl.BlockSpec((1, d), lambda i: (0, 0))],
        out_specs=pl.BlockSpec((tm, d), lambda i: (i, 0)),
        out_shape=jax.ShapeDtypeStruct((n, d), F32),
        compiler_params=_params(("parallel",)),
        name="final_norm",
    )(x, g.reshape(1, d))


def _rope_swap_columns(width):
    half = HEAD_DIM // 2
    nf = half // 2
    perm = np.zeros(width, np.int32)
    sign = np.zeros(width, np.float32)
    for c in range(width):
        r = c % half
        if r < nf:
            perm[c], sign[c] = c + nf, -1.0
        else:
            perm[c], sign[c] = c - nf, 1.0
    return perm, sign


def _pack_w_in(w_in):
    d = w_in.shape[0]
    o_aq = POOL_WIDTH + 2 * GLA_QK + 2 * GLA_WIDTH + 2 * GLA_GATE_RANK
    o_ak = o_aq + ATTN_WIDTH
    o_av = o_ak + KV_WIDTH
    w_aq = w_in[:, o_aq:o_ak]
    w_ak = w_in[:, o_ak:o_av]
    pq, sq = _rope_swap_columns(ATTN_WIDTH)
    pk, sk = _rope_swap_columns(KV_WIDTH)
    gz = jnp.pad(w_in[:, o_aq - 2 * GLA_GATE_RANK:o_aq], ((0, 0), (0, LANES - 2 * GLA_GATE_RANK)))
    cols = [w_in[:, :o_aq - 2 * GLA_GATE_RANK], gz, w_aq, w_ak, w_in[:, o_av:],
            w_aq[:, pq] * sq[None, :], w_ak[:, pk] * sk[None, :]]
    return jnp.concatenate(cols, axis=1).astype(BF16)


def _block_diag(w):
    g, c, _ = w.shape
    eye = jnp.eye(g, dtype=w.dtype)
    return jnp.einsum('gcd,gh->gchd', w, eye).reshape(g * c, g * c)


def _rope_tables(n_ctx, n_lat_seq, n_lat_batch):
    half = HEAD_DIM // 2
    nf = half // 2
    freqs = ROPE_THETA ** (-jnp.arange(nf, dtype=F32) / nf)
    rows = n_lat_seq // GRID_W
    pos_row = jnp.repeat(jnp.arange(rows), GRID_W).astype(F32)
    pos_col = jnp.tile(jnp.arange(GRID_W), rows).astype(F32)
    ar = pos_row[:, None] * freqs[None, :]
    ac = pos_col[:, None] * freqs[None, :]
    cos = jnp.concatenate([jnp.cos(ar)] * 2 + [jnp.cos(ac)] * 2, axis=1)
    sin = jnp.concatenate([jnp.sin(ar)] * 2 + [jnp.sin(ac)] * 2, axis=1)
    reps = LANES // HEAD_DIM
    cos = jnp.tile(jnp.tile(cos, (1, reps)), (n_lat_batch, 1))
    sin = jnp.tile(jnp.tile(sin, (1, reps)), (n_lat_batch, 1))
    cos = jnp.concatenate([jnp.ones((n_ctx, LANES), F32), cos], axis=0)
    sin = jnp.concatenate([jnp.zeros((n_ctx, LANES), F32), sin], axis=0)
    return cos, sin


def kernel(x_prompt, x_sample, cache_k, cache_v, state_fwd, state_bwd, c, c_ctx, w_ada, b_ada, norm1_g, norm2_g,
           w_in, pool_w, pool_scale, gla_gate_w_f, gla_gate_b_f, gla_gate_w_b, gla_gate_b_b, gla_norm_g,
           attn_sink, w_out, peer_wq, peer_subkeys, peer_u, peer_v, final_norm_g):
    nb, seq, d = x_prompt.shape
    nlb, lseq, _ = x_sample.shape
    depth = w_ada.shape[0]
    n_ctx, n_lat = nb * seq, nlb * lseq
    n = n_ctx + n_lat
    tm = TOKEN_BLOCK
    nkeys = peer_subkeys.shape[3]
    assert d == D_MODEL and n_ctx % tm == 0 and lseq % tm == 0 and n_ctx % lseq == 0
    assert seq % GLA_CHUNK == 0 and lseq % GLA_BLOCK == 0 and lseq % GRID_W == 0 and lseq >= 3 * ATTN_BLOCK
    assert nlb + 1 <= SUBLANES and nkeys % SUBLANES == 0

    x0 = jnp.concatenate([x_prompt.reshape(n_ctx, d), x_sample.reshape(n_lat, d)], axis=0)
    cvec = jnp.zeros((SUBLANES, d), F32).at[0].set(c_ctx).at[1:1 + nlb].set(c)
    mods = _ada_mods(cvec, w_ada, b_ada)
    blk_row = np.concatenate([np.zeros(n_ctx // tm, np.int32),
                              1 + np.repeat(np.arange(nlb, dtype=np.int32), lseq // tm)])
    modblk = mods[:, blk_row, :].reshape(depth, n // tm, 6, d)
    cos, sin = _rope_tables(n_ctx, lseq, nlb)
    gla_consts = _gla_consts()
    hd = np.arange(GLA_WIDTH) // GLA_DV
    hmean = jnp.asarray((hd[:, None] == hd[None, :]).astype(np.float32) / GLA_DV)
    zero_state = jnp.zeros((nb, GLA_WIDTH, GLA_QK), F32)

    def layer(x, lp):
        (mod_l, n1, n2, w_in_l, pool_w_l, pool_scale_l, gw_f, gb_f, gw_b, gb_b, gnorm, sink, w_out_l,
         wq, subk, pu, pv, ck, cv, sf, sb) = lp
        w_big = _pack_w_in(w_in_l)
        w2 = jnp.zeros((LANES, 2 * GLA_QK), F32)
        w2 = w2.at[:GLA_GATE_RANK, :GLA_QK].set(gw_f).at[GLA_GATE_RANK:2 * GLA_GATE_RANK, GLA_QK:].set(gw_b)
        b2 = jnp.concatenate([gb_f, gb_b]).reshape(1, 2 * GLA_QK)
        pp, gq, gk, gv, gg, la_f, la_b, aq, ak, av = _inproj(x, mod_l, n1.reshape(1, d), w_big, cos, sin, w2, b2)

        w_bd = _block_diag(pool_w_l).astype(BF16)
        scale = pool_scale_l.reshape(1, POOL_WIDTH)
        y_pool = (_pool(pp, w_bd, scale, seq, 0, nb), _pool(pp, w_bd, scale, lseq, n_ctx // lseq, nlb))

        lblk = min(GLA_BLOCK, lseq)
        of_c, sf_c = _gla(gq, gk, gv, la_f, zero_state, gla_consts, seq, 0, nb, False)
        ob_c, sb_c = _gla(gq, gk, gv, la_b, zero_state, gla_consts, seq, 0, nb, True)
        of_l, _ = _gla(gq, gk, gv, la_f, _state_to_blockdiag_t(sf), gla_consts, lseq, n_ctx // lblk, nlb, False)
        ob_l, _ = _gla(gq, gk, gv, la_b, _state_to_blockdiag_t(sb), gla_consts, lseq, n_ctx // lblk, nlb, True)

        sink_b = jnp.broadcast_to(sink.reshape(ATTN_HEADS, 1), (ATTN_HEADS, LANES))
        y_attn = (_ctx_attn(aq, ak, av, sink_b, seq, nb), _lat_attn(aq, ak, av, ck, cv, sink_b, lseq, n_ctx, nlb))

        x1, h2t = _outproj(y_pool, (of_c, of_l), (ob_c, ob_l), y_attn, gg, x, mod_l,
                           gnorm.reshape(1, GLA_WIDTH), hmean, w_out_l.astype(BF16), n2.reshape(1, d))

        wqt = wq.T.astype(BF16)
        sk = subk.reshape(2 * PEER_HEADS, nkeys, PEER_HALF).astype(BF16)
        s1, s2 = _peer_scores(h2t, wqt, sk)
        cnt, a, rk, bx = _peer_gates(s1, s2)
        x2 = _peer_dense(h2t, pu.astype(BF16), pv.T.astype(BF16), cnt, a, rk, bx, x1, mod_l)
        return x2, (ak[:n_ctx], av[:n_ctx], _blockdiag_t_to_state(sf_c), _blockdiag_t_to_state(sb_c))

    past = cache_k.shape[2]
    xs = (modblk, norm1_g, norm2_g, w_in, pool_w, pool_scale, gla_gate_w_f, gla_gate_b_f, gla_gate_w_b,
          gla_gate_b_b, gla_norm_g, attn_sink, w_out, peer_wq, peer_subkeys, peer_u, peer_v,
          jnp.swapaxes(cache_k, 0, 1).reshape(depth, nlb, past, KV_WIDTH),
          jnp.swapaxes(cache_v, 0, 1).reshape(depth, nlb, past, KV_WIDTH),
          jnp.swapaxes(state_fwd, 0, 1), jnp.swapaxes(state_bwd, 0, 1))
    x_fin, (ks, vs, sfs, sbs) = lax.scan(layer, x0, xs)

    y = _final_norm(x_fin, final_norm_g)
    y_prompt = y[:n_ctx].reshape(nb, seq, d)
    y_sample = y[n_ctx:].reshape(nlb, lseq, d)
    new_k = jnp.swapaxes(ks.reshape(depth, nb, seq, ATTN_KV_HEADS, HEAD_DIM), 0, 1)
    new_v = jnp.swapaxes(vs.reshape(depth, nb, seq, ATTN_KV_HEADS, HEAD_DIM), 0, 1)
    return (y_prompt, y_sample, new_k, new_v, jnp.swapaxes(sfs, 0, 1), jnp.swapaxes(sbs, 0, 1))
```

```python
import functools
import math

import numpy as np
import jax
import jax.numpy as jnp
from jax import lax
from jax.experimental import pallas as pl
from jax.experimental.pallas import tpu as pltpu

F32 = jnp.float32
BF16 = jnp.bfloat16
HIGHEST = lax.Precision.HIGHEST

D_MODEL = 1024
GRID_W = 64
EPS = 1e-6
POOL_WIDTH = 256
POOL_GROUPS = 4
POOL_GROUP_DIM = 64
POOL_WINDOWS = (2, 4, 8, 16)
GLA_HEADS = 4
GLA_DV = 64
GLA_DK = 32
GLA_QK = GLA_HEADS * GLA_DK
GLA_WIDTH = GLA_HEADS * GLA_DV
GLA_GATE_RANK = 16
GLA_GATE_TAU = 16.0
GLA_CHUNK = 32
ATTN_HEADS = 8
ATTN_KV_HEADS = 2
ATTN_GROUP = 4
HEAD_DIM = 64
ATTN_WIDTH = ATTN_HEADS * HEAD_DIM
KV_WIDTH = ATTN_KV_HEADS * HEAD_DIM
WINDOW = 128
ATTN_BLOCK = 128
ROPE_THETA = 10000.0
NEG_INF = -1e30
PEER_HEADS = 8
PEER_HALF = 64
PEER_TOPK = 16

LANES = 128
SUBLANES = 8
VMEM_LIMIT = 56 * 1024 * 1024

TOKEN_BLOCK = 512
GLA_BLOCK = 512
PEER_EXPERT_BLOCK = 2048

C_POOL, C_GQ, C_GK, C_GV, C_GG, C_GZ, C_AQ, C_AK, C_AV, C_AQS, C_AKS, C_END = (
    0, 256, 384, 512, 768, 1024, 1152, 1664, 1792, 1920, 2432, 2560)


def _params(sem):
    return pltpu.CompilerParams(dimension_semantics=sem, vmem_limit_bytes=VMEM_LIMIT)


def _bdot(a, b):
    return jnp.dot(a.astype(BF16), b.astype(BF16), preferred_element_type=F32)


def _fdot(a, b):
    return jnp.dot(a, b, preferred_element_type=F32, precision=HIGHEST)


def _sigmoid(x):
    return 1.0 / (1.0 + jnp.exp(-x))


def _rms(x):
    return x * lax.rsqrt(jnp.mean(x * x, axis=-1, keepdims=True) + EPS)


def _ada_kernel(c_ref, w_ref, b_ref, o_ref):
    c = c_ref[...]
    o_ref[...] = _fdot(c * _sigmoid(c), w_ref[...]) + b_ref[...]


def _ada_mods(cvec, w_ada, b_ada):
    depth, d, six_d = w_ada.shape
    nj = six_d // d
    return pl.pallas_call(
        _ada_kernel,
        grid=(depth, nj),
        in_specs=[pl.BlockSpec((SUBLANES, d), lambda l, j: (0, 0)),
                  pl.BlockSpec((None, d, d), lambda l, j: (l, 0, j)),
                  pl.BlockSpec((None, 1, d), lambda l, j: (l, 0, j))],
        out_specs=pl.BlockSpec((None, SUBLANES, d), lambda l, j: (l, 0, j)),
        out_shape=jax.ShapeDtypeStruct((depth, SUBLANES, six_d), F32),
        compiler_params=_params(("parallel", "parallel")),
        name="ada_mod",
    )(cvec, w_ada, b_ada.reshape(depth, 1, six_d))


def _inproj_kernel(x_ref, mod_ref, n1_ref, w_ref, cos_ref, sin_ref, w2_ref, b2_ref,
                   pp_ref, gq_ref, gk_ref, gv_ref, gg_ref, laf_ref, lab_ref, aq_ref, ak_ref, av_ref):
    h = _rms(x_ref[...]) * n1_ref[...]
    h = (h * (1.0 + mod_ref[1:2, :]) + mod_ref[0:1, :]).astype(BF16)

    def proj(lo, hi):
        return jnp.dot(h, w_ref[:, lo:hi], preferred_element_type=F32)

    pp_ref[...] = proj(C_POOL, C_GQ)
    gq_ref[...] = proj(C_GQ, C_GK) * (GLA_DK ** -0.5)
    gk_ref[...] = proj(C_GK, C_GV)
    gv_ref[...] = proj(C_GV, C_GG)
    gg_ref[...] = proj(C_GG, C_GZ)
    logit = _fdot(proj(C_GZ, C_AQ), w2_ref[...]) + b2_ref[...]
    la = (jnp.minimum(logit, 0.0) - jnp.log1p(jnp.exp(-jnp.abs(logit)))) * (1.0 / GLA_GATE_TAU)
    laf_ref[...] = la[:, :GLA_QK]
    lab_ref[...] = la[:, GLA_QK:]
    cos = cos_ref[...]
    sin = sin_ref[...]
    cos4 = jnp.concatenate([cos] * (ATTN_WIDTH // LANES), axis=1)
    sin4 = jnp.concatenate([sin] * (ATTN_WIDTH // LANES), axis=1)
    aq_ref[...] = (proj(C_AQ, C_AK) * cos4 + proj(C_AQS, C_AKS) * sin4) * (HEAD_DIM ** -0.5)
    ak_ref[...] = proj(C_AK, C_AV) * cos + proj(C_AKS, C_END) * sin
    av_ref[...] = proj(C_AV, C_AQS)


def _inproj(x, modblk, n1, w_big, cos, sin, w2, b2):
    n, d = x.shape
    tm = TOKEN_BLOCK
    widths = (POOL_WIDTH, GLA_QK, GLA_QK, GLA_WIDTH, GLA_WIDTH, GLA_QK, GLA_QK, ATTN_WIDTH, KV_WIDTH, KV_WIDTH)
    row = lambda w: pl.BlockSpec((tm, w), lambda i: (i, 0))
    full = lambda a: pl.BlockSpec(a.shape, lambda i: (0,) * a.ndim)
    return pl.pallas_call(
        _inproj_kernel,
        grid=(n // tm,),
        in_specs=[row(d), pl.BlockSpec((None, 6, d), lambda i: (i, 0, 0)), full(n1), full(w_big),
                  row(LANES), row(LANES), full(w2), full(b2)],
        out_specs=[row(w) for w in widths],
        out_shape=[jax.ShapeDtypeStruct((n, w), F32) for w in widths],
        compiler_params=_params(("parallel",)),
        name="in_proj",
    )(x, modblk, n1, w_big, cos, sin, w2, b2)


def _pool_kernel(x_ref, w_ref, scale_ref, o_ref):
    t = x_ref.shape[0]
    pad = 32
    n = t + pad
    x = x_ref[...]
    xp = jnp.concatenate([x, jnp.zeros((pad, POOL_WIDTH), F32)], axis=0)
    back = {1: xp}
    for w in (2, 4, 8, 16):
        back[w] = back[w // 2] + pltpu.roll(back[w // 2], w // 2, 0)
    grp = lax.broadcasted_iota(jnp.int32, (1, POOL_WIDTH), 1) // POOL_GROUP_DIM
    tok = lax.broadcasted_iota(jnp.int32, (t, 1), 0)
    wsum = None
    cnt = None
    for g, w in enumerate(POOL_WINDOWS):
        left = w // 2
        right = w - 1 - left
        ws = back[w] if right == 0 else pltpu.roll(back[w], n - right, 0)
        ws = ws[:t]
        c = (jnp.minimum(tok + right + 1, t) - jnp.maximum(tok - left, 0)).astype(F32)
        if wsum is None:
            wsum, cnt = ws, jnp.broadcast_to(c, (t, POOL_WIDTH))
        else:
            wsum = jnp.where(grp == g, ws, wsum)
            cnt = jnp.where(grp == g, c, cnt)
    diff = wsum / cnt - x
    o_ref[...] = _bdot(diff, w_ref[...]) * scale_ref[...]


def _pool(pp, w_bd, scale, seq, first_block, nseq):
    return pl.pallas_call(
        _pool_kernel,
        grid=(nseq,),
        in_specs=[pl.BlockSpec((seq, POOL_WIDTH), lambda b: (first_block + b, 0)),
                  pl.BlockSpec(w_bd.shape, lambda b: (0, 0)),
                  pl.BlockSpec(scale.shape, lambda b: (0, 0))],
        out_specs=pl.BlockSpec((seq, POOL_WIDTH), lambda b: (b, 0)),
        out_shape=jax.ShapeDtypeStruct((nseq * seq, POOL_WIDTH), F32),
        compiler_params=_params(("parallel",)),
        name="pool_mix",
    )(pp, w_bd, scale)


def _gla_rows(reverse):
    out = []
    for s in range(GLA_CHUNK):
        g = s // SUBLANES
        out.append((0, SUBLANES * (g + 1)) if reverse else (SUBLANES * g, GLA_CHUNK))
    return out


GLA_PAIR_ROWS = sum(hi - lo for lo, hi in _gla_rows(False))


def _gla_kernel(q_ref, k_ref, v_ref, la_ref, s0_ref, hexp_ref, tri_ref, bd_ref,
                o_ref, sout_ref, st_ref, b_ref, p_ref, z_ref, *, reverse, nchunk):
    j = pl.program_id(1)

    @pl.when(j == 0)
    def _():
        st_ref[...] = s0_ref[...]

    rows = _gla_rows(reverse)
    ngrp = GLA_CHUNK // SUBLANES
    order = list(range(nchunk - 1, -1, -1) if reverse else range(nchunk))
    chunk = lambda c: slice(c * GLA_CHUNK, (c + 1) * GLA_CHUNK)
    for c in order:
        b_ref[chunk(c), :] = _fdot(tri_ref[...], la_ref[chunk(c), :])
    tio = lax.broadcasted_iota(jnp.int32, (SUBLANES, 1), 0)
    for c in order:
        r0 = c * GLA_CHUNK
        off = c * GLA_PAIR_ROWS
        q = q_ref[chunk(c), :]
        b = b_ref[chunk(c), :]
        for s, (lo, hi) in enumerate(rows):
            dlt = jnp.minimum(b[lo:hi] - b_ref[r0 + s:r0 + s + 1, :], 0.0)
            p = q[lo:hi] * k_ref[r0 + s:r0 + s + 1, :] * jnp.exp(dlt)
            p_ref[off:off + hi - lo, :] = p
            d0 = hi - lo - SUBLANES if reverse else 0
            valid = (tio <= s % SUBLANES) if reverse else (tio >= s % SUBLANES)
            p_ref[off + d0:off + d0 + SUBLANES, :] = jnp.where(valid, p[d0:d0 + SUBLANES], 0.0)
            off += hi - lo
    z_ref[...] = jnp.dot(p_ref[...].astype(BF16), hexp_ref[...], preferred_element_type=F32)
    for c in order:
        r0 = c * GLA_CHUNK
        off = c * GLA_PAIR_ROWS
        oacc = [None] * ngrp
        for g in range(ngrp):
            lo, hi = rows[g * SUBLANES]
            cg = None
            for s in range(g * SUBLANES, (g + 1) * SUBLANES):
                term = z_ref[off:off + hi - lo, :] * v_ref[r0 + s:r0 + s + 1, :]
                cg = term if cg is None else cg + term
                off += hi - lo
            for rg in range(lo // SUBLANES, hi // SUBLANES):
                piece = cg[(rg - lo // SUBLANES) * SUBLANES:(rg - lo // SUBLANES + 1) * SUBLANES]
                oacc[rg] = piece if oacc[rg] is None else oacc[rg] + piece
        o_ref[chunk(c), :] = jnp.concatenate(oacc, axis=0)
    kvs, decay = {}, {}
    for c in order:
        b = b_ref[chunk(c), :]
        blast = b[0:1, :] if reverse else b[GLA_CHUNK - 1:GLA_CHUNK, :]
        ke = k_ref[chunk(c), :] * jnp.exp(blast - b)
        kvs[c] = jnp.dot(v_ref[chunk(c), :].T.astype(BF16), ke.astype(BF16),
                         preferred_element_type=F32) * bd_ref[...]
        decay[c] = jnp.exp(blast)
    st = st_ref[...]
    for c in order:
        qe = (q_ref[chunk(c), :] * jnp.exp(b_ref[chunk(c), :])).astype(BF16)
        o_ref[chunk(c), :] += lax.dot_general(qe, st.astype(BF16), (((1,), (1,)), ((), ())),
                                              preferred_element_type=F32)
        st = st * decay[c] + kvs[c]
    st_ref[...] = st

    @pl.when(j == pl.num_programs(1) - 1)
    def _():
        sout_ref[...] = st_ref[...]


def _gla_consts():
    hd = np.arange(GLA_QK) // GLA_DK
    he = np.arange(GLA_WIDTH) // GLA_DV
    hexp = (hd[:, None] == he[None, :]).astype(np.float32)
    bd = hexp.T.copy()
    t = np.arange(GLA_CHUNK)
    tri_f = (t[None, :] <= t[:, None]).astype(np.float32)
    tri_b = (t[None, :] >= t[:, None]).astype(np.float32)
    return jnp.asarray(hexp, BF16), jnp.asarray(bd, F32), jnp.asarray(tri_f), jnp.asarray(tri_b)


def _gla(gq, gk, gv, la, s0t, consts, seq, first_block, nseq, reverse):
    hexp, bd, tri_f, tri_b = consts
    tri = tri_b if reverse else tri_f
    blk = min(GLA_BLOCK, seq)
    nblk = seq // blk
    if reverse:
        tok = lambda b, j: (first_block + b * nblk + (nblk - 1 - j), 0)
        otok = lambda b, j: (b * nblk + (nblk - 1 - j), 0)
    else:
        tok = lambda b, j: (first_block + b * nblk + j, 0)
        otok = lambda b, j: (b * nblk + j, 0)
    const = lambda a: pl.BlockSpec(a.shape, lambda b, j: (0,) * a.ndim)
    return pl.pallas_call(
        functools.partial(_gla_kernel, reverse=reverse, nchunk=blk // GLA_CHUNK),
        grid=(nseq, nblk),
        in_specs=[pl.BlockSpec((blk, GLA_QK), tok), pl.BlockSpec((blk, GLA_QK), tok),
                  pl.BlockSpec((blk, GLA_WIDTH), tok), pl.BlockSpec((blk, GLA_QK), tok),
                  pl.BlockSpec((None, GLA_WIDTH, GLA_QK), lambda b, j: (b, 0, 0)),
                  const(hexp), const(tri), const(bd)],
        out_specs=[pl.BlockSpec((blk, GLA_WIDTH), otok),
                   pl.BlockSpec((None, GLA_WIDTH, GLA_QK), lambda b, j: (b, 0, 0))],
        out_shape=[jax.ShapeDtypeStruct((nseq * seq, GLA_WIDTH), F32),
                   jax.ShapeDtypeStruct((nseq, GLA_WIDTH, GLA_QK), F32)],
        scratch_shapes=[pltpu.VMEM((GLA_WIDTH, GLA_QK), F32), pltpu.VMEM((blk, GLA_QK), F32),
                        pltpu.VMEM((blk // GLA_CHUNK * GLA_PAIR_ROWS, GLA_QK), F32),
                        pltpu.VMEM((blk // GLA_CHUNK * GLA_PAIR_ROWS, GLA_WIDTH), F32)],
        compiler_params=_params(("parallel", "arbitrary")),
        name="gla_bwd" if reverse else "gla_fwd",
    )(gq, gk, gv, la, s0t, hexp, tri, bd)


def _state_to_blockdiag_t(s):
    b = s.shape[0]
    eye = jnp.eye(GLA_HEADS, dtype=s.dtype)
    return jnp.einsum('bhde,hg->bhegd', s, eye).reshape(b, GLA_WIDTH, GLA_QK)


def _blockdiag_t_to_state(st):
    b = st.shape[0]
    eye = jnp.eye(GLA_HEADS, dtype=st.dtype)
    return jnp.einsum('bhegd,hg->bhde', st.reshape(b, GLA_HEADS, GLA_DV, GLA_HEADS, GLA_DK), eye)


ATTN_SUBBLOCKS = 4


def _stack_heads(q_ref, kv, rows=slice(None)):
    return jnp.concatenate([q_ref[rows, (kv * ATTN_GROUP + r) * HEAD_DIM:(kv * ATTN_GROUP + r + 1) * HEAD_DIM]
                            for r in range(ATTN_GROUP)], axis=0)


def _sink_column(sink_ref, kv, rows):
    return jnp.concatenate([jnp.broadcast_to(sink_ref[kv * ATTN_GROUP + r:kv * ATTN_GROUP + r + 1, 0:1], (rows, 1))
                            for r in range(ATTN_GROUP)], axis=0)


def _qk(q, k):
    return lax.dot_general(q.astype(BF16), k.astype(BF16), (((1,), (1,)), ((), ())), preferred_element_type=F32)


def _ctx_attn_kernel(q_ref, k_ref, v_ref, sink_ref, o_ref, *, seq):
    t = seq
    chains = [(slice(sb * seq, (sb + 1) * seq), kv)
              for sb in range(q_ref.shape[0] // seq) for kv in range(ATTN_KV_HEADS)]
    lanes = lambda kv: slice(kv * HEAD_DIM, (kv + 1) * HEAD_DIM)
    s = [_qk(_stack_heads(q_ref, kv, rows), k_ref[rows, lanes(kv)]) for rows, kv in chains]
    sink = [_sink_column(sink_ref, kv, t) for _, kv in chains]
    m = [jnp.maximum(jnp.max(s[c], axis=-1, keepdims=True), sink[c]) for c in range(len(chains))]
    p = [jnp.exp(s[c] - m[c]) for c in range(len(chains))]
    den = [jnp.sum(p[c], axis=-1, keepdims=True) + jnp.exp(sink[c] - m[c]) for c in range(len(chains))]
    for c, (rows, kv) in enumerate(chains):
        o = _bdot(p[c], v_ref[rows, lanes(kv)]) / den[c]
        for r in range(ATTN_GROUP):
            h = kv * ATTN_GROUP + r
            o_ref[rows, h * HEAD_DIM:(h + 1) * HEAD_DIM] = o[r * t:(r + 1) * t]


def _ctx_attn(aq, ak, av, sink_b, seq, nseq):
    per = ATTN_SUBBLOCKS if nseq % ATTN_SUBBLOCKS == 0 else 1
    tok = lambda w: pl.BlockSpec((per * seq, w), lambda b: (b, 0))
    return pl.pallas_call(
        functools.partial(_ctx_attn_kernel, seq=seq),
        grid=(nseq // per,),
        in_specs=[tok(ATTN_WIDTH), tok(KV_WIDTH), tok(KV_WIDTH), pl.BlockSpec(sink_b.shape, lambda b: (0, 0))],
        out_specs=tok(ATTN_WIDTH),
        out_shape=jax.ShapeDtypeStruct((nseq * seq, ATTN_WIDTH), F32),
        compiler_params=_params(("parallel",)),
        name="ctx_attn",
    )(aq, ak, av, sink_b)


def _lat_attn_kernel(q_ref, k_ref, v_ref, ck_ref, cv_ref, sink_ref, o_ref):
    t = k_ref.shape[0]
    blk = ATTN_BLOCK
    span = 3 * blk
    per = q_ref.shape[0] // blk
    lanes = lambda kv: slice(kv * HEAD_DIM, (kv + 1) * HEAD_DIM)
    chains, starts, inwin = [], [], []
    for sb in range(per):
        i = pl.program_id(1) * per + sb
        start = pl.multiple_of(jnp.clip((i - 1) * blk, 0, t - span), blk)
        qpos = i * blk + lax.broadcasted_iota(jnp.int32, (blk, 1), 0)
        kpos = start + lax.broadcasted_iota(jnp.int32, (1, span), 1)
        win = jnp.concatenate([jnp.abs(qpos - kpos) <= WINDOW] * ATTN_GROUP, axis=0)
        for kv in range(ATTN_KV_HEADS):
            chains.append((slice(sb * blk, (sb + 1) * blk), kv))
            starts.append(start)
            inwin.append(win)
    n = len(chains)
    q = [_stack_heads(q_ref, kv, rows) for rows, kv in chains]
    s_loc = [jnp.where(inwin[c], _qk(q[c], k_ref[pl.ds(starts[c], span), :][:, lanes(chains[c][1])]), NEG_INF)
             for c in range(n)]
    s_ctx = [_qk(q[c], ck_ref[:, lanes(chains[c][1])]) for c in range(n)]
    sink = [_sink_column(sink_ref, kv, blk) for _, kv in chains]
    m = [jnp.maximum(jnp.maximum(jnp.max(s_loc[c], axis=-1, keepdims=True),
                                 jnp.max(s_ctx[c], axis=-1, keepdims=True)), sink[c]) for c in range(n)]
    p_loc = [jnp.exp(s_loc[c] - m[c]) for c in range(n)]
    p_ctx = [jnp.exp(s_ctx[c] - m[c]) for c in range(n)]
    den = [jnp.sum(p_loc[c], axis=-1, keepdims=True) + jnp.sum(p_ctx[c], axis=-1, keepdims=True)
           + jnp.exp(sink[c] - m[c]) for c in range(n)]
    for c, (rows, kv) in enumerate(chains):
        o = (_bdot(p_loc[c], v_ref[pl.ds(starts[c], span), :][:, lanes(kv)])
             + _bdot(p_ctx[c], cv_ref[:, lanes(kv)])) / den[c]
        for r in range(ATTN_GROUP):
            h = kv * ATTN_GROUP + r
            o_ref[rows, h * HEAD_DIM:(h + 1) * HEAD_DIM] = o[r * blk:(r + 1) * blk]


def _lat_attn(aq, ak, av, ck, cv, sink_b, seq, first_tok, nseq):
    qrows = ATTN_SUBBLOCKS * ATTN_BLOCK
    assert seq % qrows == 0 and first_tok % qrows == 0
    nblk = seq // qrows
    fb_q = first_tok // qrows
    fb_s = first_tok // seq
    return pl.pallas_call(
        _lat_attn_kernel,
        grid=(nseq, nblk),
        in_specs=[pl.BlockSpec((qrows, ATTN_WIDTH), lambda b, i: (fb_q + b * nblk + i, 0)),
                  pl.BlockSpec((seq, KV_WIDTH), lambda b, i: (fb_s + b, 0)),
                  pl.BlockSpec((seq, KV_WIDTH), lambda b, i: (fb_s + b, 0)),
                  pl.BlockSpec((None,) + ck.shape[1:], lambda b, i: (b, 0, 0)),
                  pl.BlockSpec((None,) + cv.shape[1:], lambda b, i: (b, 0, 0)),
                  pl.BlockSpec(sink_b.shape, lambda b, i: (0, 0))],
        out_specs=pl.BlockSpec((qrows, ATTN_WIDTH), lambda b, i: (b * nblk + i, 0)),
        out_shape=jax.ShapeDtypeStruct((nseq * seq, ATTN_WIDTH), F32),
        compiler_params=_params(("parallel", "arbitrary")),
        name="lat_attn",
    )(aq, ak, av, ck, cv, sink_b)


def _outproj_kernel(ypc_ref, ypl_ref, ofc_ref, ofl_ref, obc_ref, obl_ref, yac_ref, yal_ref,
                    gg_ref, x_ref, mod_ref, gn_ref, hm_ref, w_ref, n2_ref, xo_ref, h2t_ref, *, ctx_blocks):
    is_ctx = pl.program_id(0) < ctx_blocks
    pick = lambda c_ref, l_ref: jnp.where(is_ctx, c_ref[...], l_ref[...])
    o = pick(ofc_ref, ofl_ref) + pick(obc_ref, obl_ref)
    ms = _fdot(o * o, hm_ref[...])
    gg = gg_ref[...]
    y = o * lax.rsqrt(ms + EPS) * gn_ref[...] * (gg * _sigmoid(gg))
    mix = jnp.concatenate([pick(ypc_ref, ypl_ref), y, pick(yac_ref, yal_ref)], axis=1)
    xn = x_ref[...] + mod_ref[2:3, :] * _bdot(mix, w_ref[...])
    xo_ref[...] = xn
    h2 = _rms(xn) * n2_ref[...] * (1.0 + mod_ref[4:5, :]) + mod_ref[3:4, :]
    h2t_ref[...] = pltpu.bitcast(h2.T.astype(BF16), jnp.uint32)


def _outproj(ypool, o_f, o_b, yattn, gg, x, modblk, gnorm, hmean, w_out, n2):
    n, d = x.shape
    tm = TOKEN_BLOCK
    cb = ypool[0].shape[0] // tm
    lb = ypool[1].shape[0] // tm
    row = lambda w: pl.BlockSpec((tm, w), lambda i: (i, 0))
    ctx = lambda w: pl.BlockSpec((tm, w), lambda i: (jnp.minimum(i, cb - 1), 0))
    lat = lambda w: pl.BlockSpec((tm, w), lambda i: (jnp.clip(i - cb, 0, lb - 1), 0))
    full = lambda a: pl.BlockSpec(a.shape, lambda i: (0,) * a.ndim)
    pair = lambda w: [ctx(w), lat(w)]
    return pl.pallas_call(
        functools.partial(_outproj_kernel, ctx_blocks=cb),
        grid=(n // tm,),
        in_specs=pair(POOL_WIDTH) + pair(GLA_WIDTH) + pair(GLA_WIDTH) + pair(ATTN_WIDTH) + [
            row(GLA_WIDTH), row(d), pl.BlockSpec((None, 6, d), lambda i: (i, 0, 0)),
            full(gnorm), full(hmean), full(w_out), full(n2)],
        out_specs=[row(d), pl.BlockSpec((d // 2, tm), lambda i: (0, i))],
        out_shape=[jax.ShapeDtypeStruct((n, d), F32), jax.ShapeDtypeStruct((d // 2, n), jnp.uint32)],
        compiler_params=_params(("parallel",)),
        name="out_proj",
    )(*ypool, *o_f, *o_b, *yattn, gg, x, modblk, gnorm, hmean, w_out, n2)


def _peer_score_kernel(h2t_ref, wqt_ref, sk_ref, s1_ref, s2_ref):
    qt = jnp.dot(wqt_ref[...], pltpu.bitcast(h2t_ref[...], BF16), preferred_element_type=F32)
    for h in range(PEER_HEADS):
        for p, out in enumerate((s1_ref, s2_ref)):
            r = (2 * h + p) * PEER_HALF
            out[h] = jnp.dot(sk_ref[2 * h + p], qt[r:r + PEER_HALF].astype(BF16), preferred_element_type=F32)


def _peer_scores(h2t, wqt, sk):
    d, n = h2t.shape
    tb = TOKEN_BLOCK
    nkeys = sk.shape[1]
    out = pl.BlockSpec((PEER_HEADS, nkeys, tb), lambda i: (0, 0, i))
    return pl.pallas_call(
        _peer_score_kernel,
        grid=(n // tb,),
        in_specs=[pl.BlockSpec((d, tb), lambda i: (0, i)),
                  pl.BlockSpec(wqt.shape, lambda i: (0, 0)),
                  pl.BlockSpec(sk.shape, lambda i: (0, 0, 0))],
        out_specs=[out, out],
        out_shape=[jax.ShapeDtypeStruct((PEER_HEADS, nkeys, n), F32)] * 2,
        compiler_params=_params(("parallel",)),
        name="peer_scores",
    )(h2t, wqt, sk)


PEER_CANDS = [(a, b) for a in range(PEER_TOPK) for b in range(PEER_TOPK) if (a + 1) * (b + 1) <= PEER_TOPK]
PEER_CAND_ROWS = -(-len(PEER_CANDS) // SUBLANES) * SUBLANES


def _top_values_many(arrays, count, with_rank):
    cur = list(arrays)
    vals = [[] for _ in cur]
    rank = [jnp.full(c.shape, float(count), F32) if w else None for c, w in zip(cur, with_rank)]
    for r in range(count):
        for k in range(len(cur)):
            m = jnp.max(cur[k], axis=0, keepdims=True)
            vals[k].append(m)
            hit = cur[k] == m
            if with_rank[k]:
                rank[k] = jnp.where(hit, float(r), rank[k])
            cur[k] = jnp.where(hit, -jnp.inf, cur[k])
    return vals, rank


def _top_values(cur, count):
    return _top_values_many([cur], count, [False])[0][0]


def _count_above(sorted_rows, x):
    t = sorted_rows
    assert len(t) == 16
    g1 = t[7] > x
    g2 = jnp.where(g1, t[11], t[3]) > x
    g3 = jnp.where(g1, jnp.where(g2, t[13], t[9]), jnp.where(g2, t[5], t[1])) > x
    lo4 = jnp.where(g2, jnp.where(g3, t[6], t[4]), jnp.where(g3, t[2], t[0]))
    hi4 = jnp.where(g2, jnp.where(g3, t[14], t[12]), jnp.where(g3, t[10], t[8]))
    g4 = jnp.where(g1, hi4, lo4) > x
    g5 = t[15] > x
    one = lambda g, v: jnp.where(g, v, 0.0)
    return one(g1, 8.0) + one(g2, 4.0) + one(g3, 2.0) + one(g4, 1.0) + one(g5, 1.0)


def _bitonic_merge_desc(a):
    n = len(a)
    j = n // 2
    while j >= 1:
        for i in range(n):
            l = i ^ j
            if l > i:
                a[i], a[l] = jnp.maximum(a[i], a[l]), jnp.minimum(a[i], a[l])
        j //= 2
    return a


def _top16_sorted(s):
    n = PEER_TOPK
    assert s.shape[0] == n * SUBLANES
    a = [s[v * SUBLANES:(v + 1) * SUBLANES, :] for v in range(n)]
    k = 2
    while k <= n:
        j = k // 2
        while j >= 1:
            for i in range(n):
                l = i ^ j
                if l > i:
                    hi, lo = jnp.maximum(a[i], a[l]), jnp.minimum(a[i], a[l])
                    a[i], a[l] = (hi, lo) if (i & k) == 0 else (lo, hi)
            j //= 2
        k *= 2
    shift = SUBLANES // 2
    while shift >= 1:
        b = [pltpu.roll(x, SUBLANES - shift, 0) for x in a]
        a = _bitonic_merge_desc([jnp.maximum(a[v], b[n - 1 - v]) for v in range(n)])
        shift //= 2
    return [x[0:1, :] for x in a]


PEER_DENSE_RANKS = 4
PEER_GATE_TILE = 512
assert all(b < PEER_DENSE_RANKS for a, b in PEER_CANDS if a >= PEER_DENSE_RANKS)


def _peer_gate_kernel(s1_ref, s2_ref, cnt_ref, a_ref, rk_ref, bx_ref, cand_ref):
    width = cand_ref.shape[1]
    ntile = s1_ref.shape[2] // width

    def one_tile(it, carry):
        h = it // ntile
        cols = pl.ds(pl.multiple_of((it % ntile) * width, width), width)
        s1 = s1_ref[h, :, cols]
        s2 = s2_ref[h, :, cols]
        v1 = _top16_sorted(s1)
        v2 = _top16_sorted(s2)
        rk = _count_above(v2, s2)
        cand_ref[...] = jnp.full(cand_ref.shape, -jnp.inf, F32)
        for r, (a, b) in enumerate(PEER_CANDS):
            cand_ref[r:r + 1, :] = v1[a] + v2[b]
        best = _top_values(cand_ref[...], PEER_TOPK)
        z = None
        for val in best:
            e = jnp.exp(val - best[0])
            z = e if z is None else z + e
        tau = best[-1]
        cnt = None
        for b in range(PEER_DENSE_RANKS):
            reach = jnp.where(s1 + v2[b] >= tau, 1.0, 0.0)
            cnt = reach if cnt is None else cnt + reach
        for a in range(PEER_DENSE_RANKS):
            full = None
            for b in range(PEER_TOPK // (a + 1)):
                reach = jnp.where(v1[a] + v2[b] >= tau, 1.0, 0.0)
                full = reach if full is None else full + reach
            cnt = jnp.where(s1 == v1[a], full, cnt)
        cnt_ref[h, :, cols] = cnt
        a_ref[h, :, cols] = jnp.exp(s1 - v1[0]) * (0.5 / z)
        rk_ref[h, :, cols] = pltpu.bitcast(rk.astype(BF16), jnp.uint32)
        bx_ref[h, :, cols] = pltpu.bitcast(jnp.exp(s2 - v2[0]).astype(BF16), jnp.uint32)
        return carry

    lax.fori_loop(0, PEER_HEADS * ntile, one_tile, 0)


def _peer_gates(s1, s2):
    _, nkeys, n = s1.shape
    tb = TOKEN_BLOCK
    blk = pl.BlockSpec((PEER_HEADS, nkeys, tb), lambda i: (0, 0, i))
    half = pl.BlockSpec((PEER_HEADS, nkeys // 2, tb), lambda i: (0, 0, i))
    return pl.pallas_call(
        _peer_gate_kernel,
        grid=(n // tb,),
        in_specs=[blk, blk],
        out_specs=[blk, blk, half, half],
        out_shape=[jax.ShapeDtypeStruct((PEER_HEADS, nkeys, n), F32),
                   jax.ShapeDtypeStruct((PEER_HEADS, nkeys, n), F32),
                   jax.ShapeDtypeStruct((PEER_HEADS, nkeys // 2, n), jnp.uint32),
                   jax.ShapeDtypeStruct((PEER_HEADS, nkeys // 2, n), jnp.uint32)],
        scratch_shapes=[pltpu.VMEM((PEER_CAND_ROWS, PEER_GATE_TILE), F32)],
        compiler_params=_params(("parallel",)),
        name="peer_gates",
    )(s1, s2)


PEER_ROW_GROUP = 4
PEER_TILE_ROWS = 128


def _peer_dense_kernel(h2t_ref, u_ref, vt_ref, cnt_ref, a_ref, rk_ref, bx_ref, x_ref, mod_ref,
                       xo_ref, yt_ref, at_ref, w_ref, *, nkeys):
    e = pl.program_id(1)
    per_step = u_ref.shape[0] // nkeys

    @pl.when(e == 0)
    def _():
        yt_ref[...] = jnp.zeros(yt_ref.shape, F32)

    at_ref[...] = jnp.dot(u_ref[...], pltpu.bitcast(h2t_ref[...], BF16), preferred_element_type=F32)
    first_key = pl.multiple_of(e * per_step, SUBLANES)
    zero = jnp.zeros((PEER_TILE_ROWS, LANES), BF16)
    for i8 in range(0, per_step, SUBLANES):
        for tt in range(at_ref.shape[1] // LANES):
            cols = slice(tt * LANES, (tt + 1) * LANES)
            for jh in range(nkeys // PEER_TILE_ROWS):
                words = slice(jh * PEER_TILE_ROWS // 2, (jh + 1) * PEER_TILE_ROWS // 2)
                for i0 in range(i8, i8 + SUBLANES, PEER_ROW_GROUP):
                    g = [None] * PEER_ROW_GROUP
                    for h in range(PEER_HEADS):
                        rk = pltpu.bitcast(rk_ref[h, words, cols], BF16)
                        bx = pltpu.bitcast(bx_ref[h, words, cols], BF16)
                        cnts = cnt_ref[h, pl.ds(first_key + i8, SUBLANES), cols]
                        arows = a_ref[h, pl.ds(first_key + i8, SUBLANES), cols]
                        for ii in range(PEER_ROW_GROUP):
                            r = i0 - i8 + ii
                            term = (jnp.where(rk < cnts[r:r + 1, :].astype(BF16), bx, zero)
                                    * arows[r:r + 1, :].astype(BF16))
                            g[ii] = term if g[ii] is None else g[ii] + term
                    for ii in range(PEER_ROW_GROUP):
                        r0 = (i0 + ii) * nkeys + jh * PEER_TILE_ROWS
                        rows = slice(r0, r0 + PEER_TILE_ROWS)
                        xb = at_ref[rows, cols].astype(BF16)
                        w_ref[rows, cols] = g[ii] * (xb * (1.0 + lax.erf(xb * (2.0 ** -0.5))))
    yt_ref[...] += jnp.dot(vt_ref[...], w_ref[...], preferred_element_type=F32)

    @pl.when(e == pl.num_programs(1) - 1)
    def _():
        xo_ref[...] = x_ref[...] + mod_ref[5:6, :] * yt_ref[...].T


def _peer_dense(h2t, u, vt, cnt, a, rk, bx, x, modblk):
    n, d = x.shape
    nexp = u.shape[0]
    nkeys = cnt.shape[1]
    tb = TOKEN_BLOCK
    eb = min(PEER_EXPERT_BLOCK, nexp)
    ne = nexp // eb
    assert nkeys == LANES and eb % (SUBLANES * nkeys) == 0 and SUBLANES % PEER_ROW_GROUP == 0 and ne * eb == nexp
    keyed = pl.BlockSpec((PEER_HEADS, nkeys, tb), lambda t, e: (0, 0, t))
    packed = pl.BlockSpec((PEER_HEADS, nkeys // 2, tb), lambda t, e: (0, 0, t))
    return pl.pallas_call(
        functools.partial(_peer_dense_kernel, nkeys=nkeys),
        grid=(n // tb, ne),
        in_specs=[pl.BlockSpec((d // 2, tb), lambda t, e: (0, t)),
                  pl.BlockSpec((eb, d), lambda t, e: (e, 0)),
                  pl.BlockSpec((d, eb), lambda t, e: (0, e)),
                  keyed, keyed, packed, packed,
                  pl.BlockSpec((tb, d), lambda t, e: (t, 0)),
                  pl.BlockSpec((None, 6, d), lambda t, e: (t, 0, 0))],
        out_specs=pl.BlockSpec((tb, d), lambda t, e: (t, 0)),
        out_shape=jax.ShapeDtypeStruct((n, d), F32),
        scratch_shapes=[pltpu.VMEM((d, tb), F32), pltpu.VMEM((eb, tb), F32), pltpu.VMEM((eb, tb), BF16)],
        compiler_params=_params(("parallel", "arbitrary")),
        name="peer_dense",
    )(h2t, u, vt, cnt, a, rk, bx, x, modblk)


def _final_norm_kernel(x_ref, g_ref, o_ref):
    o_ref[...] = _rms(x_ref[...]) * g_ref[...]


def _final_norm(x, g):
    n, d = x.shape
    tm = TOKEN_BLOCK
    return pl.pallas_call(
        _final_norm_kernel,
        grid=(n // tm,),
        in_specs=[pl.BlockSpec((tm, d), lambda i: (i, 0)), pl.BlockSpec((1, d), lambda i: (0, 0))],
        out_specs=pl.BlockSpec((tm, d), lambda i: (i, 0)),
        out_shape=jax.ShapeDtypeStruct((n, d), F32),
        compiler_params=_params(("parallel",)),
        name="final_norm",
    )(x, g.reshape(1, d))


def _rope_swap_columns(width):
    half = HEAD_DIM // 2
    nf = half // 2
    perm = np.zeros(width, np.int32)
    sign = np.zeros(width, np.float32)
    for c in range(width):
        r = c % half
        if r < nf:
            perm[c], sign[c] = c + nf, -1.0
        else:
            perm[c], sign[c] = c - nf, 1.0
    return perm, sign


def _pack_w_in(w_in):
    d = w_in.shape[0]
    o_aq = POOL_WIDTH + 2 * GLA_QK + 2 * GLA_WIDTH + 2 * GLA_GATE_RANK
    o_ak = o_aq + ATTN_WIDTH
    o_av = o_ak + KV_WIDTH
    w_aq = w_in[:, o_aq:o_ak]
    w_ak = w_in[:, o_ak:o_av]
    pq, sq = _rope_swap_columns(ATTN_WIDTH)
    pk, sk = _rope_swap_columns(KV_WIDTH)
    gz = jnp.pad(w_in[:, o_aq - 2 * GLA_GATE_RANK:o_aq], ((0, 0), (0, LANES - 2 * GLA_GATE_RANK)))
    cols = [w_in[:, :o_aq - 2 * GLA_GATE_RANK], gz, w_aq, w_ak, w_in[:, o_av:],
            w_aq[:, pq] * sq[None, :], w_ak[:, pk] * sk[None, :]]
    return jnp.concatenate(cols, axis=1).astype(BF16)


def _block_diag(w):
    g, c, _ = w.shape
    eye = jnp.eye(g, dtype=w.dtype)
    return jnp.einsum('gcd,gh->gchd', w, eye).reshape(g * c, g * c)


def _rope_tables(n_ctx, n_lat_seq, n_lat_batch):
    half = HEAD_DIM // 2
    nf = half // 2
    freqs = ROPE_THETA ** (-jnp.arange(nf, dtype=F32) / nf)
    rows = n_lat_seq // GRID_W
    pos_row = jnp.repeat(jnp.arange(rows), GRID_W).astype(F32)
    pos_col = jnp.tile(jnp.arange(GRID_W), rows).astype(F32)
    ar = pos_row[:, None] * freqs[None, :]
    ac = pos_col[:, None] * freqs[None, :]
    cos = jnp.concatenate([jnp.cos(ar)] * 2 + [jnp.cos(ac)] * 2, axis=1)
    sin = jnp.concatenate([jnp.sin(ar)] * 2 + [jnp.sin(ac)] * 2, axis=1)
    reps = LANES // HEAD_DIM
    cos = jnp.tile(jnp.tile(cos, (1, reps)), (n_lat_batch, 1))
    sin = jnp.tile(jnp.tile(sin, (1, reps)), (n_lat_batch, 1))
    cos = jnp.concatenate([jnp.ones((n_ctx, LANES), F32), cos], axis=0)
    sin = jnp.concatenate([jnp.zeros((n_ctx, LANES), F32), sin], axis=0)
    return cos, sin


def kernel(x_prompt, x_sample, cache_k, cache_v, state_fwd, state_bwd, c, c_ctx, w_ada, b_ada, norm1_g, norm2_g,
           w_in, pool_w, pool_scale, gla_gate_w_f, gla_gate_b_f, gla_gate_w_b, gla_gate_b_b, gla_norm_g,
           attn_sink, w_out, peer_wq, peer_subkeys, peer_u, peer_v, final_norm_g):
    nb, seq, d = x_prompt.shape
    nlb, lseq, _ = x_sample.shape
    depth = w_ada.shape[0]
    n_ctx, n_lat = nb * seq, nlb * lseq
    n = n_ctx + n_lat
    tm = TOKEN_BLOCK
    nkeys = peer_subkeys.shape[3]
    assert d == D_MODEL and n_ctx % tm == 0 and lseq % tm == 0 and n_ctx % lseq == 0
    assert seq % GLA_CHUNK == 0 and lseq % GLA_BLOCK == 0 and lseq % GRID_W == 0 and lseq >= 3 * ATTN_BLOCK
    assert nlb + 1 <= SUBLANES and nkeys % SUBLANES == 0

    x0 = jnp.concatenate([x_prompt.reshape(n_ctx, d), x_sample.reshape(n_lat, d)], axis=0)
    cvec = jnp.zeros((SUBLANES, d), F32).at[0].set(c_ctx).at[1:1 + nlb].set(c)
    mods = _ada_mods(cvec, w_ada, b_ada)
    blk_row = np.concatenate([np.zeros(n_ctx // tm, np.int32),
                              1 + np.repeat(np.arange(nlb, dtype=np.int32), lseq // tm)])
    modblk = mods[:, blk_row, :].reshape(depth, n // tm, 6, d)
    cos, sin = _rope_tables(n_ctx, lseq, nlb)
    gla_consts = _gla_consts()
    hd = np.arange(GLA_WIDTH) // GLA_DV
    hmean = jnp.asarray((hd[:, None] == hd[None, :]).astype(np.float32) / GLA_DV)
    zero_state = jnp.zeros((nb, GLA_WIDTH, GLA_QK), F32)

    def layer(x, lp):
        (mod_l, n1, n2, w_in_l, pool_w_l, pool_scale_l, gw_f, gb_f, gw_b, gb_b, gnorm, sink, w_out_l,
         wq, subk, pu, pv, ck, cv, sf, sb) = lp
        w_big = _pack_w_in(w_in_l)
        w2 = jnp.zeros((LANES, 2 * GLA_QK), F32)
        w2 = w2.at[:GLA_GATE_RANK, :GLA_QK].set(gw_f).at[GLA_GATE_RANK:2 * GLA_GATE_RANK, GLA_QK:].set(gw_b)
        b2 = jnp.concatenate([gb_f, gb_b]).reshape(1, 2 * GLA_QK)
        pp, gq, gk, gv, gg, la_f, la_b, aq, ak, av = _inproj(x, mod_l, n1.reshape(1, d), w_big, cos, sin, w2, b2)

        w_bd = _block_diag(pool_w_l).astype(BF16)
        scale = pool_scale_l.reshape(1, POOL_WIDTH)
        y_pool = (_pool(pp, w_bd, scale, seq, 0, nb), _pool(pp, w_bd, scale, lseq, n_ctx // lseq, nlb))

        lblk = min(GLA_BLOCK, lseq)
        of_c, sf_c = _gla(gq, gk, gv, la_f, zero_state, gla_consts, seq, 0, nb, False)
        ob_c, sb_c = _gla(gq, gk, gv, la_b, zero_state, gla_consts, seq, 0, nb, True)
        of_l, _ = _gla(gq, gk, gv, la_f, _state_to_blockdiag_t(sf), gla_consts, lseq, n_ctx // lblk, nlb, False)
        ob_l, _ = _gla(gq, gk, gv, la_b, _state_to_blockdiag_t(sb), gla_consts, lseq, n_ctx // lblk, nlb, True)

        sink_b = jnp.broadcast_to(sink.reshape(ATTN_HEADS, 1), (ATTN_HEADS, LANES))
        y_attn = (_ctx_attn(aq, ak, av, sink_b, seq, nb), _lat_attn(aq, ak, av, ck, cv, sink_b, lseq, n_ctx, nlb))

        x1, h2t = _outproj(y_pool, (of_c, of_l), (ob_c, ob_l), y_attn, gg, x, mod_l,
                           gnorm.reshape(1, GLA_WIDTH), hmean, w_out_l.astype(BF16), n2.reshape(1, d))

        wqt = wq.T.astype(BF16)
        sk = subk.reshape(2 * PEER_HEADS, nkeys, PEER_HALF).astype(BF16)
        s1, s2 = _peer_scores(h2t, wqt, sk)
        cnt, a, rk, bx = _peer_gates(s1, s2)
        x2 = _peer_dense(h2t, pu.astype(BF16), pv.T.astype(BF16), cnt, a, rk, bx, x1, mod_l)
        return x2, (ak[:n_ctx], av[:n_ctx], _blockdiag_t_to_state(sf_c), _blockdiag_t_to_state(sb_c))

    past = cache_k.shape[2]
    xs = (modblk, norm1_g, norm2_g, w_in, pool_w, pool_scale, gla_gate_w_f, gla_gate_b_f, gla_gate_w_b,
          gla_gate_b_b, gla_norm_g, attn_sink, w_out, peer_wq, peer_subkeys, peer_u, peer_v,
          jnp.swapaxes(cache_k, 0, 1).reshape(depth, nlb, past, KV_WIDTH),
          jnp.swapaxes(cache_v, 0, 1).reshape(depth, nlb, past, KV_WIDTH),
          jnp.swapaxes(state_fwd, 0, 1), jnp.swapaxes(state_bwd, 0, 1))
    x_fin, (ks, vs, sfs, sbs) = lax.scan(layer, x0, xs)

    y = _final_norm(x_fin, final_norm_g)
    y_prompt = y[:n_ctx].reshape(nb, seq, d)
    y_sample = y[n_ctx:].reshape(nlb, lseq, d)
    new_k = jnp.swapaxes(ks.reshape(depth, nb, seq, ATTN_KV_HEADS, HEAD_DIM), 0, 1)
    new_v = jnp.swapaxes(vs.reshape(depth, nb, seq, ATTN_KV_HEADS, HEAD_DIM), 0, 1)
    return (y_prompt, y_sample, new_k, new_v, jnp.swapaxes(sfs, 0, 1), jnp.swapaxes(sbs, 0, 1))
```

```python
import functools
import math

import numpy as np
import jax
import jax.numpy as jnp
from jax import lax
from jax.experimental import pallas as pl
from jax.experimental.pallas import tpu as pltpu

F32 = jnp.float32
BF16 = jnp.bfloat16
HIGHEST = lax.Precision.HIGHEST

D_MODEL = 1024
GRID_W = 64
EPS = 1e-6
POOL_WIDTH = 256
POOL_GROUPS = 4
POOL_GROUP_DIM = 64
POOL_WINDOWS = (2, 4, 8, 16)
GLA_HEADS = 4
GLA_DV = 64
GLA_DK = 32
GLA_QK = GLA_HEADS * GLA_DK
GLA_WIDTH = GLA_HEADS * GLA_DV
GLA_GATE_RANK = 16
GLA_GATE_TAU = 16.0
GLA_CHUNK = 32
ATTN_HEADS = 8
ATTN_KV_HEADS = 2
ATTN_GROUP = 4
HEAD_DIM = 64
ATTN_WIDTH = ATTN_HEADS * HEAD_DIM
KV_WIDTH = ATTN_KV_HEADS * HEAD_DIM
WINDOW = 128
ATTN_BLOCK = 128
ROPE_THETA = 10000.0
NEG_INF = -1e30
PEER_HEADS = 8
PEER_HALF = 64
PEER_TOPK = 16

LANES = 128
SUBLANES = 8
VMEM_LIMIT = 56 * 1024 * 1024

TOKEN_BLOCK = 512
GLA_BLOCK = 512
PEER_EXPERT_BLOCK = 2048

C_POOL, C_GQ, C_GK, C_GV, C_GG, C_GZ, C_AQ, C_AK, C_AV, C_AQS, C_AKS, C_END = (
    0, 256, 384, 512, 768, 1024, 1152, 1664, 1792, 1920, 2432, 2560)


def _params(sem):
    return pltpu.CompilerParams(dimension_semantics=sem, vmem_limit_bytes=VMEM_LIMIT)


def _bdot(a, b):
    return jnp.dot(a.astype(BF16), b.astype(BF16), preferred_element_type=F32)


def _fdot(a, b):
    return jnp.dot(a, b, preferred_element_type=F32, precision=HIGHEST)


def _sigmoid(x):
    return 1.0 / (1.0 + jnp.exp(-x))


def _rms(x):
    return x * lax.rsqrt(jnp.mean(x * x, axis=-1, keepdims=True) + EPS)


def _ada_kernel(c_ref, w_ref, b_ref, o_ref):
    c = c_ref[...]
    o_ref[...] = _fdot(c * _sigmoid(c), w_ref[...]) + b_ref[...]


def _ada_mods(cvec, w_ada, b_ada):
    depth, d, six_d = w_ada.shape
    nj = six_d // d
    return pl.pallas_call(
        _ada_kernel,
        grid=(depth, nj),
        in_specs=[pl.BlockSpec((SUBLANES, d), lambda l, j: (0, 0)),
                  pl.BlockSpec((None, d, d), lambda l, j: (l, 0, j)),
                  pl.BlockSpec((None, 1, d), lambda l, j: (l, 0, j))],
        out_specs=pl.BlockSpec((None, SUBLANES, d), lambda l, j: (l, 0, j)),
        out_shape=jax.ShapeDtypeStruct((depth, SUBLANES, six_d), F32),
        compiler_params=_params(("parallel", "parallel")),
        name="ada_mod",
    )(cvec, w_ada, b_ada.reshape(depth, 1, six_d))


def _inproj_kernel(x_ref, mod_ref, n1_ref, w_ref, cos_ref, sin_ref, w2_ref, b2_ref,
                   pp_ref, gq_ref, gk_ref, gv_ref, gg_ref, laf_ref, lab_ref, aq_ref, ak_ref, av_ref):
    h = _rms(x_ref[...]) * n1_ref[...]
    h = (h * (1.0 + mod_ref[1:2, :]) + mod_ref[0:1, :]).astype(BF16)

    def proj(lo, hi):
        return jnp.dot(h, w_ref[:, lo:hi], preferred_element_type=F32)

    pp_ref[...] = proj(C_POOL, C_GQ)
    gq_ref[...] = proj(C_GQ, C_GK) * (GLA_DK ** -0.5)
    gk_ref[...] = proj(C_GK, C_GV)
    gv_ref[...] = proj(C_GV, C_GG)
    gg_ref[...] = proj(C_GG, C_GZ)
    logit = _fdot(proj(C_GZ, C_AQ), w2_ref[...]) + b2_ref[...]
    la = (jnp.minimum(logit, 0.0) - jnp.log1p(jnp.exp(-jnp.abs(logit)))) * (1.0 / GLA_GATE_TAU)
    laf_ref[...] = la[:, :GLA_QK]
    lab_ref[...] = la[:, GLA_QK:]
    cos = cos_ref[...]
    sin = sin_ref[...]
    cos4 = jnp.concatenate([cos] * (ATTN_WIDTH // LANES), axis=1)
    sin4 = jnp.concatenate([sin] * (ATTN_WIDTH // LANES), axis=1)
    aq_ref[...] = (proj(C_AQ, C_AK) * cos4 + proj(C_AQS, C_AKS) * sin4) * (HEAD_DIM ** -0.5)
    ak_ref[...] = proj(C_AK, C_AV) * cos + proj(C_AKS, C_END) * sin
    av_ref[...] = proj(C_AV, C_AQS)


def _inproj(x, modblk, n1, w_big, cos, sin, w2, b2):
    n, d = x.shape
    tm = TOKEN_BLOCK
    widths = (POOL_WIDTH, GLA_QK, GLA_QK, GLA_WIDTH, GLA_WIDTH, GLA_QK, GLA_QK, ATTN_WIDTH, KV_WIDTH, KV_WIDTH)
    row = lambda w: pl.BlockSpec((tm, w), lambda i: (i, 0))
    full = lambda a: pl.BlockSpec(a.shape, lambda i: (0,) * a.ndim)
    return pl.pallas_call(
        _inproj_kernel,
        grid=(n // tm,),
        in_specs=[row(d), pl.BlockSpec((None, 6, d), lambda i: (i, 0, 0)), full(n1), full(w_big),
                  row(LANES), row(LANES), full(w2), full(b2)],
        out_specs=[row(w) for w in widths],
        out_shape=[jax.ShapeDtypeStruct((n, w), F32) for w in widths],
        compiler_params=_params(("parallel",)),
        name="in_proj",
    )(x, modblk, n1, w_big, cos, sin, w2, b2)


def _pool_kernel(x_ref, w_ref, scale_ref, o_ref):
    t = x_ref.shape[0]
    pad = 32
    n = t + pad
    x = x_ref[...]
    xp = jnp.concatenate([x, jnp.zeros((pad, POOL_WIDTH), F32)], axis=0)
    back = {1: xp}
    for w in (2, 4, 8, 16):
        back[w] = back[w // 2] + pltpu.roll(back[w // 2], w // 2, 0)
    grp = lax.broadcasted_iota(jnp.int32, (1, POOL_WIDTH), 1) // POOL_GROUP_DIM
    tok = lax.broadcasted_iota(jnp.int32, (t, 1), 0)
    wsum = None
    cnt = None
    for g, w in enumerate(POOL_WINDOWS):
        left = w // 2
        right = w - 1 - left
        ws = back[w] if right == 0 else pltpu.roll(back[w], n - right, 0)
        ws = ws[:t]
        c = (jnp.minimum(tok + right + 1, t) - jnp.maximum(tok - left, 0)).astype(F32)
        if wsum is None:
            wsum, cnt = ws, jnp.broadcast_to(c, (t, POOL_WIDTH))
        else:
            wsum = jnp.where(grp == g, ws, wsum)
            cnt = jnp.where(grp == g, c, cnt)
    diff = wsum / cnt - x
    o_ref[...] = _bdot(diff, w_ref[...]) * scale_ref[...]


def _pool(pp, w_bd, scale, seq, first_block, nseq):
    return pl.pallas_call(
        _pool_kernel,
        grid=(nseq,),
        in_specs=[pl.BlockSpec((seq, POOL_WIDTH), lambda b: (first_block + b, 0)),
                  pl.BlockSpec(w_bd.shape, lambda b: (0, 0)),
                  pl.BlockSpec(scale.shape, lambda b: (0, 0))],
        out_specs=pl.BlockSpec((seq, POOL_WIDTH), lambda b: (b, 0)),
        out_shape=jax.ShapeDtypeStruct((nseq * seq, POOL_WIDTH), F32),
        compiler_params=_params(("parallel",)),
        name="pool_mix",
    )(pp, w_bd, scale)


def _gla_rows(reverse):
    out = []
    for s in range(GLA_CHUNK):
        g = s // SUBLANES
        out.append((0, SUBLANES * (g + 1)) if reverse else (SUBLANES * g, GLA_CHUNK))
    return out


GLA_PAIR_ROWS = sum(hi - lo for lo, hi in _gla_rows(False))


def _gla_kernel(q_ref, k_ref, v_ref, la_ref, s0_ref, hexp_ref, tri_ref, bd_ref,
                o_ref, sout_ref, st_ref, b_ref, p_ref, z_ref, *, reverse, nchunk):
    j = pl.program_id(1)

    @pl.when(j == 0)
    def _():
        st_ref[...] = s0_ref[...]

    rows = _gla_rows(reverse)
    ngrp = GLA_CHUNK // SUBLANES
    order = list(range(nchunk - 1, -1, -1) if reverse else range(nchunk))
    chunk = lambda c: slice(c * GLA_CHUNK, (c + 1) * GLA_CHUNK)
    for c in order:
        b_ref[chunk(c), :] = _fdot(tri_ref[...], la_ref[chunk(c), :])
    tio = lax.broadcasted_iota(jnp.int32, (SUBLANES, 1), 0)
    for c in order:
        r0 = c * GLA_CHUNK
        off = c * GLA_PAIR_ROWS
        q = q_ref[chunk(c), :]
        b = b_ref[chunk(c), :]
        for s, (lo, hi) in enumerate(rows):
            dlt = jnp.minimum(b[lo:hi] - b_ref[r0 + s:r0 + s + 1, :], 0.0)
            p = q[lo:hi] * k_ref[r0 + s:r0 + s + 1, :] * jnp.exp(dlt)
            p_ref[off:off + hi - lo, :] = p
            d0 = hi - lo - SUBLANES if reverse else 0
            valid = (tio <= s % SUBLANES) if reverse else (tio >= s % SUBLANES)
            p_ref[off + d0:off + d0 + SUBLANES, :] = jnp.where(valid, p[d0:d0 + SUBLANES], 0.0)
            off += hi - lo
    z_ref[...] = jnp.dot(p_ref[...].astype(BF16), hexp_ref[...], preferred_element_type=F32)
    for c in order:
        r0 = c * GLA_CHUNK
        off = c * GLA_PAIR_ROWS
        oacc = [None] * ngrp
        for g in range(ngrp):
            lo, hi = rows[g * SUBLANES]
            cg = None
            for s in range(g * SUBLANES, (g + 1) * SUBLANES):
                term = z_ref[off:off + hi - lo, :] * v_ref[r0 + s:r0 + s + 1, :]
                cg = term if cg is None else cg + term
                off += hi - lo
            for rg in range(lo // SUBLANES, hi // SUBLANES):
                piece = cg[(rg - lo // SUBLANES) * SUBLANES:(rg - lo // SUBLANES + 1) * SUBLANES]
                oacc[rg] = piece if oacc[rg] is None else oacc[rg] + piece
        o_ref[chunk(c), :] = jnp.concatenate(oacc, axis=0)
    kvs, decay = {}, {}
    for c in order:
        b = b_ref[chunk(c), :]
        blast = b[0:1, :] if reverse else b[GLA_CHUNK - 1:GLA_CHUNK, :]
        ke = k_ref[chunk(c), :] * jnp.exp(blast - b)
        kvs[c] = jnp.dot(v_ref[chunk(c), :].T.astype(BF16), ke.astype(BF16),
                         preferred_element_type=F32) * bd_ref[...]
        decay[c] = jnp.exp(blast)
    st = st_ref[...]
    for c in order:
        qe = (q_ref[chunk(c), :] * jnp.exp(b_ref[chunk(c), :])).astype(BF16)
        o_ref[chunk(c), :] += lax.dot_general(qe, st.astype(BF16), (((1,), (1,)), ((), ())),
                                              preferred_element_type=F32)
        st = st * decay[c] + kvs[c]
    st_ref[...] = st

    @pl.when(j == pl.num_programs(1) - 1)
    def _():
        sout_ref[...] = st_ref[...]


def _gla_consts():
    hd = np.arange(GLA_QK) // GLA_DK
    he = np.arange(GLA_WIDTH) // GLA_DV
    hexp = (hd[:, None] == he[None, :]).astype(np.float32)
    bd = hexp.T.copy()
    t = np.arange(GLA_CHUNK)
    tri_f = (t[None, :] <= t[:, None]).astype(np.float32)
    tri_b = (t[None, :] >= t[:, None]).astype(np.float32)
    return jnp.asarray(hexp, BF16), jnp.asarray(bd, F32), jnp.asarray(tri_f), jnp.asarray(tri_b)


def _gla(gq, gk, gv, la, s0t, consts, seq, first_block, nseq, reverse):
    hexp, bd, tri_f, tri_b = consts
    tri = tri_b if reverse else tri_f
    blk = min(GLA_BLOCK, seq)
    nblk = seq // blk
    if reverse:
        tok = lambda b, j: (first_block + b * nblk + (nblk - 1 - j), 0)
        otok = lambda b, j: (b * nblk + (nblk - 1 - j), 0)
    else:
        tok = lambda b, j: (first_block + b * nblk + j, 0)
        otok = lambda b, j: (b * nblk + j, 0)
    const = lambda a: pl.BlockSpec(a.shape, lambda b, j: (0,) * a.ndim)
    return pl.pallas_call(
        functools.partial(_gla_kernel, reverse=reverse, nchunk=blk // GLA_CHUNK),
        grid=(nseq, nblk),
        in_specs=[pl.BlockSpec((blk, GLA_QK), tok), pl.BlockSpec((blk, GLA_QK), tok),
                  pl.BlockSpec((blk, GLA_WIDTH), tok), pl.BlockSpec((blk, GLA_QK), tok),
                  pl.BlockSpec((None, GLA_WIDTH, GLA_QK), lambda b, j: (b, 0, 0)),
                  const(hexp), const(tri), const(bd)],
        out_specs=[pl.BlockSpec((blk, GLA_WIDTH), otok),
                   pl.BlockSpec((None, GLA_WIDTH, GLA_QK), lambda b, j: (b, 0, 0))],
        out_shape=[jax.ShapeDtypeStruct((nseq * seq, GLA_WIDTH), F32),
                   jax.ShapeDtypeStruct((nseq, GLA_WIDTH, GLA_QK), F32)],
        scratch_shapes=[pltpu.VMEM((GLA_WIDTH, GLA_QK), F32), pltpu.VMEM((blk, GLA_QK), F32),
                        pltpu.VMEM((blk // GLA_CHUNK * GLA_PAIR_ROWS, GLA_QK), F32),
                        pltpu.VMEM((blk // GLA_CHUNK * GLA_PAIR_ROWS, GLA_WIDTH), F32)],
        compiler_params=_params(("parallel", "arbitrary")),
        name="gla_bwd" if reverse else "gla_fwd",
    )(gq, gk, gv, la, s0t, hexp, tri, bd)


def _state_to_blockdiag_t(s):
    b = s.shape[0]
    eye = jnp.eye(GLA_HEADS, dtype=s.dtype)
    return jnp.einsum('bhde,hg->bhegd', s, eye).reshape(b, GLA_WIDTH, GLA_QK)


def _blockdiag_t_to_state(st):
    b = st.shape[0]
    eye = jnp.eye(GLA_HEADS, dtype=st.dtype)
    return jnp.einsum('bhegd,hg->bhde', st.reshape(b, GLA_HEADS, GLA_DV, GLA_HEADS, GLA_DK), eye)


ATTN_SUBBLOCKS = 4


def _stack_heads(q_ref, kv, rows=slice(None)):
    return jnp.concatenate([q_ref[rows, (kv * ATTN_GROUP + r) * HEAD_DIM:(kv * ATTN_GROUP + r + 1) * HEAD_DIM]
                            for r in range(ATTN_GROUP)], axis=0)


def _sink_column(sink_ref, kv, rows):
    return jnp.concatenate([jnp.broadcast_to(sink_ref[kv * ATTN_GROUP + r:kv * ATTN_GROUP + r + 1, 0:1], (rows, 1))
                            for r in range(ATTN_GROUP)], axis=0)


def _qk(q, k):
    return lax.dot_general(q.astype(BF16), k.astype(BF16), (((1,), (1,)), ((), ())), preferred_element_type=F32)


def _ctx_attn_kernel(q_ref, k_ref, v_ref, sink_ref, o_ref, *, seq):
    t = seq
    chains = [(slice(sb * seq, (sb + 1) * seq), kv)
              for sb in range(q_ref.shape[0] // seq) for kv in range(ATTN_KV_HEADS)]
    lanes = lambda kv: slice(kv * HEAD_DIM, (kv + 1) * HEAD_DIM)
    s = [_qk(_stack_heads(q_ref, kv, rows), k_ref[rows, lanes(kv)]) for rows, kv in chains]
    sink = [_sink_column(sink_ref, kv, t) for _, kv in chains]
    m = [jnp.maximum(jnp.max(s[c], axis=-1, keepdims=True), sink[c]) for c in range(len(chains))]
    p = [jnp.exp(s[c] - m[c]) for c in range(len(chains))]
    den = [jnp.sum(p[c], axis=-1, keepdims=True) + jnp.exp(sink[c] - m[c]) for c in range(len(chains))]
    for c, (rows, kv) in enumerate(chains):
        o = _bdot(p[c], v_ref[rows, lanes(kv)]) / den[c]
        for r in range(ATTN_GROUP):
            h = kv * ATTN_GROUP + r
            o_ref[rows, h * HEAD_DIM:(h + 1) * HEAD_DIM] = o[r * t:(r + 1) * t]


def _ctx_attn(aq, ak, av, sink_b, seq, nseq):
    per = ATTN_SUBBLOCKS if nseq % ATTN_SUBBLOCKS == 0 else 1
    tok = lambda w: pl.BlockSpec((per * seq, w), lambda b: (b, 0))
    return pl.pallas_call(
        functools.partial(_ctx_attn_kernel, seq=seq),
        grid=(nseq // per,),
        in_specs=[tok(ATTN_WIDTH), tok(KV_WIDTH), tok(KV_WIDTH), pl.BlockSpec(sink_b.shape, lambda b: (0, 0))],
        out_specs=tok(ATTN_WIDTH),
        out_shape=jax.ShapeDtypeStruct((nseq * seq, ATTN_WIDTH), F32),
        compiler_params=_params(("parallel",)),
        name="ctx_attn",
    )(aq, ak, av, sink_b)


def _lat_attn_kernel(q_ref, k_ref, v_ref, ck_ref, cv_ref, sink_ref, o_ref):
    t = k_ref.shape[0]
    blk = ATTN_BLOCK
    span = 3 * blk
    per = q_ref.shape[0] // blk
    lanes = lambda kv: slice(kv * HEAD_DIM, (kv + 1) * HEAD_DIM)
    chains, starts, inwin = [], [], []
    for sb in range(per):
        i = pl.program_id(1) * per + sb
        start = pl.multiple_of(jnp.clip((i - 1) * blk, 0, t - span), blk)
        qpos = i * blk + lax.broadcasted_iota(jnp.int32, (blk, 1), 0)
        kpos = start + lax.broadcasted_iota(jnp.int32, (1, span), 1)
        win = jnp.concatenate([jnp.abs(qpos - kpos) <= WINDOW] * ATTN_GROUP, axis=0)
        for kv in range(ATTN_KV_HEADS):
            chains.append((slice(sb * blk, (sb + 1) * blk), kv))
            starts.append(start)
            inwin.append(win)
    n = len(chains)
    q = [_stack_heads(q_ref, kv, rows) for rows, kv in chains]
    s_loc = [jnp.where(inwin[c], _qk(q[c], k_ref[pl.ds(starts[c], span), :][:, lanes(chains[c][1])]), NEG_INF)
             for c in range(n)]
    s_ctx = [_qk(q[c], ck_ref[:, lanes(chains[c][1])]) for c in range(n)]
    sink = [_sink_column(sink_ref, kv, blk) for _, kv in chains]
    m = [jnp.maximum(jnp.maximum(jnp.max(s_loc[c], axis=-1, keepdims=True),
                                 jnp.max(s_ctx[c], axis=-1, keepdims=True)), sink[c]) for c in range(n)]
    p_loc = [jnp.exp(s_loc[c] - m[c]) for c in range(n)]
    p_ctx = [jnp.exp(s_ctx[c] - m[c]) for c in range(n)]
    den = [jnp.sum(p_loc[c], axis=-1, keepdims=True) + jnp.sum(p_ctx[c], axis=-1, keepdims=True)
           + jnp.exp(sink[c] - m[c]) for c in range(n)]
    for c, (rows, kv) in enumerate(chains):
        o = (_bdot(p_loc[c], v_ref[pl.ds(starts[c], span), :][:, lanes(kv)])
             + _bdot(p_ctx[c], cv_ref[:, lanes(kv)])) / den[c]
        for r in range(ATTN_GROUP):
            h = kv * ATTN_GROUP + r
            o_ref[rows, h * HEAD_DIM:(h + 1) * HEAD_DIM] = o[r * blk:(r + 1) * blk]


def _lat_attn(aq, ak, av, ck, cv, sink_b, seq, first_tok, nseq):
    qrows = ATTN_SUBBLOCKS * ATTN_BLOCK
    assert seq % qrows == 0 and first_tok % qrows == 0
    nblk = seq // qrows
    fb_q = first_tok // qrows
    fb_s = first_tok // seq
    return pl.pallas_call(
        _lat_attn_kernel,
        grid=(nseq, nblk),
        in_specs=[pl.BlockSpec((qrows, ATTN_WIDTH), lambda b, i: (fb_q + b * nblk + i, 0)),
                  pl.BlockSpec((seq, KV_WIDTH), lambda b, i: (fb_s + b, 0)),
                  pl.BlockSpec((seq, KV_WIDTH), lambda b, i: (fb_s + b, 0)),
                  pl.BlockSpec((None,) + ck.shape[1:], lambda b, i: (b, 0, 0)),
                  pl.BlockSpec((None,) + cv.shape[1:], lambda b, i: (b, 0, 0)),
                  pl.BlockSpec(sink_b.shape, lambda b, i: (0, 0))],
        out_specs=pl.BlockSpec((qrows, ATTN_WIDTH), lambda b, i: (b * nblk + i, 0)),
        out_shape=jax.ShapeDtypeStruct((nseq * seq, ATTN_WIDTH), F32),
        compiler_params=_params(("parallel", "arbitrary")),
        name="lat_attn",
    )(aq, ak, av, ck, cv, sink_b)


def _outproj_kernel(ypc_ref, ypl_ref, ofc_ref, ofl_ref, obc_ref, obl_ref, yac_ref, yal_ref,
                    gg_ref, x_ref, mod_ref, gn_ref, hm_ref, w_ref, n2_ref, wqt_ref, sk_ref,
                    xo_ref, h2t_ref, s1_ref, s2_ref, *, ctx_blocks):
    is_ctx = pl.program_id(0) < ctx_blocks
    pick = lambda c_ref, l_ref: jnp.where(is_ctx, c_ref[...], l_ref[...])
    o = pick(ofc_ref, ofl_ref) + pick(obc_ref, obl_ref)
    ms = _fdot(o * o, hm_ref[...])
    gg = gg_ref[...]
    y = o * lax.rsqrt(ms + EPS) * gn_ref[...] * (gg * _sigmoid(gg))
    mix = jnp.concatenate([pick(ypc_ref, ypl_ref), y, pick(yac_ref, yal_ref)], axis=1)
    xn = x_ref[...] + mod_ref[2:3, :] * _bdot(mix, w_ref[...])
    xo_ref[...] = xn
    h2 = _rms(xn) * n2_ref[...] * (1.0 + mod_ref[4:5, :]) + mod_ref[3:4, :]
    h2t = h2.T.astype(BF16)
    h2t_ref[...] = pltpu.bitcast(h2t, jnp.uint32)
    qt = jnp.dot(wqt_ref[...], h2t, preferred_element_type=F32)
    for h in range(PEER_HEADS):
        for p, out in enumerate((s1_ref, s2_ref)):
            r = (2 * h + p) * PEER_HALF
            out[h] = jnp.dot(sk_ref[2 * h + p], qt[r:r + PEER_HALF].astype(BF16), preferred_element_type=F32)


def _outproj(ypool, o_f, o_b, yattn, gg, x, modblk, gnorm, hmean, w_out, n2, wqt, sk):
    n, d = x.shape
    tm = TOKEN_BLOCK
    nkeys = sk.shape[1]
    scores = pl.BlockSpec((PEER_HEADS, nkeys, tm), lambda i: (0, 0, i))
    cb = ypool[0].shape[0] // tm
    lb = ypool[1].shape[0] // tm
    row = lambda w: pl.BlockSpec((tm, w), lambda i: (i, 0))
    ctx = lambda w: pl.BlockSpec((tm, w), lambda i: (jnp.minimum(i, cb - 1), 0))
    lat = lambda w: pl.BlockSpec((tm, w), lambda i: (jnp.clip(i - cb, 0, lb - 1), 0))
    full = lambda a: pl.BlockSpec(a.shape, lambda i: (0,) * a.ndim)
    pair = lambda w: [ctx(w), lat(w)]
    return pl.pallas_call(
        functools.partial(_outproj_kernel, ctx_blocks=cb),
        grid=(n // tm,),
        in_specs=pair(POOL_WIDTH) + pair(GLA_WIDTH) + pair(GLA_WIDTH) + pair(ATTN_WIDTH) + [
            row(GLA_WIDTH), row(d), pl.BlockSpec((None, 6, d), lambda i: (i, 0, 0)),
            full(gnorm), full(hmean), full(w_out), full(n2), full(wqt), full(sk)],
        out_specs=[row(d), pl.BlockSpec((d // 2, tm), lambda i: (0, i)), scores, scores],
        out_shape=[jax.ShapeDtypeStruct((n, d), F32), jax.ShapeDtypeStruct((d // 2, n), jnp.uint32),
                   jax.ShapeDtypeStruct((PEER_HEADS, nkeys, n), F32),
                   jax.ShapeDtypeStruct((PEER_HEADS, nkeys, n), F32)],
        compiler_params=_params(("parallel",)),
        name="out_proj",
    )(*ypool, *o_f, *o_b, *yattn, gg, x, modblk, gnorm, hmean, w_out, n2, wqt, sk)


PEER_CANDS = [(a, b) for a in range(PEER_TOPK) for b in range(PEER_TOPK) if (a + 1) * (b + 1) <= PEER_TOPK]
PEER_CAND_ROWS = -(-len(PEER_CANDS) // SUBLANES) * SUBLANES


def _top_values_many(arrays, count, with_rank):
    cur = list(arrays)
    vals = [[] for _ in cur]
    rank = [jnp.full(c.shape, float(count), F32) if w else None for c, w in zip(cur, with_rank)]
    for r in range(count):
        for k in range(len(cur)):
            m = jnp.max(cur[k], axis=0, keepdims=True)
            vals[k].append(m)
            hit = cur[k] == m
            if with_rank[k]:
                rank[k] = jnp.where(hit, float(r), rank[k])
            cur[k] = jnp.where(hit, -jnp.inf, cur[k])
    return vals, rank


def _top_values(cur, count):
    return _top_values_many([cur], count, [False])[0][0]


def _count_above(sorted_rows, x):
    t = sorted_rows
    assert len(t) == 16
    g1 = t[7] > x
    g2 = jnp.where(g1, t[11], t[3]) > x
    g3 = jnp.where(g1, jnp.where(g2, t[13], t[9]), jnp.where(g2, t[5], t[1])) > x
    lo4 = jnp.where(g2, jnp.where(g3, t[6], t[4]), jnp.where(g3, t[2], t[0]))
    hi4 = jnp.where(g2, jnp.where(g3, t[14], t[12]), jnp.where(g3, t[10], t[8]))
    g4 = jnp.where(g1, hi4, lo4) > x
    g5 = t[15] > x
    one = lambda g, v: jnp.where(g, v, 0.0)
    return one(g1, 8.0) + one(g2, 4.0) + one(g3, 2.0) + one(g4, 1.0) + one(g5, 1.0)


def _bitonic_merge_desc(a):
    n = len(a)
    j = n // 2
    while j >= 1:
        for i in range(n):
            l = i ^ j
            if l > i:
                a[i], a[l] = jnp.maximum(a[i], a[l]), jnp.minimum(a[i], a[l])
        j //= 2
    return a


def _top16_sorted(s):
    n = PEER_TOPK
    assert s.shape[0] == n * SUBLANES
    a = [s[v * SUBLANES:(v + 1) * SUBLANES, :] for v in range(n)]
    k = 2
    while k <= n:
        j = k // 2
        while j >= 1:
            for i in range(n):
                l = i ^ j
                if l > i:
                    hi, lo = jnp.maximum(a[i], a[l]), jnp.minimum(a[i], a[l])
                    a[i], a[l] = (hi, lo) if (i & k) == 0 else (lo, hi)
            j //= 2
        k *= 2
    shift = SUBLANES // 2
    while shift >= 1:
        b = [pltpu.roll(x, SUBLANES - shift, 0) for x in a]
        a = _bitonic_merge_desc([jnp.maximum(a[v], b[n - 1 - v]) for v in range(n)])
        shift //= 2
    return [x[0:1, :] for x in a]


PEER_DENSE_RANKS = 4
PEER_GATE_TILE = 512
assert all(b < PEER_DENSE_RANKS for a, b in PEER_CANDS if a >= PEER_DENSE_RANKS)


def _peer_gate_kernel(s1_ref, s2_ref, cnt_ref, a_ref, rk_ref, bx_ref, cand_ref):
    width = cand_ref.shape[1]
    ntile = s1_ref.shape[2] // width

    def one_tile(it, carry):
        h = it // ntile
        cols = pl.ds(pl.multiple_of((it % ntile) * width, width), width)
        s1 = s1_ref[h, :, cols]
        s2 = s2_ref[h, :, cols]
        v1 = _top16_sorted(s1)
        v2 = _top16_sorted(s2)
        rk = _count_above(v2, s2)
        cand_ref[...] = jnp.full(cand_ref.shape, -jnp.inf, F32)
        for r, (a, b) in enumerate(PEER_CANDS):
            cand_ref[r:r + 1, :] = v1[a] + v2[b]
        best = _top_values(cand_ref[...], PEER_TOPK)
        z = None
        for val in best:
            e = jnp.exp(val - best[0])
            z = e if z is None else z + e
        tau = best[-1]
        cnt = None
        for b in range(PEER_DENSE_RANKS):
            reach = jnp.where(s1 + v2[b] >= tau, 1.0, 0.0)
            cnt = reach if cnt is None else cnt + reach
        for a in range(PEER_DENSE_RANKS):
            full = None
            for b in range(PEER_TOPK // (a + 1)):
                reach = jnp.where(v1[a] + v2[b] >= tau, 1.0, 0.0)
                full = reach if full is None else full + reach
            cnt = jnp.where(s1 == v1[a], full, cnt)
        cnt_ref[h, :, cols] = cnt
        a_ref[h, :, cols] = jnp.exp(s1 - v1[0]) * (0.5 / z)
        rk_ref[h, :, cols] = pltpu.bitcast(rk.astype(BF16), jnp.uint32)
        bx_ref[h, :, cols] = pltpu.bitcast(jnp.exp(s2 - v2[0]).astype(BF16), jnp.uint32)
        return carry

    lax.fori_loop(0, PEER_HEADS * ntile, one_tile, 0)


def _peer_gates(s1, s2):
    _, nkeys, n = s1.shape
    tb = TOKEN_BLOCK
    blk = pl.BlockSpec((PEER_HEADS, nkeys, tb), lambda i: (0, 0, i))
    half = pl.BlockSpec((PEER_HEADS, nkeys // 2, tb), lambda i: (0, 0, i))
    return pl.pallas_call(
        _peer_gate_kernel,
        grid=(n // tb,),
        in_specs=[blk, blk],
        out_specs=[blk, blk, half, half],
        out_shape=[jax.ShapeDtypeStruct((PEER_HEADS, nkeys, n), F32),
                   jax.ShapeDtypeStruct((PEER_HEADS, nkeys, n), F32),
                   jax.ShapeDtypeStruct((PEER_HEADS, nkeys // 2, n), jnp.uint32),
                   jax.ShapeDtypeStruct((PEER_HEADS, nkeys // 2, n), jnp.uint32)],
        scratch_shapes=[pltpu.VMEM((PEER_CAND_ROWS, PEER_GATE_TILE), F32)],
        compiler_params=_params(("parallel",)),
        name="peer_gates",
    )(s1, s2)


PEER_ROW_GROUP = 4
PEER_TILE_ROWS = 128


def _peer_dense_kernel(h2t_ref, u_ref, vt_ref, cnt_ref, a_ref, rk_ref, bx_ref, x_ref, mod_ref,
                       xo_ref, yt_ref, at_ref, w_ref, *, nkeys):
    e = pl.program_id(1)
    per_step = u_ref.shape[0] // nkeys

    @pl.when(e == 0)
    def _():
        yt_ref[...] = jnp.zeros(yt_ref.shape, F32)

    at_ref[...] = jnp.dot(u_ref[...], pltpu.bitcast(h2t_ref[...], BF16), preferred_element_type=F32)
    first_key = pl.multiple_of(e * per_step, SUBLANES)
    zero = jnp.zeros((PEER_TILE_ROWS, LANES), BF16)
    for i8 in range(0, per_step, SUBLANES):
        for tt in range(at_ref.shape[1] // LANES):
            cols = slice(tt * LANES, (tt + 1) * LANES)
            for jh in range(nkeys // PEER_TILE_ROWS):
                words = slice(jh * PEER_TILE_ROWS // 2, (jh + 1) * PEER_TILE_ROWS // 2)
                for i0 in range(i8, i8 + SUBLANES, PEER_ROW_GROUP):
                    g = [None] * PEER_ROW_GROUP
                    for h in range(PEER_HEADS):
                        rk = pltpu.bitcast(rk_ref[h, words, cols], BF16)
                        bx = pltpu.bitcast(bx_ref[h, words, cols], BF16)
                        cnts = cnt_ref[h, pl.ds(first_key + i8, SUBLANES), cols]
                        arows = a_ref[h, pl.ds(first_key + i8, SUBLANES), cols]
                        for ii in range(PEER_ROW_GROUP):
                            r = i0 - i8 + ii
                            term = (jnp.where(rk < cnts[r:r + 1, :].astype(BF16), bx, zero)
                                    * arows[r:r + 1, :].astype(BF16))
                            g[ii] = term if g[ii] is None else g[ii] + term
                    for ii in range(PEER_ROW_GROUP):
                        r0 = (i0 + ii) * nkeys + jh * PEER_TILE_ROWS
                        rows = slice(r0, r0 + PEER_TILE_ROWS)
                        xb = at_ref[rows, cols].astype(BF16)
                        w_ref[rows, cols] = g[ii] * (xb * (1.0 + lax.erf(xb * (2.0 ** -0.5))))
    yt_ref[...] += jnp.dot(vt_ref[...], w_ref[...], preferred_element_type=F32)

    @pl.when(e == pl.num_programs(1) - 1)
    def _():
        xo_ref[...] = x_ref[...] + mod_ref[5:6, :] * yt_ref[...].T


def _peer_dense(h2t, u, vt, cnt, a, rk, bx, x, modblk):
    n, d = x.shape
    nexp = u.shape[0]
    nkeys = cnt.shape[1]
    tb = TOKEN_BLOCK
    eb = min(PEER_EXPERT_BLOCK, nexp)
    ne = nexp // eb
    assert nkeys == LANES and eb % (SUBLANES * nkeys) == 0 and SUBLANES % PEER_ROW_GROUP == 0 and ne * eb == nexp
    keyed = pl.BlockSpec((PEER_HEADS, nkeys, tb), lambda t, e: (0, 0, t))
    packed = pl.BlockSpec((PEER_HEADS, nkeys // 2, tb), lambda t, e: (0, 0, t))
    return pl.pallas_call(
        functools.partial(_peer_dense_kernel, nkeys=nkeys),
        grid=(n // tb, ne),
        in_specs=[pl.BlockSpec((d // 2, tb), lambda t, e: (0, t)),
                  pl.BlockSpec((eb, d), lambda t, e: (e, 0)),
                  pl.BlockSpec((d, eb), lambda t, e: (0, e)),
                  keyed, keyed, packed, packed,
                  pl.BlockSpec((tb, d), lambda t, e: (t, 0)),
                  pl.BlockSpec((None, 6, d), lambda t, e: (t, 0, 0))],
        out_specs=pl.BlockSpec((tb, d), lambda t, e: (t, 0)),
        out_shape=jax.ShapeDtypeStruct((n, d), F32),
        scratch_shapes=[pltpu.VMEM((d, tb), F32), pltpu.VMEM((eb, tb), F32), pltpu.VMEM((eb, tb), BF16)],
        compiler_params=_params(("parallel", "arbitrary")),
        name="peer_dense",
    )(h2t, u, vt, cnt, a, rk, bx, x, modblk)


def _final_norm_kernel(x_ref, g_ref, o_ref):
    o_ref[...] = _rms(x_ref[...]) * g_ref[...]


def _final_norm(x, g):
    n, d = x.shape
    tm = TOKEN_BLOCK
    return pl.pallas_call(
        _final_norm_kernel,
        grid=(n // tm,),
        in_specs=[pl.BlockSpec((tm, d), lambda i: (i, 0)), pl.BlockSpec((1, d), lambda i: (0, 0))],
        out_specs=pl.BlockSpec((tm, d), lambda i: (i, 0)),
        out_shape=jax.ShapeDtypeStruct((n, d), F32),
        compiler_params=_params(("parallel",)),
        name="final_norm",
    )(x, g.reshape(1, d))


def _rope_swap_columns(width):
    half = HEAD_DIM // 2
    nf = half // 2
    perm = np.zeros(width, np.int32)
    sign = np.zeros(width, np.float32)
    for c in range(width):
        r = c % half
        if r < nf:
            perm[c], sign[c] = c + nf, -1.0
        else:
            perm[c], sign[c] = c - nf, 1.0
    return perm, sign


def _pack_w_in(w_in):
    d = w_in.shape[0]
    o_aq = POOL_WIDTH + 2 * GLA_QK + 2 * GLA_WIDTH + 2 * GLA_GATE_RANK
    o_ak = o_aq + ATTN_WIDTH
    o_av = o_ak + KV_WIDTH
    w_aq = w_in[:, o_aq:o_ak]
    w_ak = w_in[:, o_ak:o_av]
    pq, sq = _rope_swap_columns(ATTN_WIDTH)
    pk, sk = _rope_swap_columns(KV_WIDTH)
    gz = jnp.pad(w_in[:, o_aq - 2 * GLA_GATE_RANK:o_aq], ((0, 0), (0, LANES - 2 * GLA_GATE_RANK)))
    cols = [w_in[:, :o_aq - 2 * GLA_GATE_RANK], gz, w_aq, w_ak, w_in[:, o_av:],
            w_aq[:, pq] * sq[None, :], w_ak[:, pk] * sk[None, :]]
    return jnp.concatenate(cols, axis=1).astype(BF16)


def _block_diag(w):
    g, c, _ = w.shape
    eye = jnp.eye(g, dtype=w.dtype)
    return jnp.einsum('gcd,gh->gchd', w, eye).reshape(g * c, g * c)


def _rope_tables(n_ctx, n_lat_seq, n_lat_batch):
    half = HEAD_DIM // 2
    nf = half // 2
    freqs = ROPE_THETA ** (-jnp.arange(nf, dtype=F32) / nf)
    rows = n_lat_seq // GRID_W
    pos_row = jnp.repeat(jnp.arange(rows), GRID_W).astype(F32)
    pos_col = jnp.tile(jnp.arange(GRID_W), rows).astype(F32)
    ar = pos_row[:, None] * freqs[None, :]
    ac = pos_col[:, None] * freqs[None, :]
    cos = jnp.concatenate([jnp.cos(ar)] * 2 + [jnp.cos(ac)] * 2, axis=1)
    sin = jnp.concatenate([jnp.sin(ar)] * 2 + [jnp.sin(ac)] * 2, axis=1)
    reps = LANES // HEAD_DIM
    cos = jnp.tile(jnp.tile(cos, (1, reps)), (n_lat_batch, 1))
    sin = jnp.tile(jnp.tile(sin, (1, reps)), (n_lat_batch, 1))
    cos = jnp.concatenate([jnp.ones((n_ctx, LANES), F32), cos], axis=0)
    sin = jnp.concatenate([jnp.zeros((n_ctx, LANES), F32), sin], axis=0)
    return cos, sin


def kernel(x_prompt, x_sample, cache_k, cache_v, state_fwd, state_bwd, c, c_ctx, w_ada, b_ada, norm1_g, norm2_g,
           w_in, pool_w, pool_scale, gla_gate_w_f, gla_gate_b_f, gla_gate_w_b, gla_gate_b_b, gla_norm_g,
           attn_sink, w_out, peer_wq, peer_subkeys, peer_u, peer_v, final_norm_g):
    nb, seq, d = x_prompt.shape
    nlb, lseq, _ = x_sample.shape
    depth = w_ada.shape[0]
    n_ctx, n_lat = nb * seq, nlb * lseq
    n = n_ctx + n_lat
    tm = TOKEN_BLOCK
    nkeys = peer_subkeys.shape[3]
    assert d == D_MODEL and n_ctx % tm == 0 and lseq % tm == 0 and n_ctx % lseq == 0
    assert seq % GLA_CHUNK == 0 and lseq % GLA_BLOCK == 0 and lseq % GRID_W == 0 and lseq >= 3 * ATTN_BLOCK
    assert nlb + 1 <= SUBLANES and nkeys % SUBLANES == 0

    x0 = jnp.concatenate([x_prompt.reshape(n_ctx, d), x_sample.reshape(n_lat, d)], axis=0)
    cvec = jnp.zeros((SUBLANES, d), F32).at[0].set(c_ctx).at[1:1 + nlb].set(c)
    mods = _ada_mods(cvec, w_ada, b_ada)
    blk_row = np.concatenate([np.zeros(n_ctx // tm, np.int32),
                              1 + np.repeat(np.arange(nlb, dtype=np.int32), lseq // tm)])
    modblk = mods[:, blk_row, :].reshape(depth, n // tm, 6, d)
    cos, sin = _rope_tables(n_ctx, lseq, nlb)
    gla_consts = _gla_consts()
    hd = np.arange(GLA_WIDTH) // GLA_DV
    hmean = jnp.asarray((hd[:, None] == hd[None, :]).astype(np.float32) / GLA_DV)
    zero_state = jnp.zeros((nb, GLA_WIDTH, GLA_QK), F32)

    def layer(x, lp):
        (mod_l, n1, n2, w_in_l, pool_w_l, pool_scale_l, gw_f, gb_f, gw_b, gb_b, gnorm, sink, w_out_l,
         wq, subk, pu, pv, ck, cv, sf, sb) = lp
        w_big = _pack_w_in(w_in_l)
        w2 = jnp.zeros((LANES, 2 * GLA_QK), F32)
        w2 = w2.at[:GLA_GATE_RANK, :GLA_QK].set(gw_f).at[GLA_GATE_RANK:2 * GLA_GATE_RANK, GLA_QK:].set(gw_b)
        b2 = jnp.concatenate([gb_f, gb_b]).reshape(1, 2 * GLA_QK)
        pp, gq, gk, gv, gg, la_f, la_b, aq, ak, av = _inproj(x, mod_l, n1.reshape(1, d), w_big, cos, sin, w2, b2)

        w_bd = _block_diag(pool_w_l).astype(BF16)
        scale = pool_scale_l.reshape(1, POOL_WIDTH)
        y_pool = (_pool(pp, w_bd, scale, seq, 0, nb), _pool(pp, w_bd, scale, lseq, n_ctx // lseq, nlb))

        lblk = min(GLA_BLOCK, lseq)
        of_c, sf_c = _gla(gq, gk, gv, la_f, zero_state, gla_consts, seq, 0, nb, False)
        ob_c, sb_c = _gla(gq, gk, gv, la_b, zero_state, gla_consts, seq, 0, nb, True)
        of_l, _ = _gla(gq, gk, gv, la_f, _state_to_blockdiag_t(sf), gla_consts, lseq, n_ctx // lblk, nlb, False)
        ob_l, _ = _gla(gq, gk, gv, la_b, _state_to_blockdiag_t(sb), gla_consts, lseq, n_ctx // lblk, nlb, True)

        sink_b = jnp.broadcast_to(sink.reshape(ATTN_HEADS, 1), (ATTN_HEADS, LANES))
        y_attn = (_ctx_attn(aq, ak, av, sink_b, seq, nb), _lat_attn(aq, ak, av, ck, cv, sink_b, lseq, n_ctx, nlb))

        wqt = wq.T.astype(BF16)
        sk = subk.reshape(2 * PEER_HEADS, nkeys, PEER_HALF).astype(BF16)
        x1, h2t, s1, s2 = _outproj(y_pool, (of_c, of_l), (ob_c, ob_l), y_attn, gg, x, mod_l,
                                   gnorm.reshape(1, GLA_WIDTH), hmean, w_out_l.astype(BF16), n2.reshape(1, d),
                                   wqt, sk)
        cnt, a, rk, bx = _peer_gates(s1, s2)
        x2 = _peer_dense(h2t, pu.astype(BF16), pv.T.astype(BF16), cnt, a, rk, bx, x1, mod_l)
        return x2, (ak[:n_ctx], av[:n_ctx], _blockdiag_t_to_state(sf_c), _blockdiag_t_to_state(sb_c))

    past = cache_k.shape[2]
    xs = (modblk, norm1_g, norm2_g, w_in, pool_w, pool_scale, gla_gate_w_f, gla_gate_b_f, gla_gate_w_b,
          gla_gate_b_b, gla_norm_g, attn_sink, w_out, peer_wq, peer_subkeys, peer_u, peer_v,
          jnp.swapaxes(cache_k, 0, 1).reshape(depth, nlb, past, KV_WIDTH),
          jnp.swapaxes(cache_v, 0, 1).reshape(depth, nlb, past, KV_WIDTH),
          jnp.swapaxes(state_fwd, 0, 1), jnp.swapaxes(state_bwd, 0, 1))
    x_fin, (ks, vs, sfs, sbs) = lax.scan(layer, x0, xs)

    y = _final_norm(x_fin, final_norm_g)
    y_prompt = y[:n_ctx].reshape(nb, seq, d)
    y_sample = y[n_ctx:].reshape(nlb, lseq, d)
    new_k = jnp.swapaxes(ks.reshape(depth, nb, seq, ATTN_KV_HEADS, HEAD_DIM), 0, 1)
    new_v = jnp.swapaxes(vs.reshape(depth, nb, seq, ATTN_KV_HEADS, HEAD_DIM), 0, 1)
    return (y_prompt, y_sample, new_k, new_v, jnp.swapaxes(sfs, 0, 1), jnp.swapaxes(sbs, 0, 1))
```
